```python
import math
import jax, jax.numpy as jnp
from jax import lax
import numpy as np

D_MODEL = 2048
BATCH = 8
SEQ = 2048
DEPTH = 1

D_MIX = D_MODEL
SB_HEADS = 8
SB_HEAD_DIM = 128
SB_WIDTH = SB_HEADS * SB_HEAD_DIM
DF_HEADS = 8
DF_QK_DIM = 64
DF_V_DIM = 2 * DF_QK_DIM
DF_WIDTH = DF_HEADS * DF_V_DIM
ROPE_DIM = DF_QK_DIM // 4
ROPE_THETA = 500000.0
PLE_DIM = 256
Q_BLOCK = 128
NORM_EPS = 1e-6
SUBLN_EPS = 1e-5
LAMBDA_STD = 0.1
IN_SIZES = (SB_WIDTH, SB_WIDTH, SB_WIDTH, SB_WIDTH,
            DF_HEADS * 2 * DF_QK_DIM, DF_HEADS * 2 * DF_QK_DIM,
            DF_WIDTH, DF_WIDTH)
IN_COLS = sum(IN_SIZES)

kernel_name = "hybrid_stickbreak_diffattn_ple"


def rmsnorm(x, g, eps=NORM_EPS):
    xf = x.astype(jnp.float32)
    y = xf * lax.rsqrt(jnp.mean(xf * xf, axis=-1, keepdims=True) + eps)
    return (y * g.astype(jnp.float32)).astype(x.dtype)


def partial_rope(x, pos):
    half = ROPE_DIM // 2
    inv_freq = ROPE_THETA ** (-jnp.arange(0, ROPE_DIM, 2, dtype=jnp.float32) / ROPE_DIM)
    ang = pos.astype(jnp.float32)[:, None] * inv_freq[None, :]
    cos = jnp.cos(ang)[:, None, None, :].astype(x.dtype)
    sin = jnp.sin(ang)[:, None, None, :].astype(x.dtype)
    x1 = x[..., :half]
    x2 = x[..., half:ROPE_DIM]
    rot = jnp.concatenate([x1 * cos - x2 * sin, x2 * cos + x1 * sin], axis=-1)
    return jnp.concatenate([rot, x[..., ROPE_DIM:]], axis=-1)


def stick_breaking_attention(q, k, v):
    B, S, H, Dh = q.shape
    nb = S // Q_BLOCK
    scale = Dh ** -0.5
    kpos = jnp.arange(S)
    qb = q.reshape(B, nb, Q_BLOCK, H, Dh).transpose(1, 0, 2, 3, 4)

    def block(args):
        qi, bi = args
        z = jnp.einsum('bqhd,bkhd->bhqk', qi, k).astype(jnp.float32) * scale
        qpos = bi * Q_BLOCK + jnp.arange(Q_BLOCK)
        mask = kpos[None, :] < qpos[:, None]
        log_not_beta = jnp.where(mask, -jax.nn.softplus(z), 0.0)
        later = lax.cumsum(log_not_beta, axis=3, reverse=True) - log_not_beta
        w = jnp.where(mask, jnp.exp(jax.nn.log_sigmoid(z) + later), 0.0)
        return jnp.einsum('bhqk,bkhd->bqhd', w.astype(v.dtype), v)

    out = lax.map(block, (qb, jnp.arange(nb)))
    return out.transpose(1, 0, 2, 3, 4).reshape(B, S, H, Dh)


def differential_attention(q, k, v, lam):
    B, S, H, _, dq = q.shape
    nb = S // Q_BLOCK
    scale = dq ** -0.5
    kpos = jnp.arange(S)
    qb = q.reshape(B, nb, Q_BLOCK, H, 2, dq).transpose(1, 0, 2, 3, 4, 5)

    def block(args):
        qi, bi = args
        s = jnp.einsum('bqhcd,bkhcd->bhcqk', qi, k).astype(jnp.float32) * scale
        qpos = bi * Q_BLOCK + jnp.arange(Q_BLOCK)
        mask = kpos[None, :] <= qpos[:, None]
        pr = jax.nn.softmax(jnp.where(mask, s, -jnp.inf), axis=-1)
        a = pr[:, :, 0] - lam * pr[:, :, 1]
        return jnp.einsum('bhqk,bkhd->bqhd', a.astype(v.dtype), v)

    out = lax.map(block, (qb, jnp.arange(nb)))
    return out.transpose(1, 0, 2, 3, 4).reshape(B, S, H, v.shape[-1])


def setup_inputs(seed: int = 0) -> dict:
    key = jax.random.key(seed)
    ks = jax.random.split(key, 16)
    f32 = jnp.float32
    nrm = lambda k, shape, scale: jax.random.normal(k, shape, f32) * scale
    return {
        "x": nrm(ks[0], (BATCH, SEQ, D_MODEL), 1.0),
        "p": nrm(ks[1], (DEPTH, BATCH, SEQ, PLE_DIM), 1.0),
        "norm_mix_g": 1.0 + nrm(ks[2], (DEPTH, D_MODEL), 0.02),
        "w_in": nrm(ks[3], (DEPTH, D_MODEL, IN_COLS), D_MODEL ** -0.5),
        "lambda_q1": nrm(ks[4], (DEPTH, DF_QK_DIM), LAMBDA_STD),
        "lambda_k1": nrm(ks[5], (DEPTH, DF_QK_DIM), LAMBDA_STD),
        "lambda_q2": nrm(ks[6], (DEPTH, DF_QK_DIM), LAMBDA_STD),
        "lambda_k2": nrm(ks[7], (DEPTH, DF_QK_DIM), LAMBDA_STD),
        "subln_g": 1.0 + nrm(ks[8], (DEPTH, DF_V_DIM), 0.02),
        "w_out": nrm(ks[9], (DEPTH, D_MIX, D_MODEL), D_MIX ** -0.5),
        "norm_ple_g": 1.0 + nrm(ks[10], (DEPTH, D_MODEL), 0.02),
        "w_ple_gate": nrm(ks[11], (DEPTH, D_MODEL, D_MODEL), D_MODEL ** -0.5),
        "w_ple_proj": nrm(ks[12], (DEPTH, PLE_DIM, D_MODEL), PLE_DIM ** -0.5),
        "norm_final_g": 1.0 + nrm(ks[13], (D_MODEL,), 0.02),
    }


def reference(x, p, norm_mix_g, w_in, lambda_q1, lambda_k1, lambda_q2, lambda_k2, subln_g,
              w_out, norm_ple_g, w_ple_gate, w_ple_proj, norm_final_g):
    B, S, _ = x.shape
    pos = jnp.arange(S)
    split_at = [int(v) for v in np.cumsum(IN_SIZES)[:-1]]
    h = x
    for i in range(DEPTH):
        lambda_init = 0.8 - 0.6 * math.exp(-0.3 * i)
        hn = rmsnorm(h, norm_mix_g[i])
        proj = hn @ w_in[i]
        sb_q, sb_k, sb_v, sb_g, df_q, df_k, df_v, df_g = jnp.split(proj, split_at, axis=-1)

        sb_shape = (B, S, SB_HEADS, SB_HEAD_DIM)
        sb_out = stick_breaking_attention(sb_q.reshape(sb_shape), sb_k.reshape(sb_shape),
                                          sb_v.reshape(sb_shape)).reshape(B, S, SB_WIDTH)

        qk_shape = (B, S, DF_HEADS, 2, DF_QK_DIM)
        dq = partial_rope(df_q.reshape(qk_shape), pos)
        dk = partial_rope(df_k.reshape(qk_shape), pos)
        lam = (jnp.exp(jnp.sum(lambda_q1[i] * lambda_k1[i]).astype(jnp.float32))
               - jnp.exp(jnp.sum(lambda_q2[i] * lambda_k2[i]).astype(jnp.float32))
               + lambda_init)
        df_out = differential_attention(dq, dk, df_v.reshape(B, S, DF_HEADS, DF_V_DIM), lam)
        df_out = rmsnorm(df_out, subln_g[i], SUBLN_EPS) * (1.0 - lambda_init)
        df_out = df_out.reshape(B, S, DF_WIDTH)

        mixed = jnp.concatenate([sb_out * jax.nn.silu(sb_g), df_out * jax.nn.silu(df_g)], axis=-1)
        h = h + mixed @ w_out[i]

        gate = jax.nn.sigmoid(rmsnorm(h, norm_ple_g[i]) @ w_ple_gate[i])
        h = h + gate * (p[i] @ w_ple_proj[i])
    return rmsnorm(h, norm_final_g)
```

```python
import functools
import math

import jax
import jax.numpy as jnp
from jax import lax
from jax.experimental import pallas as pl
from jax.experimental.pallas import tpu as pltpu

F32 = jnp.float32
BF16 = jnp.bfloat16

SB_HEADS = 8
DF_HEADS = 8
HEAD_W = 128
DF_QK_DIM = 64
ROPE_DIM = DF_QK_DIM // 4
ROPE_THETA = 500000.0
NORM_EPS = 1e-6
SUBLN_EPS = 1e-5

SEC_W = 1024
SEC_SB_Q, SEC_SB_K, SEC_SB_V, SEC_SB_G, SEC_DF_Q, SEC_DF_K, SEC_DF_V, SEC_DF_G = range(8)

IN_PROJ_TM = 512
IN_PROJ_TN = SEC_W
ATTN_TQ = 256
ATTN_KB = 256
TAIL_TM = 256
VMEM_LIMIT = 56 * 1024 * 1024

_NT = (((1,), (1,)), ((), ()))


def _sigmoid(x):
    return 1.0 / (1.0 + jnp.exp(-x))


def _in_proj_kernel(x_ref, g_ref, w_ref, cos_ref, sa_ref, sb_ref, o_ref, hn_ref):
    j = pl.program_id(1)

    @pl.when(j == 0)
    def _():
        x = x_ref[...]
        ms = jnp.mean(x * x, axis=-1, keepdims=True)
        hn_ref[...] = ((x * lax.rsqrt(ms + NORM_EPS)) * g_ref[...]).astype(BF16)

    acc = jnp.dot(hn_ref[...], w_ref[...], preferred_element_type=F32)
    is_rope = jnp.logical_or(j == SEC_DF_Q, j == SEC_DF_K)

    @pl.when(is_rope)
    def _():
        c, sa, sb = cos_ref[...], sa_ref[...], sb_ref[...]
        for h in range(IN_PROJ_TN // HEAD_W):
            xh = acc[:, h * HEAD_W:(h + 1) * HEAD_W]
            rot = xh * c + pltpu.roll(xh, HEAD_W - ROPE_DIM // 2, 1) * sa + pltpu.roll(xh, ROPE_DIM // 2, 1) * sb
            o_ref[:, h * HEAD_W:(h + 1) * HEAD_W] = rot.astype(BF16)

    @pl.when(jnp.logical_not(is_rope))
    def _():
        o_ref[...] = acc.astype(BF16)


def _rope_tables(seq):
    half = ROPE_DIM // 2
    inv_freq = ROPE_THETA ** (-jnp.arange(0, ROPE_DIM, 2, dtype=F32) / ROPE_DIM)
    ang = jnp.arange(seq, dtype=F32)[:, None] * inv_freq[None, :]
    cos, sin = jnp.cos(ang), jnp.sin(ang)
    lane = jnp.arange(HEAD_W)
    within = lane % DF_QK_DIM
    freq = within % half
    first = within < half
    second = jnp.logical_and(within >= half, within < ROPE_DIM)
    cos_l, sin_l = cos[:, freq], sin[:, freq]
    c = jnp.where(jnp.logical_or(first, second)[None, :], cos_l, 1.0)
    sa = jnp.where(first[None, :], -sin_l, 0.0)
    sb = jnp.where(second[None, :], sin_l, 0.0)
    return c.astype(F32), sa.astype(F32), sb.astype(F32)


def _in_proj(x2d, g, w_bf16, seq):
    t, d = x2d.shape
    n = w_bf16.shape[1]
    tm, tn = IN_PROJ_TM, IN_PROJ_TN
    c, sa, sb = _rope_tables(seq)
    pos_blocks = seq // tm
    tab_spec = pl.BlockSpec((tm, HEAD_W), lambda i, j: (i % pos_blocks, 0))
    return pl.pallas_call(
        _in_proj_kernel,
        grid=(t // tm, n // tn),
        in_specs=[
            pl.BlockSpec((tm, d), lambda i, j: (i, 0)),
            pl.BlockSpec((1, d), lambda i, j: (0, 0)),
            pl.BlockSpec((d, tn), lambda i, j: (0, j)),
            tab_spec, tab_spec, tab_spec,
        ],
        out_specs=pl.BlockSpec((tm, tn), lambda i, j: (i, j)),
        out_shape=jax.ShapeDtypeStruct((t, n), BF16),
        scratch_shapes=[pltpu.VMEM((tm, d), BF16)],
        compiler_params=pltpu.CompilerParams(
            dimension_semantics=("arbitrary", "arbitrary"), vmem_limit_bytes=VMEM_LIMIT),
        name="in_proj",
    )(x2d, g.reshape(1, d), w_bf16, c, sa, sb)


def _fill_vt(v_ref, vt_ref, kb):
    r = lax.broadcasted_iota(jnp.int32, (HEAD_W, HEAD_W), 0)
    c = lax.broadcasted_iota(jnp.int32, (HEAD_W, HEAD_W), 1)
    eye = jnp.where(r == c, 1.0, 0.0).astype(BF16)
    for blk in range(vt_ref.shape[0]):
        vb = v_ref[0, blk * kb:(blk + 1) * kb, :]
        vt_ref[blk] = lax.dot_general(eye, vb, _NT, preferred_element_type=F32).astype(BF16)


def _sb_attn_kernel(q_ref, k_ref, v_ref, g_ref, o_ref, vt_ref, acc_ref, carry_ref, *, tq, kb, scale):
    qi = pl.program_id(2)

    @pl.when(qi == 0)
    def _():
        _fill_vt(v_ref, vt_ref, kb)

    q = q_ref[0]
    row = lax.broadcasted_iota(jnp.int32, (kb, kb), 0)
    col = lax.broadcasted_iota(jnp.int32, (kb, kb), 1)
    upper = jnp.where(col > row, 1.0, 0.0).astype(BF16)
    valid = lax.broadcasted_iota(jnp.int32, (kb, tq), 0) < lax.broadcasted_iota(jnp.int32, (kb, tq), 1)

    acc_ref[...] = jnp.zeros_like(acc_ref)
    carry_ref[...] = jnp.zeros_like(carry_ref)

    def tile(j, masked):
        ks = pl.multiple_of(j * kb, kb)
        kblk = k_ref[0, pl.ds(ks, kb), :]
        z = lax.dot_general(kblk, q, _NT, preferred_element_type=F32) * scale
        lse = jnp.log(1.0 + jnp.exp(-jnp.abs(z)))
        sp = jnp.maximum(z, 0.0) + lse
        if masked:
            sp = jnp.where(valid, sp, 0.0)
        hi = sp.astype(BF16)
        lo = (sp - hi.astype(F32)).astype(BF16)
        cs = (jnp.dot(upper, hi, preferred_element_type=F32)
              + jnp.dot(upper, lo, preferred_element_type=F32))
        w = jnp.exp(jnp.minimum(z, 0.0) - lse - cs - carry_ref[...])
        if masked:
            w = jnp.where(valid, w, 0.0)
        acc_ref[...] += jnp.dot(vt_ref[j], w.astype(BF16), preferred_element_type=F32)
        carry_ref[...] += jnp.sum(sp, axis=0, keepdims=True)

    tile(qi, True)

    def body(i, _):
        tile(qi - 1 - i, False)
        return 0

    lax.fori_loop(0, qi, body, 0)

    g = g_ref[0].astype(F32)
    o_ref[0] = (acc_ref[...].T * (g * _sigmoid(g))).astype(BF16)


def _sb_attn(proj3d):
    b, s, _ = proj3d.shape
    tq, kb = ATTN_TQ, ATTN_KB
    assert tq == kb
    nsec = SEC_W // HEAD_W
    kern = functools.partial(_sb_attn_kernel, tq=tq, kb=kb, scale=HEAD_W ** -0.5)
    return pl.pallas_call(
        kern,
        grid=(b, SB_HEADS, s // tq),
        in_specs=[
            pl.BlockSpec((1, tq, HEAD_W), lambda bi, h, qi: (bi, qi, SEC_SB_Q * nsec + h)),
            pl.BlockSpec((1, s, HEAD_W), lambda bi, h, qi: (bi, 0, SEC_SB_K * nsec + h)),
            pl.BlockSpec((1, s, HEAD_W), lambda bi, h, qi: (bi, 0, SEC_SB_V * nsec + h)),
            pl.BlockSpec((1, tq, HEAD_W), lambda bi, h, qi: (bi, qi, SEC_SB_G * nsec + h)),
        ],
        out_specs=pl.BlockSpec((1, tq, HEAD_W), lambda bi, h, qi: (bi, qi, h)),
        out_shape=jax.ShapeDtypeStruct((b, s, SB_HEADS * HEAD_W), BF16),
        scratch_shapes=[
            pltpu.VMEM((s // kb, HEAD_W, kb), BF16),
            pltpu.VMEM((HEAD_W, tq), F32),
            pltpu.VMEM((1, tq), F32),
        ],
        compiler_params=pltpu.CompilerParams(
            dimension_semantics=("arbitrary", "arbitrary", "arbitrary"), vmem_limit_bytes=VMEM_LIMIT),
        name="sb_attn",
    )(proj3d, proj3d, proj3d, proj3d)


def _df_attn_kernel(lam_ref, sg_ref, q_ref, k_ref, v_ref, g_ref, o_ref, vt_ref, acc_ref, m_ref, l_ref,
                    *, tq, kb, lambda_init):
    qi = pl.program_id(2)

    @pl.when(qi == 0)
    def _():
        _fill_vt(v_ref, vt_ref, kb)

    qf = q_ref[0].astype(F32) * (DF_QK_DIM ** -0.5)
    lane = lax.broadcasted_iota(jnp.int32, (tq, HEAD_W), 1)
    qs = (jnp.where(lane < DF_QK_DIM, qf, 0.0).astype(BF16),
          jnp.where(lane >= DF_QK_DIM, qf, 0.0).astype(BF16))
    valid = lax.broadcasted_iota(jnp.int32, (kb, tq), 0) <= lax.broadcasted_iota(jnp.int32, (kb, tq), 1)

    acc_ref[...] = jnp.zeros_like(acc_ref)
    l_ref[...] = jnp.zeros_like(l_ref)
    m_ref[...] = jnp.full_like(m_ref, -jnp.inf)

    def tile(j, masked):
        ks = pl.multiple_of(j * kb, kb)
        kblk = k_ref[0, pl.ds(ks, kb), :]
        vt = vt_ref[j]
        for c in range(2):
            s = lax.dot_general(kblk, qs[c], _NT, preferred_element_type=F32)
            if masked:
                s = jnp.where(valid, s, -jnp.inf)
            m_old = m_ref[c]
            m_new = jnp.maximum(m_old, jnp.max(s, axis=0, keepdims=True))
            alpha = jnp.exp(m_old - m_new)
            p = jnp.exp(s - m_new)
            l_ref[c] = alpha * l_ref[c] + jnp.sum(p, axis=0, keepdims=True)
            acc_ref[c] = alpha * acc_ref[c] + jnp.dot(vt, p.astype(BF16), preferred_element_type=F32)
            m_ref[c] = m_new

    def body(j, _):
        tile(j, False)
        return 0

    lax.fori_loop(0, qi, body, 0)
    tile(qi, True)

    lam_p = lam_ref[...]
    lam = (jnp.exp(jnp.sum(lam_p[0:1] * lam_p[1:2], axis=-1, keepdims=True))
           - jnp.exp(jnp.sum(lam_p[2:3] * lam_p[3:4], axis=-1, keepdims=True))
           + lambda_init)
    o = acc_ref[0] * (1.0 / l_ref[0]) - lam * (acc_ref[1] * (1.0 / l_ref[1]))
    ms = jnp.mean(o * o, axis=0, keepdims=True)
    on = (o * lax.rsqrt(ms + SUBLN_EPS)).T
    g = g_ref[0].astype(F32)
    o_ref[0] = (((on * sg_ref[...]) * (1.0 - lambda_init)) * (g * _sigmoid(g))).astype(BF16)


def _df_attn(proj3d, lam_params, subln_g, lambda_init):
    b, s, _ = proj3d.shape
    tq, kb = ATTN_TQ, ATTN_KB
    assert tq == kb
    nsec = SEC_W // HEAD_W
    kern = functools.partial(_df_attn_kernel, tq=tq, kb=kb, lambda_init=lambda_init)
    return pl.pallas_call(
        kern,
        grid=(b, DF_HEADS, s // tq),
        in_specs=[
            pl.BlockSpec(lam_params.shape, lambda bi, h, qi: (0, 0)),
            pl.BlockSpec((1, HEAD_W), lambda bi, h, qi: (0, 0)),
            pl.BlockSpec((1, tq, HEAD_W), lambda bi, h, qi: (bi, qi, SEC_DF_Q * nsec + h)),
            pl.BlockSpec((1, s, HEAD_W), lambda bi, h, qi: (bi, 0, SEC_DF_K * nsec + h)),
            pl.BlockSpec((1, s, HEAD_W), lambda bi, h, qi: (bi, 0, SEC_DF_V * nsec + h)),
            pl.BlockSpec((1, tq, HEAD_W), lambda bi, h, qi: (bi, qi, SEC_DF_G * nsec + h)),
        ],
        out_specs=pl.BlockSpec((1, tq, HEAD_W), lambda bi, h, qi: (bi, qi, h)),
        out_shape=jax.ShapeDtypeStruct((b, s, DF_HEADS * HEAD_W), BF16),
        scratch_shapes=[
            pltpu.VMEM((s // kb, HEAD_W, kb), BF16),
            pltpu.VMEM((2, HEAD_W, tq), F32),
            pltpu.VMEM((2, 1, tq), F32),
            pltpu.VMEM((2, 1, tq), F32),
        ],
        compiler_params=pltpu.CompilerParams(
            dimension_semantics=("arbitrary", "arbitrary", "arbitrary"), vmem_limit_bytes=VMEM_LIMIT),
        name="df_attn",
    )(lam_params, subln_g.reshape(1, HEAD_W), proj3d, proj3d, proj3d, proj3d)


def _tail_kernel(x_ref, msb_ref, mdf_ref, p_ref, wo_sb_ref, wo_df_ref, gp_ref, wg_ref, wp_ref, gf_ref, o_ref,
                 *, final_norm):
    h = (x_ref[...]
         + jnp.dot(msb_ref[...], wo_sb_ref[...], preferred_element_type=F32)
         + jnp.dot(mdf_ref[...], wo_df_ref[...], preferred_element_type=F32))
    ms = jnp.mean(h * h, axis=-1, keepdims=True)
    hn = ((h * lax.rsqrt(ms + NORM_EPS)) * gp_ref[...]).astype(BF16)
    gate = _sigmoid(jnp.dot(hn, wg_ref[...], preferred_element_type=F32))
    pp = jnp.dot(p_ref[...].astype(BF16), wp_ref[...], preferred_element_type=F32)
    h = h + gate * pp
    if final_norm:
        ms = jnp.mean(h * h, axis=-1, keepdims=True)
        h = (h * lax.rsqrt(ms + NORM_EPS)) * gf_ref[...]
    o_ref[...] = h


def _tail(x2d, msb, mdf, p2d, w_out, g_ple, w_gate, w_ple, g_final, final_norm):
    t, d = x2d.shape
    tm = TAIL_TM
    dsb, ddf, dp = msb.shape[1], mdf.shape[1], p2d.shape[1]
    wo_sb, wo_df = w_out[:dsb], w_out[dsb:]
    rows = lambda width: pl.BlockSpec((tm, width), lambda i: (i, 0))
    whole = lambda a: pl.BlockSpec(a.shape, lambda i: (0, 0), pipeline_mode=pl.Buffered(1))
    g_ple, g_final = g_ple.reshape(1, d), g_final.reshape(1, d)
    return pl.pallas_call(
        functools.partial(_tail_kernel, final_norm=final_norm),
        grid=(t // tm,),
        in_specs=[rows(d), rows(dsb), rows(ddf), rows(dp),
                  whole(wo_sb), whole(wo_df), whole(g_ple), whole(w_gate), whole(w_ple), whole(g_final)],
        out_specs=rows(d),
        out_shape=jax.ShapeDtypeStruct((t, d), F32),
        compiler_params=pltpu.CompilerParams(
            dimension_semantics=("arbitrary",), vmem_limit_bytes=VMEM_LIMIT),
        name="tail",
    )(x2d, msb, mdf, p2d, wo_sb, wo_df, g_ple, w_gate, w_ple, g_final)


def kernel(x, p, norm_mix_g, w_in, lambda_q1, lambda_k1, lambda_q2, lambda_k2, subln_g, w_out, norm_ple_g,
           w_ple_gate, w_ple_proj, norm_final_g):
    b, s, d = x.shape
    depth = w_in.shape[0]
    h = x.reshape(b * s, d)
    for i in range(depth):
        lambda_init = 0.8 - 0.6 * math.exp(-0.3 * i)
        proj = _in_proj(h, norm_mix_g[i], w_in[i].astype(BF16), s).reshape(b, s, -1)
        msb = _sb_attn(proj).reshape(b * s, -1)
        lam_params = jnp.stack([lambda_q1[i], lambda_k1[i], lambda_q2[i], lambda_k2[i]])
        mdf = _df_attn(proj, lam_params, subln_g[i], lambda_init).reshape(b * s, -1)
        h = _tail(h, msb, mdf, p[i].reshape(b * s, -1), w_out[i].astype(BF16), norm_ple_g[i],
                  w_ple_gate[i].astype(BF16), w_ple_proj[i].astype(BF16), norm_final_g,
                  final_norm=(i == depth - 1))
    return h.reshape(b, s, d)
```

```python
import functools
import math

import jax
import jax.numpy as jnp
from jax import lax
from jax.experimental import pallas as pl
from jax.experimental.pallas import tpu as pltpu

F32 = jnp.float32
BF16 = jnp.bfloat16

SB_HEADS = 8
DF_HEADS = 8
HEAD_W = 128
DF_QK_DIM = 64
ROPE_DIM = DF_QK_DIM // 4
ROPE_THETA = 500000.0
NORM_EPS = 1e-6
SUBLN_EPS = 1e-5

SEC_W = 1024
SEC_SB_Q, SEC_SB_K, SEC_SB_V, SEC_SB_G, SEC_DF_Q, SEC_DF_K, SEC_DF_V, SEC_DF_G = range(8)

IN_PROJ_TM = 512
IN_PROJ_TN = SEC_W
ATTN_TQ = 1024
ATTN_KB = 256
TAIL_TM = 256
VMEM_LIMIT = 56 * 1024 * 1024

_NT = (((1,), (1,)), ((), ()))


def _sigmoid(x):
    return 1.0 / (1.0 + jnp.exp(-x))


def _in_proj_kernel(x_ref, g_ref, w_ref, cos_ref, sa_ref, sb_ref, o_ref, hn_ref):
    j = pl.program_id(1)

    @pl.when(j == 0)
    def _():
        x = x_ref[...]
        ms = jnp.mean(x * x, axis=-1, keepdims=True)
        hn_ref[...] = ((x * lax.rsqrt(ms + NORM_EPS)) * g_ref[...]).astype(BF16)

    acc = jnp.dot(hn_ref[...], w_ref[...], preferred_element_type=F32)
    is_rope = jnp.logical_or(j == SEC_DF_Q, j == SEC_DF_K)

    @pl.when(is_rope)
    def _():
        c, sa, sb = cos_ref[...], sa_ref[...], sb_ref[...]
        for h in range(IN_PROJ_TN // HEAD_W):
            xh = acc[:, h * HEAD_W:(h + 1) * HEAD_W]
            rot = xh * c + pltpu.roll(xh, HEAD_W - ROPE_DIM // 2, 1) * sa + pltpu.roll(xh, ROPE_DIM // 2, 1) * sb
            o_ref[:, h * HEAD_W:(h + 1) * HEAD_W] = rot.astype(BF16)

    @pl.when(jnp.logical_not(is_rope))
    def _():
        o_ref[...] = acc.astype(BF16)


def _rope_tables(seq):
    half = ROPE_DIM // 2
    inv_freq = ROPE_THETA ** (-jnp.arange(0, ROPE_DIM, 2, dtype=F32) / ROPE_DIM)
    ang = jnp.arange(seq, dtype=F32)[:, None] * inv_freq[None, :]
    cos, sin = jnp.cos(ang), jnp.sin(ang)
    lane = jnp.arange(HEAD_W)
    within = lane % DF_QK_DIM
    freq = within % half
    first = within < half
    second = jnp.logical_and(within >= half, within < ROPE_DIM)
    cos_l, sin_l = cos[:, freq], sin[:, freq]
    c = jnp.where(jnp.logical_or(first, second)[None, :], cos_l, 1.0)
    sa = jnp.where(first[None, :], -sin_l, 0.0)
    sb = jnp.where(second[None, :], sin_l, 0.0)
    return c.astype(F32), sa.astype(F32), sb.astype(F32)


def _in_proj(x2d, g, w_bf16, seq):
    t, d = x2d.shape
    n = w_bf16.shape[1]
    tm, tn = IN_PROJ_TM, IN_PROJ_TN
    c, sa, sb = _rope_tables(seq)
    pos_blocks = seq // tm
    tab_spec = pl.BlockSpec((tm, HEAD_W), lambda i, j: (i % pos_blocks, 0))
    return pl.pallas_call(
        _in_proj_kernel,
        grid=(t // tm, n // tn),
        in_specs=[
            pl.BlockSpec((tm, d), lambda i, j: (i, 0)),
            pl.BlockSpec((1, d), lambda i, j: (0, 0)),
            pl.BlockSpec((d, tn), lambda i, j: (0, j)),
            tab_spec, tab_spec, tab_spec,
        ],
        out_specs=pl.BlockSpec((tm, tn), lambda i, j: (i, j)),
        out_shape=jax.ShapeDtypeStruct((t, n), BF16),
        scratch_shapes=[pltpu.VMEM((tm, d), BF16)],
        compiler_params=pltpu.CompilerParams(
            dimension_semantics=("arbitrary", "arbitrary"), vmem_limit_bytes=VMEM_LIMIT),
        name="in_proj",
    )(x2d, g.reshape(1, d), w_bf16, c, sa, sb)


def _fill_vt(v_ref, vt_ref, kb):
    r = lax.broadcasted_iota(jnp.int32, (HEAD_W, HEAD_W), 0)
    c = lax.broadcasted_iota(jnp.int32, (HEAD_W, HEAD_W), 1)
    eye = jnp.where(r == c, 1.0, 0.0).astype(BF16)
    for blk in range(vt_ref.shape[0]):
        vb = v_ref[0, blk * kb:(blk + 1) * kb, :]
        vt_ref[blk] = lax.dot_general(eye, vb, _NT, preferred_element_type=F32).astype(BF16)


def _sb_attn_kernel(q_ref, k_ref, v_ref, g_ref, o_ref, vt_ref, acc_ref, carry_ref, *, tq, kb, scale):
    qi = pl.program_id(2)
    groups = tq // kb

    @pl.when(qi == 0)
    def _():
        _fill_vt(v_ref, vt_ref, kb)

    row = lax.broadcasted_iota(jnp.int32, (kb, kb), 0)
    col = lax.broadcasted_iota(jnp.int32, (kb, kb), 1)
    upper = jnp.where(col > row, 1.0, 0.0).astype(BF16)

    acc_ref[...] = jnp.zeros_like(acc_ref)
    carry_ref[...] = jnp.zeros_like(carry_ref)

    def tile(j, lane_lo, masked):
        width = tq - lane_lo
        ks = pl.multiple_of(j * kb, kb)
        kblk = k_ref[0, pl.ds(ks, kb), :]
        z = lax.dot_general(kblk, q_ref[0, lane_lo:, :], _NT, preferred_element_type=F32) * scale
        lse = jnp.log(1.0 + jnp.exp(-jnp.abs(z)))
        sp = jnp.maximum(z, 0.0) + lse
        if masked:
            valid = (lax.broadcasted_iota(jnp.int32, (kb, width), 0)
                     < lax.broadcasted_iota(jnp.int32, (kb, width), 1))
            sp = jnp.where(valid, sp, 0.0)
        hi = sp.astype(BF16)
        lo = (sp - hi.astype(F32)).astype(BF16)
        cs = (jnp.dot(upper, hi, preferred_element_type=F32)
              + jnp.dot(upper, lo, preferred_element_type=F32))
        w = jnp.exp(jnp.minimum(z, 0.0) - lse - cs - carry_ref[:, lane_lo:])
        if masked:
            w = jnp.where(valid, w, 0.0)
        acc_ref[:, lane_lo:] += jnp.dot(vt_ref[j], w.astype(BF16), preferred_element_type=F32)
        carry_ref[:, lane_lo:] += jnp.sum(sp, axis=0, keepdims=True)

    for d in reversed(range(groups)):
        tile(qi * groups + d, d * kb, True)

    def body(i, _):
        tile(qi * groups - 1 - i, 0, False)
        return 0

    lax.fori_loop(0, qi * groups, body, 0)

    g = g_ref[0].astype(F32)
    o_ref[0] = (acc_ref[...].T * (g * _sigmoid(g))).astype(BF16)


def _sb_attn(proj3d):
    b, s, _ = proj3d.shape
    tq, kb = ATTN_TQ, ATTN_KB
    assert tq % kb == 0
    nsec = SEC_W // HEAD_W
    kern = functools.partial(_sb_attn_kernel, tq=tq, kb=kb, scale=HEAD_W ** -0.5)
    return pl.pallas_call(
        kern,
        grid=(b, SB_HEADS, s // tq),
        in_specs=[
            pl.BlockSpec((1, tq, HEAD_W), lambda bi, h, qi: (bi, qi, SEC_SB_Q * nsec + h)),
            pl.BlockSpec((1, s, HEAD_W), lambda bi, h, qi: (bi, 0, SEC_SB_K * nsec + h)),
            pl.BlockSpec((1, s, HEAD_W), lambda bi, h, qi: (bi, 0, SEC_SB_V * nsec + h)),
            pl.BlockSpec((1, tq, HEAD_W), lambda bi, h, qi: (bi, qi, SEC_SB_G * nsec + h)),
        ],
        out_specs=pl.BlockSpec((1, tq, HEAD_W), lambda bi, h, qi: (bi, qi, h)),
        out_shape=jax.ShapeDtypeStruct((b, s, SB_HEADS * HEAD_W), BF16),
        scratch_shapes=[
            pltpu.VMEM((s // kb, HEAD_W, kb), BF16),
            pltpu.VMEM((HEAD_W, tq), F32),
            pltpu.VMEM((1, tq), F32),
        ],
        compiler_params=pltpu.CompilerParams(
            dimension_semantics=("arbitrary", "arbitrary", "arbitrary"), vmem_limit_bytes=VMEM_LIMIT),
        name="sb_attn",
    )(proj3d, proj3d, proj3d, proj3d)


def _df_attn_kernel(lam_ref, sg_ref, q_ref, k_ref, v_ref, g_ref, o_ref, vt_ref, qs_ref, acc_ref, m_ref, l_ref,
                    *, tq, kb, lambda_init):
    qi = pl.program_id(2)
    groups = tq // kb

    @pl.when(qi == 0)
    def _():
        _fill_vt(v_ref, vt_ref, kb)

    qf = q_ref[0].astype(F32) * (DF_QK_DIM ** -0.5)
    lane = lax.broadcasted_iota(jnp.int32, (tq, HEAD_W), 1)
    qs_ref[0] = jnp.where(lane < DF_QK_DIM, qf, 0.0).astype(BF16)
    qs_ref[1] = jnp.where(lane >= DF_QK_DIM, qf, 0.0).astype(BF16)

    acc_ref[...] = jnp.zeros_like(acc_ref)
    l_ref[...] = jnp.zeros_like(l_ref)
    m_ref[...] = jnp.full_like(m_ref, -jnp.inf)

    def tile(j, lane_lo, masked):
        width = tq - lane_lo
        ks = pl.multiple_of(j * kb, kb)
        kblk = k_ref[0, pl.ds(ks, kb), :]
        vt = vt_ref[j]
        for c in range(2):
            s = lax.dot_general(kblk, qs_ref[c, lane_lo:, :], _NT, preferred_element_type=F32)
            if masked:
                valid = (lax.broadcasted_iota(jnp.int32, (kb, width), 0)
                         <= lax.broadcasted_iota(jnp.int32, (kb, width), 1))
                s = jnp.where(valid, s, -jnp.inf)
            m_old = m_ref[c, :, lane_lo:]
            m_new = jnp.maximum(m_old, jnp.max(s, axis=0, keepdims=True))
            alpha = jnp.exp(m_old - m_new)
            p = jnp.exp(s - m_new)
            l_ref[c, :, lane_lo:] = alpha * l_ref[c, :, lane_lo:] + jnp.sum(p, axis=0, keepdims=True)
            acc_ref[c, :, lane_lo:] = (alpha * acc_ref[c, :, lane_lo:]
                                       + jnp.dot(vt, p.astype(BF16), preferred_element_type=F32))
            m_ref[c, :, lane_lo:] = m_new

    def body(j, _):
        tile(j, 0, False)
        return 0

    lax.fori_loop(0, qi * groups, body, 0)
    for d in range(groups):
        tile(qi * groups + d, d * kb, True)

    lam_p = lam_ref[...]
    lam = (jnp.exp(jnp.sum(lam_p[0:1] * lam_p[1:2], axis=-1, keepdims=True))
           - jnp.exp(jnp.sum(lam_p[2:3] * lam_p[3:4], axis=-1, keepdims=True))
           + lambda_init)
    o = acc_ref[0] * (1.0 / l_ref[0]) - lam * (acc_ref[1] * (1.0 / l_ref[1]))
    ms = jnp.mean(o * o, axis=0, keepdims=True)
    on = (o * lax.rsqrt(ms + SUBLN_EPS)).T
    g = g_ref[0].astype(F32)
    o_ref[0] = (((on * sg_ref[...]) * (1.0 - lambda_init)) * (g * _sigmoid(g))).astype(BF16)


def _df_attn(proj3d, lam_params, subln_g, lambda_init):
    b, s, _ = proj3d.shape
    tq, kb = ATTN_TQ, ATTN_KB
    assert tq % kb == 0
    nsec = SEC_W // HEAD_W
    kern = functools.partial(_df_attn_kernel, tq=tq, kb=kb, lambda_init=lambda_init)
    return pl.pallas_call(
        kern,
        grid=(b, DF_HEADS, s // tq),
        in_specs=[
            pl.BlockSpec(lam_params.shape, lambda bi, h, qi: (0, 0)),
            pl.BlockSpec((1, HEAD_W), lambda bi, h, qi: (0, 0)),
            pl.BlockSpec((1, tq, HEAD_W), lambda bi, h, qi: (bi, qi, SEC_DF_Q * nsec + h)),
            pl.BlockSpec((1, s, HEAD_W), lambda bi, h, qi: (bi, 0, SEC_DF_K * nsec + h)),
            pl.BlockSpec((1, s, HEAD_W), lambda bi, h, qi: (bi, 0, SEC_DF_V * nsec + h)),
            pl.BlockSpec((1, tq, HEAD_W), lambda bi, h, qi: (bi, qi, SEC_DF_G * nsec + h)),
        ],
        out_specs=pl.BlockSpec((1, tq, HEAD_W), lambda bi, h, qi: (bi, qi, h)),
        out_shape=jax.ShapeDtypeStruct((b, s, DF_HEADS * HEAD_W), BF16),
        scratch_shapes=[
            pltpu.VMEM((s // kb, HEAD_W, kb), BF16),
            pltpu.VMEM((2, tq, HEAD_W), BF16),
            pltpu.VMEM((2, HEAD_W, tq), F32),
            pltpu.VMEM((2, 1, tq), F32),
            pltpu.VMEM((2, 1, tq), F32),
        ],
        compiler_params=pltpu.CompilerParams(
            dimension_semantics=("arbitrary", "arbitrary", "arbitrary"), vmem_limit_bytes=VMEM_LIMIT),
        name="df_attn",
    )(lam_params, subln_g.reshape(1, HEAD_W), proj3d, proj3d, proj3d, proj3d)


def _tail_kernel(x_ref, msb_ref, mdf_ref, p_ref, wo_sb_ref, wo_df_ref, gp_ref, wg_ref, wp_ref, gf_ref, o_ref,
                 *, final_norm):
    h = (x_ref[...]
         + jnp.dot(msb_ref[...], wo_sb_ref[...], preferred_element_type=F32)
         + jnp.dot(mdf_ref[...], wo_df_ref[...], preferred_element_type=F32))
    ms = jnp.mean(h * h, axis=-1, keepdims=True)
    hn = ((h * lax.rsqrt(ms + NORM_EPS)) * gp_ref[...]).astype(BF16)
    gate = _sigmoid(jnp.dot(hn, wg_ref[...], preferred_element_type=F32))
    pp = jnp.dot(p_ref[...].astype(BF16), wp_ref[...], preferred_element_type=F32)
    h = h + gate * pp
    if final_norm:
        ms = jnp.mean(h * h, axis=-1, keepdims=True)
        h = (h * lax.rsqrt(ms + NORM_EPS)) * gf_ref[...]
    o_ref[...] = h


def _tail(x2d, msb, mdf, p2d, w_out, g_ple, w_gate, w_ple, g_final, final_norm):
    t, d = x2d.shape
    tm = TAIL_TM
    dsb, ddf, dp = msb.shape[1], mdf.shape[1], p2d.shape[1]
    wo_sb, wo_df = w_out[:dsb], w_out[dsb:]
    rows = lambda width: pl.BlockSpec((tm, width), lambda i: (i, 0))
    whole = lambda a: pl.BlockSpec(a.shape, lambda i: (0, 0), pipeline_mode=pl.Buffered(1))
    g_ple, g_final = g_ple.reshape(1, d), g_final.reshape(1, d)
    return pl.pallas_call(
        functools.partial(_tail_kernel, final_norm=final_norm),
        grid=(t // tm,),
        in_specs=[rows(d), rows(dsb), rows(ddf), rows(dp),
                  whole(wo_sb), whole(wo_df), whole(g_ple), whole(w_gate), whole(w_ple), whole(g_final)],
        out_specs=rows(d),
        out_shape=jax.ShapeDtypeStruct((t, d), F32),
        compiler_params=pltpu.CompilerParams(
            dimension_semantics=("arbitrary",), vmem_limit_bytes=VMEM_LIMIT),
        name="tail",
    )(x2d, msb, mdf, p2d, wo_sb, wo_df, g_ple, w_gate, w_ple, g_final)


def kernel(x, p, norm_mix_g, w_in, lambda_q1, lambda_k1, lambda_q2, lambda_k2, subln_g, w_out, norm_ple_g,
           w_ple_gate, w_ple_proj, norm_final_g):
    b, s, d = x.shape
    depth = w_in.shape[0]
    h = x.reshape(b * s, d)
    for i in range(depth):
        lambda_init = 0.8 - 0.6 * math.exp(-0.3 * i)
        proj = _in_proj(h, norm_mix_g[i], w_in[i].astype(BF16), s).reshape(b, s, -1)
        msb = _sb_attn(proj).reshape(b * s, -1)
        lam_params = jnp.stack([lambda_q1[i], lambda_k1[i], lambda_q2[i], lambda_k2[i]])
        mdf = _df_attn(proj, lam_params, subln_g[i], lambda_init).reshape(b * s, -1)
        h = _tail(h, msb, mdf, p[i].reshape(b * s, -1), w_out[i].astype(BF16), norm_ple_g[i],
                  w_ple_gate[i].astype(BF16), w_ple_proj[i].astype(BF16), norm_final_g,
                  final_norm=(i == depth - 1))
    return h.reshape(b, s, d)
```

```python
import functools
import math

import jax
import jax.numpy as jnp
from jax import lax
from jax.experimental import pallas as pl
from jax.experimental.pallas import tpu as pltpu

F32 = jnp.float32
BF16 = jnp.bfloat16

SB_HEADS = 8
DF_HEADS = 8
HEAD_W = 128
DF_QK_DIM = 64
ROPE_DIM = DF_QK_DIM // 4
ROPE_THETA = 500000.0
NORM_EPS = 1e-6
SUBLN_EPS = 1e-5

SEC_W = 1024
SEC_SB_Q, SEC_SB_K, SEC_SB_V, SEC_SB_G, SEC_DF_Q, SEC_DF_K, SEC_DF_V, SEC_DF_G = range(8)

IN_PROJ_TM = 512
IN_PROJ_TN = SEC_W
ATTN_TQ = 2048
ATTN_KB = 256
TAIL_TM = 256
VMEM_LIMIT = 56 * 1024 * 1024

_NT = (((1,), (1,)), ((), ()))
_LOG2E = math.log2(math.e)


def _sigmoid(x):
    return 1.0 / (1.0 + jnp.exp(-x))


def _in_proj_kernel(x_ref, g_ref, w_ref, cos_ref, sa_ref, sb_ref, o_ref, hn_ref):
    j = pl.program_id(1)

    @pl.when(j == 0)
    def _():
        x = x_ref[...]
        ms = jnp.mean(x * x, axis=-1, keepdims=True)
        hn_ref[...] = ((x * lax.rsqrt(ms + NORM_EPS)) * g_ref[...]).astype(BF16)

    acc = jnp.dot(hn_ref[...], w_ref[...], preferred_element_type=F32)
    is_rope = jnp.logical_or(j == SEC_DF_Q, j == SEC_DF_K)

    @pl.when(is_rope)
    def _():
        c, sa, sb = cos_ref[...], sa_ref[...], sb_ref[...]
        for h in range(IN_PROJ_TN // HEAD_W):
            xh = acc[:, h * HEAD_W:(h + 1) * HEAD_W]
            rot = xh * c + pltpu.roll(xh, HEAD_W - ROPE_DIM // 2, 1) * sa + pltpu.roll(xh, ROPE_DIM // 2, 1) * sb
            o_ref[:, h * HEAD_W:(h + 1) * HEAD_W] = rot.astype(BF16)

    @pl.when(jnp.logical_not(is_rope))
    def _():
        o_ref[...] = acc.astype(BF16)


def _rope_tables(seq):
    half = ROPE_DIM // 2
    inv_freq = ROPE_THETA ** (-jnp.arange(0, ROPE_DIM, 2, dtype=F32) / ROPE_DIM)
    ang = jnp.arange(seq, dtype=F32)[:, None] * inv_freq[None, :]
    cos, sin = jnp.cos(ang), jnp.sin(ang)
    lane = jnp.arange(HEAD_W)
    within = lane % DF_QK_DIM
    freq = within % half
    first = within < half
    second = jnp.logical_and(within >= half, within < ROPE_DIM)
    cos_l, sin_l = cos[:, freq], sin[:, freq]
    c = jnp.where(jnp.logical_or(first, second)[None, :], cos_l, 1.0)
    sa = jnp.where(first[None, :], -sin_l, 0.0)
    sb = jnp.where(second[None, :], sin_l, 0.0)
    return c.astype(F32), sa.astype(F32), sb.astype(F32)


def _in_proj(x2d, g, w_bf16, seq):
    t, d = x2d.shape
    n = w_bf16.shape[1]
    tm, tn = IN_PROJ_TM, IN_PROJ_TN
    c, sa, sb = _rope_tables(seq)
    pos_blocks = seq // tm
    tab_spec = pl.BlockSpec((tm, HEAD_W), lambda i, j: (i % pos_blocks, 0))
    return pl.pallas_call(
        _in_proj_kernel,
        grid=(t // tm, n // tn),
        in_specs=[
            pl.BlockSpec((tm, d), lambda i, j: (i, 0)),
            pl.BlockSpec((1, d), lambda i, j: (0, 0)),
            pl.BlockSpec((d, tn), lambda i, j: (0, j)),
            tab_spec, tab_spec, tab_spec,
        ],
        out_specs=pl.BlockSpec((tm, tn), lambda i, j: (i, j)),
        out_shape=jax.ShapeDtypeStruct((t, n), BF16),
        scratch_shapes=[pltpu.VMEM((tm, d), BF16)],
        compiler_params=pltpu.CompilerParams(
            dimension_semantics=("arbitrary", "arbitrary"), vmem_limit_bytes=VMEM_LIMIT),
        name="in_proj",
    )(x2d, g.reshape(1, d), w_bf16, c, sa, sb)


def _fill_vt(v_ref, vt_ref, kb):
    r = lax.broadcasted_iota(jnp.int32, (HEAD_W, HEAD_W), 0)
    c = lax.broadcasted_iota(jnp.int32, (HEAD_W, HEAD_W), 1)
    eye = jnp.where(r == c, 1.0, 0.0).astype(BF16)
    for blk in range(vt_ref.shape[0]):
        vb = v_ref[0, blk * kb:(blk + 1) * kb, :]
        vt_ref[blk] = lax.dot_general(eye, vb, _NT, preferred_element_type=F32).astype(BF16)


def _sb_attn_kernel(q_ref, k_ref, v_ref, g_ref, o_ref, vt_ref, acc_ref, carry_ref, *, tq, kb, scale):
    qi = pl.program_id(2)
    groups = tq // kb

    @pl.when(qi == 0)
    def _():
        _fill_vt(v_ref, vt_ref, kb)

    row = lax.broadcasted_iota(jnp.int32, (kb, 2 * kb), 0)
    col = lax.broadcasted_iota(jnp.int32, (kb, 2 * kb), 1) % kb
    suffix = jnp.where(col >= row, 1.0, 0.0).astype(BF16)

    acc_ref[...] = jnp.zeros_like(acc_ref)
    carry_ref[...] = jnp.zeros_like(carry_ref)

    def tile(j, lane_lo, masked):
        width = tq - lane_lo
        ks = pl.multiple_of(j * kb, kb)
        kblk = k_ref[0, pl.ds(ks, kb), :]
        y = lax.dot_general(kblk, q_ref[0, lane_lo:, :], _NT, preferred_element_type=F32) * (scale * _LOG2E)
        neg_abs = pltpu.bitcast(pltpu.bitcast(y, jnp.uint32) | jnp.uint32(0x80000000), F32)
        sp = jnp.maximum(y, 0.0) + jnp.log2(1.0 + jnp.exp2(neg_abs))
        if masked:
            valid = (lax.broadcasted_iota(jnp.int32, (kb, width), 0)
                     < lax.broadcasted_iota(jnp.int32, (kb, width), 1))
            sp = jnp.where(valid, sp, 0.0)
        hi = sp.astype(BF16)
        lo = (sp - hi.astype(F32)).astype(BF16)
        cs = jnp.dot(suffix, jnp.concatenate([hi, lo], axis=0), preferred_element_type=F32)
        w = jnp.exp2((y - carry_ref[:, lane_lo:]) - cs)
        if masked:
            w = jnp.where(valid, w, 0.0)
        acc_ref[:, lane_lo:] += jnp.dot(vt_ref[j], w.astype(BF16), preferred_element_type=F32)
        carry_ref[:, lane_lo:] += jnp.sum(sp, axis=0, keepdims=True)

    for d in reversed(range(groups)):
        tile(qi * groups + d, d * kb, True)

    def body(i, _):
        tile(qi * groups - 1 - i, 0, False)
        return 0

    lax.fori_loop(0, qi * groups, body, 0)

    g = g_ref[0].astype(F32)
    o_ref[0] = (acc_ref[...].T * (g * _sigmoid(g))).astype(BF16)


def _sb_attn(proj3d):
    b, s, _ = proj3d.shape
    tq, kb = ATTN_TQ, ATTN_KB
    assert tq % kb == 0
    nsec = SEC_W // HEAD_W
    kern = functools.partial(_sb_attn_kernel, tq=tq, kb=kb, scale=HEAD_W ** -0.5)
    return pl.pallas_call(
        kern,
        grid=(b, SB_HEADS, s // tq),
        in_specs=[
            pl.BlockSpec((1, tq, HEAD_W), lambda bi, h, qi: (bi, qi, SEC_SB_Q * nsec + h)),
            pl.BlockSpec((1, s, HEAD_W), lambda bi, h, qi: (bi, 0, SEC_SB_K * nsec + h)),
            pl.BlockSpec((1, s, HEAD_W), lambda bi, h, qi: (bi, 0, SEC_SB_V * nsec + h)),
            pl.BlockSpec((1, tq, HEAD_W), lambda bi, h, qi: (bi, qi, SEC_SB_G * nsec + h)),
        ],
        out_specs=pl.BlockSpec((1, tq, HEAD_W), lambda bi, h, qi: (bi, qi, h)),
        out_shape=jax.ShapeDtypeStruct((b, s, SB_HEADS * HEAD_W), BF16),
        scratch_shapes=[
            pltpu.VMEM((s // kb, HEAD_W, kb), BF16),
            pltpu.VMEM((HEAD_W, tq), F32),
            pltpu.VMEM((1, tq), F32),
        ],
        compiler_params=pltpu.CompilerParams(
            dimension_semantics=("arbitrary", "arbitrary", "arbitrary"), vmem_limit_bytes=VMEM_LIMIT),
        name="sb_attn",
    )(proj3d, proj3d, proj3d, proj3d)


def _df_attn_kernel(lam_ref, sg_ref, q_ref, k_ref, v_ref, g_ref, o_ref, vt_ref, qs_ref, acc_ref, m_ref, l_ref,
                    *, tq, kb, lambda_init):
    qi = pl.program_id(2)
    groups = tq // kb

    @pl.when(qi == 0)
    def _():
        _fill_vt(v_ref, vt_ref, kb)

    qf = q_ref[0].astype(F32) * (DF_QK_DIM ** -0.5)
    lane = lax.broadcasted_iota(jnp.int32, (tq, HEAD_W), 1)
    qs_ref[0] = jnp.where(lane < DF_QK_DIM, qf, 0.0).astype(BF16)
    qs_ref[1] = jnp.where(lane >= DF_QK_DIM, qf, 0.0).astype(BF16)

    acc_ref[...] = jnp.zeros_like(acc_ref)
    l_ref[...] = jnp.zeros_like(l_ref)
    m_ref[...] = jnp.full_like(m_ref, -jnp.inf)

    def tile(j, lane_lo, masked):
        width = tq - lane_lo
        ks = pl.multiple_of(j * kb, kb)
        kblk = k_ref[0, pl.ds(ks, kb), :]
        vt = vt_ref[j]
        for c in range(2):
            s = lax.dot_general(kblk, qs_ref[c, lane_lo:, :], _NT, preferred_element_type=F32)
            if masked:
                valid = (lax.broadcasted_iota(jnp.int32, (kb, width), 0)
                         <= lax.broadcasted_iota(jnp.int32, (kb, width), 1))
                s = jnp.where(valid, s, -jnp.inf)
            m_old = m_ref[c, :, lane_lo:]
            m_new = jnp.maximum(m_old, jnp.max(s, axis=0, keepdims=True))
            alpha = jnp.exp(m_old - m_new)
            p = jnp.exp(s - m_new)
            l_ref[c, :, lane_lo:] = alpha * l_ref[c, :, lane_lo:] + jnp.sum(p, axis=0, keepdims=True)
            acc_ref[c, :, lane_lo:] = (alpha * acc_ref[c, :, lane_lo:]
                                       + jnp.dot(vt, p.astype(BF16), preferred_element_type=F32))
            m_ref[c, :, lane_lo:] = m_new

    def body(j, _):
        tile(j, 0, False)
        return 0

    lax.fori_loop(0, qi * groups, body, 0)
    for d in range(groups):
        tile(qi * groups + d, d * kb, True)

    lam_p = lam_ref[...]
    lam = (jnp.exp(jnp.sum(lam_p[0:1] * lam_p[1:2], axis=-1, keepdims=True))
           - jnp.exp(jnp.sum(lam_p[2:3] * lam_p[3:4], axis=-1, keepdims=True))
           + lambda_init)
    o = acc_ref[0] * (1.0 / l_ref[0]) - lam * (acc_ref[1] * (1.0 / l_ref[1]))
    ms = jnp.mean(o * o, axis=0, keepdims=True)
    on = (o * lax.rsqrt(ms + SUBLN_EPS)).T
    g = g_ref[0].astype(F32)
    o_ref[0] = (((on * sg_ref[...]) * (1.0 - lambda_init)) * (g * _sigmoid(g))).astype(BF16)


def _df_attn(proj3d, lam_params, subln_g, lambda_init):
    b, s, _ = proj3d.shape
    tq, kb = ATTN_TQ, ATTN_KB
    assert tq % kb == 0
    nsec = SEC_W // HEAD_W
    kern = functools.partial(_df_attn_kernel, tq=tq, kb=kb, lambda_init=lambda_init)
    return pl.pallas_call(
        kern,
        grid=(b, DF_HEADS, s // tq),
        in_specs=[
            pl.BlockSpec(lam_params.shape, lambda bi, h, qi: (0, 0)),
            pl.BlockSpec((1, HEAD_W), lambda bi, h, qi: (0, 0)),
            pl.BlockSpec((1, tq, HEAD_W), lambda bi, h, qi: (bi, qi, SEC_DF_Q * nsec + h)),
            pl.BlockSpec((1, s, HEAD_W), lambda bi, h, qi: (bi, 0, SEC_DF_K * nsec + h)),
            pl.BlockSpec((1, s, HEAD_W), lambda bi, h, qi: (bi, 0, SEC_DF_V * nsec + h)),
            pl.BlockSpec((1, tq, HEAD_W), lambda bi, h, qi: (bi, qi, SEC_DF_G * nsec + h)),
        ],
        out_specs=pl.BlockSpec((1, tq, HEAD_W), lambda bi, h, qi: (bi, qi, h)),
        out_shape=jax.ShapeDtypeStruct((b, s, DF_HEADS * HEAD_W), BF16),
        scratch_shapes=[
            pltpu.VMEM((s // kb, HEAD_W, kb), BF16),
            pltpu.VMEM((2, tq, HEAD_W), BF16),
            pltpu.VMEM((2, HEAD_W, tq), F32),
            pltpu.VMEM((2, 1, tq), F32),
            pltpu.VMEM((2, 1, tq), F32),
        ],
        compiler_params=pltpu.CompilerParams(
            dimension_semantics=("arbitrary", "arbitrary", "arbitrary"), vmem_limit_bytes=VMEM_LIMIT),
        name="df_attn",
    )(lam_params, subln_g.reshape(1, HEAD_W), proj3d, proj3d, proj3d, proj3d)


def _tail_kernel(x_ref, msb_ref, mdf_ref, p_ref, wo_sb_ref, wo_df_ref, gp_ref, wg_ref, wp_ref, gf_ref, o_ref,
                 *, final_norm):
    h = (x_ref[...]
         + jnp.dot(msb_ref[...], wo_sb_ref[...], preferred_element_type=F32)
         + jnp.dot(mdf_ref[...], wo_df_ref[...], preferred_element_type=F32))
    ms = jnp.mean(h * h, axis=-1, keepdims=True)
    hn = ((h * lax.rsqrt(ms + NORM_EPS)) * gp_ref[...]).astype(BF16)
    gate = _sigmoid(jnp.dot(hn, wg_ref[...], preferred_element_type=F32))
    pp = jnp.dot(p_ref[...].astype(BF16), wp_ref[...], preferred_element_type=F32)
    h = h + gate * pp
    if final_norm:
        ms = jnp.mean(h * h, axis=-1, keepdims=True)
        h = (h * lax.rsqrt(ms + NORM_EPS)) * gf_ref[...]
    o_ref[...] = h


def _tail(x2d, msb, mdf, p2d, w_out, g_ple, w_gate, w_ple, g_final, final_norm):
    t, d = x2d.shape
    tm = TAIL_TM
    dsb, ddf, dp = msb.shape[1], mdf.shape[1], p2d.shape[1]
    wo_sb, wo_df = w_out[:dsb], w_out[dsb:]
    rows = lambda width: pl.BlockSpec((tm, width), lambda i: (i, 0))
    whole = lambda a: pl.BlockSpec(a.shape, lambda i: (0, 0), pipeline_mode=pl.Buffered(1))
    g_ple, g_final = g_ple.reshape(1, d), g_final.reshape(1, d)
    return pl.pallas_call(
        functools.partial(_tail_kernel, final_norm=final_norm),
        grid=(t // tm,),
        in_specs=[rows(d), rows(dsb), rows(ddf), rows(dp),
                  whole(wo_sb), whole(wo_df), whole(g_ple), whole(w_gate), whole(w_ple), whole(g_final)],
        out_specs=rows(d),
        out_shape=jax.ShapeDtypeStruct((t, d), F32),
        compiler_params=pltpu.CompilerParams(
            dimension_semantics=("arbitrary",), vmem_limit_bytes=VMEM_LIMIT),
        name="tail",
    )(x2d, msb, mdf, p2d, wo_sb, wo_df, g_ple, w_gate, w_ple, g_final)


def kernel(x, p, norm_mix_g, w_in, lambda_q1, lambda_k1, lambda_q2, lambda_k2, subln_g, w_out, norm_ple_g,
           w_ple_gate, w_ple_proj, norm_final_g):
    b, s, d = x.shape
    depth = w_in.shape[0]
    h = x.reshape(b * s, d)
    for i in range(depth):
        lambda_init = 0.8 - 0.6 * math.exp(-0.3 * i)
        proj = _in_proj(h, norm_mix_g[i], w_in[i].astype(BF16), s).reshape(b, s, -1)
        msb = _sb_attn(proj).reshape(b * s, -1)
        lam_params = jnp.stack([lambda_q1[i], lambda_k1[i], lambda_q2[i], lambda_k2[i]])
        mdf = _df_attn(proj, lam_params, subln_g[i], lambda_init).reshape(b * s, -1)
        h = _tail(h, msb, mdf, p[i].reshape(b * s, -1), w_out[i].astype(BF16), norm_ple_g[i],
                  w_ple_gate[i].astype(BF16), w_ple_proj[i].astype(BF16), norm_final_g,
                  final_norm=(i == depth - 1))
    return h.reshape(b, s, d)
```

```python
import functools
import math

import jax
import jax.numpy as jnp
from jax import lax
from jax.experimental import pallas as pl
from jax.experimental.pallas import tpu as pltpu

F32 = jnp.float32
BF16 = jnp.bfloat16

SB_HEADS = 8
DF_HEADS = 8
HEAD_W = 128
DF_QK_DIM = 64
ROPE_DIM = DF_QK_DIM // 4
ROPE_THETA = 500000.0
NORM_EPS = 1e-6
SUBLN_EPS = 1e-5

SEC_W = 1024
SEC_SB_Q, SEC_SB_K, SEC_SB_V, SEC_SB_G, SEC_DF_Q, SEC_DF_K, SEC_DF_V, SEC_DF_G = range(8)

IN_PROJ_TM = 512
IN_PROJ_TN = SEC_W
ATTN_GW = 256
SB_CUMSUM_ROWS = 128
TAIL_TM = 256
VMEM_LIMIT = 56 * 1024 * 1024

_NT = (((1,), (1,)), ((), ()))
_LOG2E = math.log2(math.e)
SB_Q_SCALE = HEAD_W ** -0.5 * _LOG2E
DF_Q_SCALE = DF_QK_DIM ** -0.5 * _LOG2E


def _sigmoid(x):
    return 1.0 / (1.0 + jnp.exp(-x))


def _in_proj_kernel(x_ref, g_ref, w_ref, cos_ref, sa_ref, sb_ref, o_ref, hn_ref):
    j = pl.program_id(1)

    @pl.when(j == 0)
    def _():
        x = x_ref[...]
        ms = jnp.mean(x * x, axis=-1, keepdims=True)
        hn_ref[...] = ((x * lax.rsqrt(ms + NORM_EPS)) * g_ref[...]).astype(BF16)

    acc = jnp.dot(hn_ref[...], w_ref[...], preferred_element_type=F32)
    is_rope = jnp.logical_or(j == SEC_DF_Q, j == SEC_DF_K)
    is_sb_q = j == SEC_SB_Q

    @pl.when(is_rope)
    def _():
        qk_scale = jnp.where(j == SEC_DF_Q, DF_Q_SCALE, 1.0).astype(F32)
        c, sa, sb = cos_ref[...] * qk_scale, sa_ref[...] * qk_scale, sb_ref[...] * qk_scale
        for h in range(IN_PROJ_TN // HEAD_W):
            xh = acc[:, h * HEAD_W:(h + 1) * HEAD_W]
            rot = xh * c + pltpu.roll(xh, HEAD_W - ROPE_DIM // 2, 1) * sa + pltpu.roll(xh, ROPE_DIM // 2, 1) * sb
            o_ref[:, h * HEAD_W:(h + 1) * HEAD_W] = rot.astype(BF16)

    @pl.when(is_sb_q)
    def _():
        o_ref[...] = (acc * SB_Q_SCALE).astype(BF16)

    @pl.when(jnp.logical_not(jnp.logical_or(is_rope, is_sb_q)))
    def _():
        o_ref[...] = acc.astype(BF16)


def _rope_tables(seq):
    half = ROPE_DIM // 2
    inv_freq = ROPE_THETA ** (-jnp.arange(0, ROPE_DIM, 2, dtype=F32) / ROPE_DIM)
    ang = jnp.arange(seq, dtype=F32)[:, None] * inv_freq[None, :]
    cos, sin = jnp.cos(ang), jnp.sin(ang)
    lane = jnp.arange(HEAD_W)
    within = lane % DF_QK_DIM
    freq = within % half
    first = within < half
    second = jnp.logical_and(within >= half, within < ROPE_DIM)
    cos_l, sin_l = cos[:, freq], sin[:, freq]
    c = jnp.where(jnp.logical_or(first, second)[None, :], cos_l, 1.0)
    sa = jnp.where(first[None, :], -sin_l, 0.0)
    sb = jnp.where(second[None, :], sin_l, 0.0)
    return c.astype(F32), sa.astype(F32), sb.astype(F32)


def _in_proj(x2d, g, w_bf16, seq):
    t, d = x2d.shape
    n = w_bf16.shape[1]
    tm, tn = IN_PROJ_TM, IN_PROJ_TN
    c, sa, sb = _rope_tables(seq)
    pos_blocks = seq // tm
    tab_spec = pl.BlockSpec((tm, HEAD_W), lambda i, j: (i % pos_blocks, 0))
    return pl.pallas_call(
        _in_proj_kernel,
        grid=(t // tm, n // tn),
        in_specs=[
            pl.BlockSpec((tm, d), lambda i, j: (i, 0)),
            pl.BlockSpec((1, d), lambda i, j: (0, 0)),
            pl.BlockSpec((d, tn), lambda i, j: (0, j)),
            tab_spec, tab_spec, tab_spec,
        ],
        out_specs=pl.BlockSpec((tm, tn), lambda i, j: (i, j)),
        out_shape=jax.ShapeDtypeStruct((t, n), BF16),
        scratch_shapes=[pltpu.VMEM((tm, d), BF16)],
        compiler_params=pltpu.CompilerParams(
            dimension_semantics=("arbitrary", "arbitrary"), vmem_limit_bytes=VMEM_LIMIT),
        name="in_proj",
    )(x2d, g.reshape(1, d), w_bf16, c, sa, sb)


def _fill_vt(v_ref, vt_ref, chunk):
    r = lax.broadcasted_iota(jnp.int32, (HEAD_W, HEAD_W), 0)
    c = lax.broadcasted_iota(jnp.int32, (HEAD_W, HEAD_W), 1)
    eye = jnp.where(r == c, 1.0, 0.0).astype(BF16)
    for lo in range(0, vt_ref.shape[1], chunk):
        vb = v_ref[0, lo:lo + chunk, :]
        vt_ref[:, lo:lo + chunk] = lax.dot_general(eye, vb, _NT, preferred_element_type=F32).astype(BF16)


def _attn_call(kern, proj3d, sec_q, sec_k, sec_v, sec_g, extra_inputs, extra_scratch, name):
    b, s, _ = proj3d.shape
    nsec = SEC_W // HEAD_W
    head_block = lambda sec: pl.BlockSpec((1, s, HEAD_W), lambda bi, h: (bi, 0, sec * nsec + h))
    return pl.pallas_call(
        kern,
        grid=(b, SEC_W // HEAD_W),
        in_specs=[pl.BlockSpec(a.shape, lambda bi, h: (0, 0)) for a in extra_inputs]
        + [head_block(sec_q), head_block(sec_k), head_block(sec_v), head_block(sec_g)],
        out_specs=pl.BlockSpec((1, s, HEAD_W), lambda bi, h: (bi, 0, h)),
        out_shape=jax.ShapeDtypeStruct((b, s, SEC_W), BF16),
        scratch_shapes=[pltpu.VMEM((HEAD_W, s), BF16)] + extra_scratch,
        compiler_params=pltpu.CompilerParams(
            dimension_semantics=("arbitrary", "arbitrary"), vmem_limit_bytes=VMEM_LIMIT),
        name=name,
    )(*extra_inputs, proj3d, proj3d, proj3d, proj3d)


def _sb_attn_kernel(q_ref, k_ref, v_ref, g_ref, o_ref, vt_ref, *, gw, cb):
    seq = k_ref.shape[1]
    _fill_vt(v_ref, vt_ref, gw)

    row = lax.broadcasted_iota(jnp.int32, (cb, 2 * cb), 0)
    col = lax.broadcasted_iota(jnp.int32, (cb, 2 * cb), 1) % cb
    suffix = jnp.where(col >= row, 1.0, 0.0).astype(BF16)
    key_pos = lax.broadcasted_iota(jnp.int32, (cb, gw), 0)
    qry_pos = lax.broadcasted_iota(jnp.int32, (cb, gw), 1)
    strict = [key_pos + r * cb < qry_pos for r in range(gw // cb)]

    ngroups = seq // gw
    per_group = gw // cb
    logits, splits, sums, weights = {}, {}, {}, {}

    def diag_mask(g, b):
        return strict[b - g * per_group] if b >= g * per_group else None

    def stage_logits(g):
        logits[g] = lax.dot_general(k_ref[0, :(g + 1) * gw, :], q_ref[0, g * gw:(g + 1) * gw, :], _NT,
                                    preferred_element_type=F32)

    def stage_split(g):
        y = logits[g]
        sp = jnp.maximum(y, 0.0) + jnp.log2(1.0 + jnp.exp2(-jnp.abs(y)))
        parts = []
        for b in range((g + 1) * per_group):
            sp_b = sp[b * cb:(b + 1) * cb]
            if diag_mask(g, b) is not None:
                sp_b = jnp.where(diag_mask(g, b), sp_b, 0.0)
            hi = sp_b.astype(BF16)
            lo = (sp_b - hi.astype(F32)).astype(BF16)
            parts.append(jnp.concatenate([hi, lo], axis=0))
        splits[g] = parts

    def stage_sums(g):
        sums[g] = [jnp.dot(suffix, part, preferred_element_type=F32) for part in splits.pop(g)]

    def stage_weights(g):
        y, block_sums = logits.pop(g), sums.pop(g)
        nblocks = (g + 1) * per_group
        w = [None] * nblocks
        carry = None
        for b in reversed(range(nblocks)):
            y_b = y[b * cb:(b + 1) * cb]
            w_b = jnp.exp2((y_b if carry is None else y_b - carry) - block_sums[b])
            if diag_mask(g, b) is not None:
                w_b = jnp.where(diag_mask(g, b), w_b, 0.0)
            w[b] = w_b.astype(BF16)
            total = block_sums[b][0:1, :]
            carry = total if carry is None else carry + total
        weights[g] = jnp.concatenate(w, axis=0)

    def stage_out(g):
        rows = slice(g * gw, (g + 1) * gw)
        out_t = jnp.dot(vt_ref[:, :(g + 1) * gw], weights.pop(g), preferred_element_type=F32)
        gate = g_ref[0, rows, :].astype(F32)
        o_ref[0, rows, :] = (out_t.T * (gate * _sigmoid(gate))).astype(BF16)

    stages = (stage_logits, stage_split, stage_sums, stage_weights, stage_out)
    for step in range(ngroups + len(stages) - 1):
        for lag, stage in enumerate(stages):
            if 0 <= step - lag < ngroups:
                stage(step - lag)


def _sb_attn(proj3d):
    kern = functools.partial(_sb_attn_kernel, gw=ATTN_GW, cb=SB_CUMSUM_ROWS)
    return _attn_call(kern, proj3d, SEC_SB_Q, SEC_SB_K, SEC_SB_V, SEC_SB_G, [], [], "sb_attn")


def _df_attn_kernel(lam_ref, sg_ref, q_ref, k_ref, v_ref, g_ref, o_ref, vt_ref, qs_ref, *, gw, lambda_init):
    seq = k_ref.shape[1]
    _fill_vt(v_ref, vt_ref, gw)

    q = q_ref[0]
    lane = lax.broadcasted_iota(jnp.int32, (seq, HEAD_W), 1)
    qs_ref[0] = jnp.where(lane < DF_QK_DIM, q, jnp.zeros_like(q))
    qs_ref[1] = jnp.where(lane >= DF_QK_DIM, q, jnp.zeros_like(q))

    lam_p = lam_ref[...]
    lam = (jnp.exp(jnp.sum(lam_p[0:1] * lam_p[1:2], axis=-1, keepdims=True))
           - jnp.exp(jnp.sum(lam_p[2:3] * lam_p[3:4], axis=-1, keepdims=True))
           + lambda_init)
    causal = lax.broadcasted_iota(jnp.int32, (gw, gw), 0) <= lax.broadcasted_iota(jnp.int32, (gw, gw), 1)

    ngroups = seq // gw
    scores, probs, outs = {}, {}, {}

    def stage_scores(g, c):
        keys = (g + 1) * gw
        s = lax.dot_general(k_ref[0, :keys, :], qs_ref[c, g * gw:(g + 1) * gw, :], _NT,
                            preferred_element_type=F32)
        s_diag = jnp.where(causal, s[keys - gw:], -jnp.inf)
        s = s_diag if g == 0 else jnp.concatenate([s[:keys - gw], s_diag], axis=0)
        scores[g, c] = (s, jnp.max(s, axis=0, keepdims=True))

    def stage_probs(g, c):
        s, m = scores.pop((g, c))
        p = jnp.exp2(s - m)
        probs[g, c] = (p.astype(BF16), jnp.sum(p, axis=0, keepdims=True))

    def stage_out(g, c):
        keys = (g + 1) * gw
        p, denom = probs.pop((g, c))
        pv = jnp.dot(vt_ref[:, :keys], p, preferred_element_type=F32)
        outs[g, c] = pv * (1.0 / denom)
        if c == 1:
            rows = slice(g * gw, (g + 1) * gw)
            o = outs.pop((g, 0)) - lam * outs.pop((g, 1))
            ms = jnp.mean(o * o, axis=0, keepdims=True)
            on = (o * lax.rsqrt(ms + SUBLN_EPS)).T
            gate = g_ref[0, rows, :].astype(F32)
            o_ref[0, rows, :] = (((on * sg_ref[...]) * (1.0 - lambda_init))
                                 * (gate * _sigmoid(gate))).astype(BF16)

    order = list(range(ngroups))
    for step in range(ngroups + 2):
        for c in range(2):
            if step < ngroups:
                stage_scores(order[step], c)
        for c in range(2):
            if 0 <= step - 1 < ngroups:
                stage_probs(order[step - 1], c)
        for c in range(2):
            if 0 <= step - 2 < ngroups:
                stage_out(order[step - 2], c)


def _df_attn(proj3d, lam_params, subln_g, lambda_init):
    s = proj3d.shape[1]
    kern = functools.partial(_df_attn_kernel, gw=ATTN_GW, lambda_init=lambda_init)
    return _attn_call(kern, proj3d, SEC_DF_Q, SEC_DF_K, SEC_DF_V, SEC_DF_G,
                      [lam_params, subln_g.reshape(1, HEAD_W)], [pltpu.VMEM((2, s, HEAD_W), BF16)], "df_attn")


def _tail_kernel(x_ref, msb_ref, mdf_ref, p_ref, wo_sb_ref, wo_df_ref, gp_ref, wg_ref, wp_ref, gf_ref, o_ref,
                 *, final_norm):
    h = (x_ref[...]
         + jnp.dot(msb_ref[...], wo_sb_ref[...], preferred_element_type=F32)
         + jnp.dot(mdf_ref[...], wo_df_ref[...], preferred_element_type=F32))
    ms = jnp.mean(h * h, axis=-1, keepdims=True)
    hn = ((h * lax.rsqrt(ms + NORM_EPS)) * gp_ref[...]).astype(BF16)
    gate = _sigmoid(jnp.dot(hn, wg_ref[...], preferred_element_type=F32))
    pp = jnp.dot(p_ref[...].astype(BF16), wp_ref[...], preferred_element_type=F32)
    h = h + gate * pp
    if final_norm:
        ms = jnp.mean(h * h, axis=-1, keepdims=True)
        h = (h * lax.rsqrt(ms + NORM_EPS)) * gf_ref[...]
    o_ref[...] = h


def _tail(x2d, msb, mdf, p2d, w_out, g_ple, w_gate, w_ple, g_final, final_norm):
    t, d = x2d.shape
    tm = TAIL_TM
    dsb, ddf, dp = msb.shape[1], mdf.shape[1], p2d.shape[1]
    wo_sb, wo_df = w_out[:dsb], w_out[dsb:]
    rows = lambda width: pl.BlockSpec((tm, width), lambda i: (i, 0))
    whole = lambda a: pl.BlockSpec(a.shape, lambda i: (0, 0), pipeline_mode=pl.Buffered(1))
    g_ple, g_final = g_ple.reshape(1, d), g_final.reshape(1, d)
    return pl.pallas_call(
        functools.partial(_tail_kernel, final_norm=final_norm),
        grid=(t // tm,),
        in_specs=[rows(d), rows(dsb), rows(ddf), rows(dp),
                  whole(wo_sb), whole(wo_df), whole(g_ple), whole(w_gate), whole(w_ple), whole(g_final)],
        out_specs=rows(d),
        out_shape=jax.ShapeDtypeStruct((t, d), F32),
        compiler_params=pltpu.CompilerParams(
            dimension_semantics=("arbitrary",), vmem_limit_bytes=VMEM_LIMIT),
        name="tail",
    )(x2d, msb, mdf, p2d, wo_sb, wo_df, g_ple, w_gate, w_ple, g_final)


def kernel(x, p, norm_mix_g, w_in, lambda_q1, lambda_k1, lambda_q2, lambda_k2, subln_g, w_out, norm_ple_g,
           w_ple_gate, w_ple_proj, norm_final_g):
    b, s, d = x.shape
    depth = w_in.shape[0]
    h = x.reshape(b * s, d)
    for i in range(depth):
        lambda_init = 0.8 - 0.6 * math.exp(-0.3 * i)
        proj = _in_proj(h, norm_mix_g[i], w_in[i].astype(BF16), s).reshape(b, s, -1)
        msb = _sb_attn(proj).reshape(b * s, -1)
        lam_params = jnp.stack([lambda_q1[i], lambda_k1[i], lambda_q2[i], lambda_k2[i]])
        mdf = _df_attn(proj, lam_params, subln_g[i], lambda_init).reshape(b * s, -1)
        h = _tail(h, msb, mdf, p[i].reshape(b * s, -1), w_out[i].astype(BF16), norm_ple_g[i],
                  w_ple_gate[i].astype(BF16), w_ple_proj[i].astype(BF16), norm_final_g,
                  final_norm=(i == depth - 1))
    return h.reshape(b, s, d)
```

```python
import functools
import math

import jax
import jax.numpy as jnp
from jax import lax
from jax.experimental import pallas as pl
from jax.experimental.pallas import tpu as pltpu

F32 = jnp.float32
BF16 = jnp.bfloat16

SB_HEADS = 8
DF_HEADS = 8
HEAD_W = 128
BF16_SUBLANES = 16
DF_QK_DIM = 64
ROPE_DIM = DF_QK_DIM // 4
ROPE_THETA = 500000.0
NORM_EPS = 1e-6
SUBLN_EPS = 1e-5

SEC_W = 1024
SEC_SB_Q, SEC_SB_K, SEC_SB_V, SEC_SB_G, SEC_DF_Q, SEC_DF_K, SEC_DF_V, SEC_DF_G = range(8)

IN_PROJ_TM = 1024
IN_PROJ_TN = SEC_W
IN_PROJ_CHUNK = 256
ATTN_GW = 256
SB_CUMSUM_ROWS = 128
TAIL_TM = 256
VMEM_LIMIT = 56 * 1024 * 1024

_NT = (((1,), (1,)), ((), ()))
_LOG2E = math.log2(math.e)
SB_Q_SCALE = HEAD_W ** -0.5 * _LOG2E
DF_Q_SCALE = DF_QK_DIM ** -0.5 * _LOG2E


def _sigmoid(x):
    return 1.0 / (1.0 + jnp.exp(-x))


def _in_proj_kernel(x_ref, g_ref, w_ref, cos_ref, sa_ref, sb_ref, o_ref, hn_ref):
    j = pl.program_id(1)

    @pl.when(j == 0)
    def _():
        x = x_ref[...]
        ms = jnp.mean(x * x, axis=-1, keepdims=True)
        hn_ref[...] = ((x * lax.rsqrt(ms + NORM_EPS)) * g_ref[...]).astype(BF16)

    is_rope = jnp.logical_or(j == SEC_DF_Q, j == SEC_DF_K)
    is_sb_q = j == SEC_SB_Q
    nchunks = IN_PROJ_TN // IN_PROJ_CHUNK

    def project(epilogue):
        chunk = lambda n: jnp.dot(hn_ref[...], w_ref[:, n * IN_PROJ_CHUNK:(n + 1) * IN_PROJ_CHUNK],
                                  preferred_element_type=F32)
        nxt = chunk(0)
        for n in range(nchunks):
            acc, nxt = nxt, (chunk(n + 1) if n + 1 < nchunks else None)
            epilogue(acc, n * IN_PROJ_CHUNK)

    @pl.when(is_rope)
    def _():
        qk_scale = jnp.where(j == SEC_DF_Q, DF_Q_SCALE, 1.0).astype(F32)
        c, sa, sb = cos_ref[...] * qk_scale, sa_ref[...] * qk_scale, sb_ref[...] * qk_scale

        def rope(acc, col):
            for h in range(IN_PROJ_CHUNK // HEAD_W):
                xh = acc[:, h * HEAD_W:(h + 1) * HEAD_W]
                rot = (xh * c + pltpu.roll(xh, HEAD_W - ROPE_DIM // 2, 1) * sa
                       + pltpu.roll(xh, ROPE_DIM // 2, 1) * sb)
                o_ref[:, col + h * HEAD_W:col + (h + 1) * HEAD_W] = rot.astype(BF16)

        project(rope)

    @pl.when(is_sb_q)
    def _():
        def scaled(acc, col):
            o_ref[:, col:col + IN_PROJ_CHUNK] = (acc * SB_Q_SCALE).astype(BF16)

        project(scaled)

    @pl.when(jnp.logical_not(jnp.logical_or(is_rope, is_sb_q)))
    def _():
        def plain(acc, col):
            o_ref[:, col:col + IN_PROJ_CHUNK] = acc.astype(BF16)

        project(plain)


def _rope_tables(seq):
    half = ROPE_DIM // 2
    inv_freq = ROPE_THETA ** (-jnp.arange(0, ROPE_DIM, 2, dtype=F32) / ROPE_DIM)
    ang = jnp.arange(seq, dtype=F32)[:, None] * inv_freq[None, :]
    cos, sin = jnp.cos(ang), jnp.sin(ang)
    lane = jnp.arange(HEAD_W)
    within = lane % DF_QK_DIM
    freq = within % half
    first = within < half
    second = jnp.logical_and(within >= half, within < ROPE_DIM)
    cos_l, sin_l = cos[:, freq], sin[:, freq]
    c = jnp.where(jnp.logical_or(first, second)[None, :], cos_l, 1.0)
    sa = jnp.where(first[None, :], -sin_l, 0.0)
    sb = jnp.where(second[None, :], sin_l, 0.0)
    return c.astype(F32), sa.astype(F32), sb.astype(F32)


def _in_proj(x2d, g, w_bf16, seq):
    t, d = x2d.shape
    n = w_bf16.shape[1]
    tm, tn = IN_PROJ_TM, IN_PROJ_TN
    c, sa, sb = _rope_tables(seq)
    pos_blocks = seq // tm
    tab_spec = pl.BlockSpec((tm, HEAD_W), lambda i, j: (i % pos_blocks, 0))
    return pl.pallas_call(
        _in_proj_kernel,
        grid=(t // tm, n // tn),
        in_specs=[
            pl.BlockSpec((tm, d), lambda i, j: (i, 0)),
            pl.BlockSpec((1, d), lambda i, j: (0, 0)),
            pl.BlockSpec((d, tn), lambda i, j: (0, j)),
            tab_spec, tab_spec, tab_spec,
        ],
        out_specs=pl.BlockSpec((tm, tn), lambda i, j: (i, j)),
        out_shape=jax.ShapeDtypeStruct((t, n), BF16),
        scratch_shapes=[pltpu.VMEM((tm, d), BF16)],
        compiler_params=pltpu.CompilerParams(
            dimension_semantics=("arbitrary", "arbitrary"), vmem_limit_bytes=VMEM_LIMIT),
        name="in_proj",
    )(x2d, g.reshape(1, d), w_bf16, c, sa, sb)


def _fill_vt(v_ref, vt_ref, chunk):
    r = lax.broadcasted_iota(jnp.int32, (HEAD_W, HEAD_W), 0)
    c = lax.broadcasted_iota(jnp.int32, (HEAD_W, HEAD_W), 1)
    eye = jnp.where(r == c, 1.0, 0.0).astype(BF16)
    for lo in range(0, vt_ref.shape[1], chunk):
        vb = v_ref[0, lo:lo + chunk, :]
        vt_ref[:HEAD_W, lo:lo + chunk] = lax.dot_general(eye, vb, _NT, preferred_element_type=F32).astype(BF16)
    if vt_ref.shape[0] > HEAD_W:
        vt_ref[HEAD_W:, :] = jnp.ones((vt_ref.shape[0] - HEAD_W, vt_ref.shape[1]), BF16)


def _attn_call(kern, proj3d, sec_q, sec_k, sec_v, sec_g, extra_inputs, extra_scratch, vt_rows, name):
    b, s, _ = proj3d.shape
    nsec = SEC_W // HEAD_W
    head_block = lambda sec: pl.BlockSpec((1, s, HEAD_W), lambda bi, h: (bi, 0, sec * nsec + h))
    return pl.pallas_call(
        kern,
        grid=(b, SEC_W // HEAD_W),
        in_specs=[pl.BlockSpec(a.shape, lambda bi, h: (0, 0)) for a in extra_inputs]
        + [head_block(sec_q), head_block(sec_k), head_block(sec_v), head_block(sec_g)],
        out_specs=pl.BlockSpec((1, s, HEAD_W), lambda bi, h: (bi, 0, h)),
        out_shape=jax.ShapeDtypeStruct((b, s, SEC_W), BF16),
        scratch_shapes=[pltpu.VMEM((vt_rows, s), BF16)] + extra_scratch,
        compiler_params=pltpu.CompilerParams(
            dimension_semantics=("arbitrary", "arbitrary"), vmem_limit_bytes=VMEM_LIMIT),
        name=name,
    )(*extra_inputs, proj3d, proj3d, proj3d, proj3d)


def _sb_attn_kernel(q_ref, k_ref, v_ref, g_ref, o_ref, vt_ref, *, gw, cb):
    seq = k_ref.shape[1]
    _fill_vt(v_ref, vt_ref, gw)

    row = lax.broadcasted_iota(jnp.int32, (cb, 2 * cb), 0)
    col = lax.broadcasted_iota(jnp.int32, (cb, 2 * cb), 1) % cb
    suffix = jnp.where(col >= row, 1.0, 0.0).astype(BF16)
    key_pos = lax.broadcasted_iota(jnp.int32, (cb, gw), 0)
    qry_pos = lax.broadcasted_iota(jnp.int32, (cb, gw), 1)
    strict = [key_pos + r * cb < qry_pos for r in range(gw // cb)]

    ngroups = seq // gw
    per_group = gw // cb
    logits, splits, sums, weights = {}, {}, {}, {}

    def diag_mask(g, b):
        return strict[b - g * per_group] if b >= g * per_group else None

    def stage_logits(g):
        logits[g] = lax.dot_general(k_ref[0, :(g + 1) * gw, :], q_ref[0, g * gw:(g + 1) * gw, :], _NT,
                                    preferred_element_type=F32)

    def stage_split(g):
        y = logits[g]
        sp = jnp.maximum(y, 0.0) + jnp.log2(1.0 + jnp.exp2(-jnp.abs(y)))
        parts = []
        for b in range((g + 1) * per_group):
            sp_b = sp[b * cb:(b + 1) * cb]
            if diag_mask(g, b) is not None:
                sp_b = jnp.where(diag_mask(g, b), sp_b, 0.0)
            hi = sp_b.astype(BF16)
            lo = (sp_b - hi.astype(F32)).astype(BF16)
            parts.append(jnp.concatenate([hi, lo], axis=0))
        splits[g] = parts

    def stage_sums(g):
        sums[g] = [jnp.dot(suffix, part, preferred_element_type=F32) for part in splits.pop(g)]

    def stage_weights(g):
        y, block_sums = logits.pop(g), sums.pop(g)
        nblocks = (g + 1) * per_group
        w = [None] * nblocks
        carry = None
        for b in reversed(range(nblocks)):
            y_b = y[b * cb:(b + 1) * cb]
            w_b = jnp.exp2((y_b if carry is None else y_b - carry) - block_sums[b])
            if diag_mask(g, b) is not None:
                w_b = jnp.where(diag_mask(g, b), w_b, 0.0)
            w[b] = w_b.astype(BF16)
            total = block_sums[b][0:1, :]
            carry = total if carry is None else carry + total
        weights[g] = jnp.concatenate(w, axis=0)

    def stage_out(g):
        rows = slice(g * gw, (g + 1) * gw)
        out_t = jnp.dot(vt_ref[:, :(g + 1) * gw], weights.pop(g), preferred_element_type=F32)
        gate = g_ref[0, rows, :].astype(F32)
        o_ref[0, rows, :] = (out_t.T * (gate * _sigmoid(gate))).astype(BF16)

    stages = (stage_logits, stage_split, stage_sums, stage_weights, stage_out)
    for step in range(ngroups + len(stages) - 1):
        for lag, stage in enumerate(stages):
            if 0 <= step - lag < ngroups:
                stage(step - lag)


def _sb_attn(proj3d):
    kern = functools.partial(_sb_attn_kernel, gw=ATTN_GW, cb=SB_CUMSUM_ROWS)
    return _attn_call(kern, proj3d, SEC_SB_Q, SEC_SB_K, SEC_SB_V, SEC_SB_G, [], [], HEAD_W, "sb_attn")


def _df_attn_kernel(lam_ref, sg_ref, q_ref, k_ref, v_ref, g_ref, o_ref, vt_ref, qs_ref, *, gw, lambda_init):
    seq = k_ref.shape[1]
    _fill_vt(v_ref, vt_ref, gw)

    q = q_ref[0]
    lane = lax.broadcasted_iota(jnp.int32, (seq, HEAD_W), 1)
    qs_ref[0] = jnp.where(lane < DF_QK_DIM, q, jnp.zeros_like(q))
    qs_ref[1] = jnp.where(lane >= DF_QK_DIM, q, jnp.zeros_like(q))

    lam_p = lam_ref[...]
    lam = (jnp.exp(jnp.sum(lam_p[0:1] * lam_p[1:2], axis=-1, keepdims=True))
           - jnp.exp(jnp.sum(lam_p[2:3] * lam_p[3:4], axis=-1, keepdims=True))
           + lambda_init)
    causal = lax.broadcasted_iota(jnp.int32, (gw, gw), 0) <= lax.broadcasted_iota(jnp.int32, (gw, gw), 1)

    ngroups = seq // gw
    scores, probs, outs = {}, {}, {}

    def stage_scores(g, c):
        keys = (g + 1) * gw
        s = lax.dot_general(k_ref[0, :keys, :], qs_ref[c, g * gw:(g + 1) * gw, :], _NT,
                            preferred_element_type=F32)
        s_diag = jnp.where(causal, s[keys - gw:], -jnp.inf)
        s = s_diag if g == 0 else jnp.concatenate([s[:keys - gw], s_diag], axis=0)
        scores[g, c] = (s, jnp.max(s, axis=0, keepdims=True))

    def stage_probs(g, c):
        s, m = scores.pop((g, c))
        probs[g, c] = jnp.exp2(s - m).astype(BF16)

    def stage_out(g, c):
        keys = (g + 1) * gw
        pv = jnp.dot(vt_ref[:, :keys], probs.pop((g, c)), preferred_element_type=F32)
        outs[g, c] = pv[:HEAD_W] * (1.0 / pv[HEAD_W:HEAD_W + 1])
        if c == 1:
            rows = slice(g * gw, (g + 1) * gw)
            o = outs.pop((g, 0)) - lam * outs.pop((g, 1))
            ms = jnp.mean(o * o, axis=0, keepdims=True)
            on = (o * lax.rsqrt(ms + SUBLN_EPS)).T
            gate = g_ref[0, rows, :].astype(F32)
            o_ref[0, rows, :] = (((on * sg_ref[...]) * (1.0 - lambda_init))
                                 * (gate * _sigmoid(gate))).astype(BF16)

    order = list(range(ngroups))
    for step in range(ngroups + 2):
        for c in range(2):
            if step < ngroups:
                stage_scores(order[step], c)
        for c in range(2):
            if 0 <= step - 1 < ngroups:
                stage_probs(order[step - 1], c)
        for c in range(2):
            if 0 <= step - 2 < ngroups:
                stage_out(order[step - 2], c)


def _df_attn(proj3d, lam_params, subln_g, lambda_init):
    s = proj3d.shape[1]
    kern = functools.partial(_df_attn_kernel, gw=ATTN_GW, lambda_init=lambda_init)
    return _attn_call(kern, proj3d, SEC_DF_Q, SEC_DF_K, SEC_DF_V, SEC_DF_G,
                      [lam_params, subln_g.reshape(1, HEAD_W)], [pltpu.VMEM((2, s, HEAD_W), BF16)],
                      HEAD_W + BF16_SUBLANES, "df_attn")


def _tail_kernel(x_ref, msb_ref, mdf_ref, p_ref, wo_sb_ref, wo_df_ref, gp_ref, wg_ref, wp_ref, gf_ref, o_ref,
                 *, final_norm):
    h = (x_ref[...]
         + jnp.dot(msb_ref[...], wo_sb_ref[...], preferred_element_type=F32)
         + jnp.dot(mdf_ref[...], wo_df_ref[...], preferred_element_type=F32))
    ms = jnp.mean(h * h, axis=-1, keepdims=True)
    hn = ((h * lax.rsqrt(ms + NORM_EPS)) * gp_ref[...]).astype(BF16)
    gate = _sigmoid(jnp.dot(hn, wg_ref[...], preferred_element_type=F32))
    pp = jnp.dot(p_ref[...].astype(BF16), wp_ref[...], preferred_element_type=F32)
    h = h + gate * pp
    if final_norm:
        ms = jnp.mean(h * h, axis=-1, keepdims=True)
        h = (h * lax.rsqrt(ms + NORM_EPS)) * gf_ref[...]
    o_ref[...] = h


def _tail(x2d, msb, mdf, p2d, w_out, g_ple, w_gate, w_ple, g_final, final_norm):
    t, d = x2d.shape
    tm = TAIL_TM
    dsb, ddf, dp = msb.shape[1], mdf.shape[1], p2d.shape[1]
    wo_sb, wo_df = w_out[:dsb], w_out[dsb:]
    rows = lambda width: pl.BlockSpec((tm, width), lambda i: (i, 0))
    whole = lambda a: pl.BlockSpec(a.shape, lambda i: (0, 0), pipeline_mode=pl.Buffered(1))
    g_ple, g_final = g_ple.reshape(1, d), g_final.reshape(1, d)
    return pl.pallas_call(
        functools.partial(_tail_kernel, final_norm=final_norm),
        grid=(t // tm,),
        in_specs=[rows(d), rows(dsb), rows(ddf), rows(dp),
                  whole(wo_sb), whole(wo_df), whole(g_ple), whole(w_gate), whole(w_ple), whole(g_final)],
        out_specs=rows(d),
        out_shape=jax.ShapeDtypeStruct((t, d), F32),
        compiler_params=pltpu.CompilerParams(
            dimension_semantics=("arbitrary",), vmem_limit_bytes=VMEM_LIMIT),
        name="tail",
    )(x2d, msb, mdf, p2d, wo_sb, wo_df, g_ple, w_gate, w_ple, g_final)


def kernel(x, p, norm_mix_g, w_in, lambda_q1, lambda_k1, lambda_q2, lambda_k2, subln_g, w_out, norm_ple_g,
           w_ple_gate, w_ple_proj, norm_final_g):
    b, s, d = x.shape
    depth = w_in.shape[0]
    h = x.reshape(b * s, d)
    for i in range(depth):
        lambda_init = 0.8 - 0.6 * math.exp(-0.3 * i)
        proj = _in_proj(h, norm_mix_g[i], w_in[i].astype(BF16), s).reshape(b, s, -1)
        msb = _sb_attn(proj).reshape(b * s, -1)
        lam_params = jnp.stack([lambda_q1[i], lambda_k1[i], lambda_q2[i], lambda_k2[i]])
        mdf = _df_attn(proj, lam_params, subln_g[i], lambda_init).reshape(b * s, -1)
        h = _tail(h, msb, mdf, p[i].reshape(b * s, -1), w_out[i].astype(BF16), norm_ple_g[i],
                  w_ple_gate[i].astype(BF16), w_ple_proj[i].astype(BF16), norm_final_g,
                  final_norm=(i == depth - 1))
    return h.reshape(b, s, d)
```

```python
import functools
import math

import jax
import jax.numpy as jnp
from jax import lax
from jax.experimental import pallas as pl
from jax.experimental.pallas import tpu as pltpu

F32 = jnp.float32
BF16 = jnp.bfloat16

SB_HEADS = 8
DF_HEADS = 8
HEAD_W = 128
BF16_SUBLANES = 16
DF_QK_DIM = 64
ROPE_DIM = DF_QK_DIM // 4
ROPE_THETA = 500000.0
NORM_EPS = 1e-6
SUBLN_EPS = 1e-5

SEC_W = 1024
SEC_SB_Q, SEC_SB_K, SEC_SB_V, SEC_SB_G, SEC_DF_Q, SEC_DF_K, SEC_DF_V, SEC_DF_G = range(8)

IN_PROJ_TM = 1024
IN_PROJ_TN = SEC_W
IN_PROJ_CHUNK = 256
ATTN_GW = 256
SB_CUMSUM_ROWS = 128
TAIL_TM = 256
VMEM_LIMIT = 56 * 1024 * 1024

_NT = (((1,), (1,)), ((), ()))
_LOG2E = math.log2(math.e)
SB_Q_SCALE = HEAD_W ** -0.5 * _LOG2E
DF_Q_SCALE = DF_QK_DIM ** -0.5 * _LOG2E


def _sigmoid(x):
    return 1.0 / (1.0 + jnp.exp(-x))


def _in_proj_kernel(x_ref, g_ref, w_ref, cos_ref, sa_ref, sb_ref, o_ref, hn_ref):
    j = pl.program_id(1)

    @pl.when(j == 0)
    def _():
        x = x_ref[...]
        ms = jnp.mean(x * x, axis=-1, keepdims=True)
        hn_ref[...] = ((x * lax.rsqrt(ms + NORM_EPS)) * g_ref[...]).astype(BF16)

    is_rope = jnp.logical_or(j == SEC_DF_Q, j == SEC_DF_K)
    is_sb_q = j == SEC_SB_Q
    nchunks = IN_PROJ_TN // IN_PROJ_CHUNK

    def project(epilogue):
        chunk = lambda n: jnp.dot(hn_ref[...], w_ref[:, n * IN_PROJ_CHUNK:(n + 1) * IN_PROJ_CHUNK],
                                  preferred_element_type=F32)
        nxt = chunk(0)
        for n in range(nchunks):
            acc, nxt = nxt, (chunk(n + 1) if n + 1 < nchunks else None)
            epilogue(acc, n * IN_PROJ_CHUNK)

    @pl.when(is_rope)
    def _():
        qk_scale = jnp.where(j == SEC_DF_Q, DF_Q_SCALE, 1.0).astype(F32)
        c, sa, sb = cos_ref[...] * qk_scale, sa_ref[...] * qk_scale, sb_ref[...] * qk_scale

        def rope(acc, col):
            for h in range(IN_PROJ_CHUNK // HEAD_W):
                xh = acc[:, h * HEAD_W:(h + 1) * HEAD_W]
                rot = (xh * c + pltpu.roll(xh, HEAD_W - ROPE_DIM // 2, 1) * sa
                       + pltpu.roll(xh, ROPE_DIM // 2, 1) * sb)
                o_ref[:, col + h * HEAD_W:col + (h + 1) * HEAD_W] = rot.astype(BF16)

        project(rope)

    @pl.when(is_sb_q)
    def _():
        def scaled(acc, col):
            o_ref[:, col:col + IN_PROJ_CHUNK] = (acc * SB_Q_SCALE).astype(BF16)

        project(scaled)

    @pl.when(jnp.logical_not(jnp.logical_or(is_rope, is_sb_q)))
    def _():
        def plain(acc, col):
            o_ref[:, col:col + IN_PROJ_CHUNK] = acc.astype(BF16)

        project(plain)


def _rope_tables(seq):
    half = ROPE_DIM // 2
    inv_freq = ROPE_THETA ** (-jnp.arange(0, ROPE_DIM, 2, dtype=F32) / ROPE_DIM)
    ang = jnp.arange(seq, dtype=F32)[:, None] * inv_freq[None, :]
    cos, sin = jnp.cos(ang), jnp.sin(ang)
    lane = jnp.arange(HEAD_W)
    within = lane % DF_QK_DIM
    freq = within % half
    first = within < half
    second = jnp.logical_and(within >= half, within < ROPE_DIM)
    cos_l, sin_l = cos[:, freq], sin[:, freq]
    c = jnp.where(jnp.logical_or(first, second)[None, :], cos_l, 1.0)
    sa = jnp.where(first[None, :], -sin_l, 0.0)
    sb = jnp.where(second[None, :], sin_l, 0.0)
    return c.astype(F32), sa.astype(F32), sb.astype(F32)


def _in_proj(x2d, g, w_bf16, seq):
    t, d = x2d.shape
    n = w_bf16.shape[1]
    tm, tn = IN_PROJ_TM, IN_PROJ_TN
    c, sa, sb = _rope_tables(seq)
    pos_blocks = seq // tm
    tab_spec = pl.BlockSpec((tm, HEAD_W), lambda i, j: (i % pos_blocks, 0))
    return pl.pallas_call(
        _in_proj_kernel,
        grid=(t // tm, n // tn),
        in_specs=[
            pl.BlockSpec((tm, d), lambda i, j: (i, 0)),
            pl.BlockSpec((1, d), lambda i, j: (0, 0)),
            pl.BlockSpec((d, tn), lambda i, j: (0, j)),
            tab_spec, tab_spec, tab_spec,
        ],
        out_specs=pl.BlockSpec((tm, tn), lambda i, j: (i, j)),
        out_shape=jax.ShapeDtypeStruct((t, n), BF16),
        scratch_shapes=[pltpu.VMEM((tm, d), BF16)],
        compiler_params=pltpu.CompilerParams(
            dimension_semantics=("arbitrary", "arbitrary"), vmem_limit_bytes=VMEM_LIMIT),
        name="in_proj",
    )(x2d, g.reshape(1, d), w_bf16, c, sa, sb)


def _fill_vt(v_ref, vt_ref, chunk):
    r = lax.broadcasted_iota(jnp.int32, (HEAD_W, HEAD_W), 0)
    c = lax.broadcasted_iota(jnp.int32, (HEAD_W, HEAD_W), 1)
    eye = jnp.where(r == c, 1.0, 0.0).astype(BF16)
    for lo in range(0, vt_ref.shape[1], chunk):
        vb = v_ref[0, lo:lo + chunk, :]
        vt_ref[:HEAD_W, lo:lo + chunk] = lax.dot_general(eye, vb, _NT, preferred_element_type=F32).astype(BF16)
    if vt_ref.shape[0] > HEAD_W:
        vt_ref[HEAD_W:, :] = jnp.ones((vt_ref.shape[0] - HEAD_W, vt_ref.shape[1]), BF16)


def _sb_stages(q_ref, k_ref, v_ref, g_ref, o_ref, vt_ref, *, gw, cb):
    _fill_vt(v_ref, vt_ref, gw)

    row = lax.broadcasted_iota(jnp.int32, (cb, 2 * cb), 0)
    col = lax.broadcasted_iota(jnp.int32, (cb, 2 * cb), 1) % cb
    suffix = jnp.where(col >= row, 1.0, 0.0).astype(BF16)
    key_pos = lax.broadcasted_iota(jnp.int32, (cb, gw), 0)
    qry_pos = lax.broadcasted_iota(jnp.int32, (cb, gw), 1)
    strict = [key_pos + r * cb < qry_pos for r in range(gw // cb)]

    per_group = gw // cb
    logits, splits, sums, weights = {}, {}, {}, {}

    def diag_mask(g, b):
        return strict[b - g * per_group] if b >= g * per_group else None

    def stage_logits(g):
        logits[g] = lax.dot_general(k_ref[0, :(g + 1) * gw, :], q_ref[0, g * gw:(g + 1) * gw, :], _NT,
                                    preferred_element_type=F32)

    def stage_split(g):
        y = logits[g]
        sp = jnp.maximum(y, 0.0) + jnp.log2(1.0 + jnp.exp2(-jnp.abs(y)))
        parts = []
        for b in range((g + 1) * per_group):
            sp_b = sp[b * cb:(b + 1) * cb]
            if diag_mask(g, b) is not None:
                sp_b = jnp.where(diag_mask(g, b), sp_b, 0.0)
            hi = sp_b.astype(BF16)
            lo = (sp_b - hi.astype(F32)).astype(BF16)
            parts.append(jnp.concatenate([hi, lo], axis=0))
        splits[g] = parts

    def stage_sums(g):
        sums[g] = [jnp.dot(suffix, part, preferred_element_type=F32) for part in splits.pop(g)]

    def stage_weights(g):
        y, block_sums = logits.pop(g), sums.pop(g)
        nblocks = (g + 1) * per_group
        w = [None] * nblocks
        carry = None
        for b in reversed(range(nblocks)):
            y_b = y[b * cb:(b + 1) * cb]
            w_b = jnp.exp2((y_b if carry is None else y_b - carry) - block_sums[b])
            if diag_mask(g, b) is not None:
                w_b = jnp.where(diag_mask(g, b), w_b, 0.0)
            w[b] = w_b.astype(BF16)
            total = block_sums[b][0:1, :]
            carry = total if carry is None else carry + total
        weights[g] = jnp.concatenate(w, axis=0)

    def stage_out(g):
        rows = slice(g * gw, (g + 1) * gw)
        out_t = jnp.dot(vt_ref[:, :(g + 1) * gw], weights.pop(g), preferred_element_type=F32)
        gate = g_ref[0, rows, :].astype(F32)
        o_ref[0, rows, :] = (out_t.T * (gate * _sigmoid(gate))).astype(BF16)

    return stage_logits, stage_split, stage_sums, stage_weights, stage_out


def _df_stages(lam_ref, sg_ref, q_ref, k_ref, v_ref, g_ref, o_ref, vt_ref, qs_ref, *, gw, lambda_init):
    seq = k_ref.shape[1]
    _fill_vt(v_ref, vt_ref, gw)

    q = q_ref[0]
    lane = lax.broadcasted_iota(jnp.int32, (seq, HEAD_W), 1)
    qs_ref[0] = jnp.where(lane < DF_QK_DIM, q, jnp.zeros_like(q))
    qs_ref[1] = jnp.where(lane >= DF_QK_DIM, q, jnp.zeros_like(q))

    lam_p = lam_ref[...]
    lam = (jnp.exp(jnp.sum(lam_p[0:1] * lam_p[1:2], axis=-1, keepdims=True))
           - jnp.exp(jnp.sum(lam_p[2:3] * lam_p[3:4], axis=-1, keepdims=True))
           + lambda_init)
    causal = lax.broadcasted_iota(jnp.int32, (gw, gw), 0) <= lax.broadcasted_iota(jnp.int32, (gw, gw), 1)
    scores, probs = {}, {}

    def stage_scores(g):
        keys = (g + 1) * gw
        for c in range(2):
            s = lax.dot_general(k_ref[0, :keys, :], qs_ref[c, g * gw:(g + 1) * gw, :], _NT,
                                preferred_element_type=F32)
            s_diag = jnp.where(causal, s[keys - gw:], -jnp.inf)
            s = s_diag if g == 0 else jnp.concatenate([s[:keys - gw], s_diag], axis=0)
            scores[g, c] = (s, jnp.max(s, axis=0, keepdims=True))

    def stage_probs(g):
        for c in range(2):
            s, m = scores.pop((g, c))
            probs[g, c] = jnp.exp2(s - m).astype(BF16)

    def stage_out(g):
        keys = (g + 1) * gw
        rows = slice(g * gw, (g + 1) * gw)
        maps = []
        for c in range(2):
            pv = jnp.dot(vt_ref[:, :keys], probs.pop((g, c)), preferred_element_type=F32)
            maps.append(pv[:HEAD_W] * (1.0 / pv[HEAD_W:HEAD_W + 1]))
        o = maps[0] - lam * maps[1]
        ms = jnp.mean(o * o, axis=0, keepdims=True)
        on = (o * lax.rsqrt(ms + SUBLN_EPS)).T
        gate = g_ref[0, rows, :].astype(F32)
        o_ref[0, rows, :] = (((on * sg_ref[...]) * (1.0 - lambda_init)) * (gate * _sigmoid(gate))).astype(BF16)

    return stage_scores, stage_probs, stage_out


def _attn_kernel(lam_ref, sg_ref, sb_q, sb_k, sb_v, sb_g, df_q, df_k, df_v, df_g, sb_o, df_o,
                 sb_vt, df_vt, qs_ref, *, gw, cb, lambda_init):
    sb = _sb_stages(sb_q, sb_k, sb_v, sb_g, sb_o, sb_vt, gw=gw, cb=cb)
    df = _df_stages(lam_ref, sg_ref, df_q, df_k, df_v, df_g, df_o, df_vt, qs_ref, gw=gw, lambda_init=lambda_init)
    schedule = ((0, sb[0]), (0, df[0]), (1, sb[1]), (1, df[1]), (2, sb[2]), (2, df[2]), (3, sb[3]), (4, sb[4]))
    ngroups = sb_k.shape[1] // gw
    for step in range(ngroups + max(lag for lag, _ in schedule)):
        for lag, stage in schedule:
            if 0 <= step - lag < ngroups:
                stage(step - lag)


def _attention(proj3d, lam_params, subln_g, lambda_init):
    b, s, _ = proj3d.shape
    nsec = SEC_W // HEAD_W
    assert SB_HEADS == DF_HEADS == nsec
    head_block = lambda sec: pl.BlockSpec((1, s, HEAD_W), lambda bi, h: (bi, 0, sec * nsec + h))
    whole = lambda a: pl.BlockSpec(a.shape, lambda bi, h: (0, 0))
    subln_g = subln_g.reshape(1, HEAD_W)
    out_block = pl.BlockSpec((1, s, HEAD_W), lambda bi, h: (bi, 0, h))
    out_shape = jax.ShapeDtypeStruct((b, s, SEC_W), BF16)
    kern = functools.partial(_attn_kernel, gw=ATTN_GW, cb=SB_CUMSUM_ROWS, lambda_init=lambda_init)
    return pl.pallas_call(
        kern,
        grid=(b, nsec),
        in_specs=[whole(lam_params), whole(subln_g)]
        + [head_block(sec) for sec in (SEC_SB_Q, SEC_SB_K, SEC_SB_V, SEC_SB_G, SEC_DF_Q, SEC_DF_K, SEC_DF_V, SEC_DF_G)],
        out_specs=[out_block, out_block],
        out_shape=[out_shape, out_shape],
        scratch_shapes=[pltpu.VMEM((HEAD_W, s), BF16),
                        pltpu.VMEM((HEAD_W + BF16_SUBLANES, s), BF16),
                        pltpu.VMEM((2, s, HEAD_W), BF16)],
        compiler_params=pltpu.CompilerParams(
            dimension_semantics=("arbitrary", "arbitrary"), vmem_limit_bytes=VMEM_LIMIT),
        name="attention",
    )(lam_params, subln_g, *([proj3d] * 8))


def _tail_kernel(x_ref, msb_ref, mdf_ref, p_ref, wo_sb_ref, wo_df_ref, gp_ref, wg_ref, wp_ref, gf_ref, o_ref,
                 *, final_norm):
    h = (x_ref[...]
         + jnp.dot(msb_ref[...], wo_sb_ref[...], preferred_element_type=F32)
         + jnp.dot(mdf_ref[...], wo_df_ref[...], preferred_element_type=F32))
    ms = jnp.mean(h * h, axis=-1, keepdims=True)
    hn = ((h * lax.rsqrt(ms + NORM_EPS)) * gp_ref[...]).astype(BF16)
    gate = _sigmoid(jnp.dot(hn, wg_ref[...], preferred_element_type=F32))
    pp = jnp.dot(p_ref[...].astype(BF16), wp_ref[...], preferred_element_type=F32)
    h = h + gate * pp
    if final_norm:
        ms = jnp.mean(h * h, axis=-1, keepdims=True)
        h = (h * lax.rsqrt(ms + NORM_EPS)) * gf_ref[...]
    o_ref[...] = h


def _tail(x2d, msb, mdf, p2d, w_out, g_ple, w_gate, w_ple, g_final, final_norm):
    t, d = x2d.shape
    tm = TAIL_TM
    dsb, ddf, dp = msb.shape[1], mdf.shape[1], p2d.shape[1]
    wo_sb, wo_df = w_out[:dsb], w_out[dsb:]
    rows = lambda width: pl.BlockSpec((tm, width), lambda i: (i, 0))
    whole = lambda a: pl.BlockSpec(a.shape, lambda i: (0, 0), pipeline_mode=pl.Buffered(1))
    g_ple, g_final = g_ple.reshape(1, d), g_final.reshape(1, d)
    return pl.pallas_call(
        functools.partial(_tail_kernel, final_norm=final_norm),
        grid=(t // tm,),
        in_specs=[rows(d), rows(dsb), rows(ddf), rows(dp),
                  whole(wo_sb), whole(wo_df), whole(g_ple), whole(w_gate), whole(w_ple), whole(g_final)],
        out_specs=rows(d),
        out_shape=jax.ShapeDtypeStruct((t, d), F32),
        compiler_params=pltpu.CompilerParams(
            dimension_semantics=("arbitrary",), vmem_limit_bytes=VMEM_LIMIT),
        name="tail",
    )(x2d, msb, mdf, p2d, wo_sb, wo_df, g_ple, w_gate, w_ple, g_final)


def kernel(x, p, norm_mix_g, w_in, lambda_q1, lambda_k1, lambda_q2, lambda_k2, subln_g, w_out, norm_ple_g,
           w_ple_gate, w_ple_proj, norm_final_g):
    b, s, d = x.shape
    depth = w_in.shape[0]
    h = x.reshape(b * s, d)
    for i in range(depth):
        lambda_init = 0.8 - 0.6 * math.exp(-0.3 * i)
        proj = _in_proj(h, norm_mix_g[i], w_in[i].astype(BF16), s).reshape(b, s, -1)
        lam_params = jnp.stack([lambda_q1[i], lambda_k1[i], lambda_q2[i], lambda_k2[i]])
        msb, mdf = _attention(proj, lam_params, subln_g[i], lambda_init)
        msb, mdf = msb.reshape(b * s, -1), mdf.reshape(b * s, -1)
        h = _tail(h, msb, mdf, p[i].reshape(b * s, -1), w_out[i].astype(BF16), norm_ple_g[i],
                  w_ple_gate[i].astype(BF16), w_ple_proj[i].astype(BF16), norm_final_g,
                  final_norm=(i == depth - 1))
    return h.reshape(b, s, d)
```

```python
import functools
import math

import jax
import jax.numpy as jnp
from jax import lax
from jax.experimental import pallas as pl
from jax.experimental.pallas import tpu as pltpu

F32 = jnp.float32
BF16 = jnp.bfloat16

SB_HEADS = 8
DF_HEADS = 8
HEAD_W = 128
BF16_SUBLANES = 16
DF_QK_DIM = 64
ROPE_DIM = DF_QK_DIM // 4
ROPE_THETA = 500000.0
NORM_EPS = 1e-6
SUBLN_EPS = 1e-5

SEC_W = 1024
SEC_SB_Q, SEC_SB_K, SEC_SB_V, SEC_SB_G, SEC_DF_Q, SEC_DF_K, SEC_DF_V, SEC_DF_G = range(8)

IN_PROJ_TM = 1024
IN_PROJ_TN = SEC_W
IN_PROJ_CHUNK = 256
ATTN_GW = 256
SB_CUMSUM_ROWS = 128
TAIL_TM = 256
VMEM_LIMIT = 56 * 1024 * 1024

_NT = (((1,), (1,)), ((), ()))
_LOG2E = math.log2(math.e)
SB_Q_SCALE = HEAD_W ** -0.5 * _LOG2E
DF_Q_SCALE = DF_QK_DIM ** -0.5 * _LOG2E


def _sigmoid(x):
    return 1.0 / (1.0 + jnp.exp(-x))


def _in_proj_kernel(x_ref, g_ref, w_ref, cos_ref, sin_ref, o_ref, hn_ref):
    j = pl.program_id(1)

    @pl.when(j == 0)
    def _():
        x = x_ref[...]
        ms = jnp.mean(x * x, axis=-1, keepdims=True)
        hn_ref[...] = ((x * lax.rsqrt(ms + NORM_EPS)) * g_ref[...]).astype(BF16)

    is_rope = jnp.logical_or(j == SEC_DF_Q, j == SEC_DF_K)
    is_sb_q = j == SEC_SB_Q
    nchunks = IN_PROJ_TN // IN_PROJ_CHUNK

    def project(epilogue):
        chunk = lambda n: jnp.dot(hn_ref[...], w_ref[:, n * IN_PROJ_CHUNK:(n + 1) * IN_PROJ_CHUNK],
                                  preferred_element_type=F32)
        nxt = chunk(0)
        for n in range(nchunks):
            acc, nxt = nxt, (chunk(n + 1) if n + 1 < nchunks else None)
            epilogue(acc, n * IN_PROJ_CHUNK)

    @pl.when(is_rope)
    def _():
        qk_scale = jnp.where(j == SEC_DF_Q, DF_Q_SCALE, 1.0).astype(F32)
        c, s = cos_ref[...] * qk_scale, sin_ref[...] * qk_scale
        lane = lax.broadcasted_iota(jnp.int32, c.shape, 1)
        partner = jnp.where(lane % DF_QK_DIM < ROPE_DIM, lane ^ (ROPE_DIM // 2), lane)

        def rope(acc, col):
            for h in range(IN_PROJ_CHUNK // HEAD_W):
                xh = acc[:, h * HEAD_W:(h + 1) * HEAD_W]
                rot = xh * c + jnp.take_along_axis(xh * s, partner, axis=1)
                o_ref[:, col + h * HEAD_W:col + (h + 1) * HEAD_W] = rot.astype(BF16)

        project(rope)

    @pl.when(is_sb_q)
    def _():
        def scaled(acc, col):
            o_ref[:, col:col + IN_PROJ_CHUNK] = (acc * SB_Q_SCALE).astype(BF16)

        project(scaled)

    @pl.when(jnp.logical_not(jnp.logical_or(is_rope, is_sb_q)))
    def _():
        def plain(acc, col):
            o_ref[:, col:col + IN_PROJ_CHUNK] = acc.astype(BF16)

        project(plain)


def _rope_tables(seq):
    half = ROPE_DIM // 2
    inv_freq = ROPE_THETA ** (-jnp.arange(0, ROPE_DIM, 2, dtype=F32) / ROPE_DIM)
    ang = jnp.arange(seq, dtype=F32)[:, None] * inv_freq[None, :]
    cos, sin = jnp.cos(ang), jnp.sin(ang)
    lane = jnp.arange(HEAD_W)
    within = lane % DF_QK_DIM
    freq = within % half
    first = within < half
    second = jnp.logical_and(within >= half, within < ROPE_DIM)
    cos_l, sin_l = cos[:, freq], sin[:, freq]
    c = jnp.where(jnp.logical_or(first, second)[None, :], cos_l, 1.0)
    sn = jnp.where(first[None, :], sin_l, jnp.where(second[None, :], -sin_l, 0.0))
    return c.astype(F32), sn.astype(F32)


def _in_proj(x2d, g, w_bf16, seq):
    t, d = x2d.shape
    n = w_bf16.shape[1]
    tm, tn = IN_PROJ_TM, IN_PROJ_TN
    c, sn = _rope_tables(seq)
    pos_blocks = seq // tm
    tab_spec = pl.BlockSpec((tm, HEAD_W), lambda i, j: (i % pos_blocks, 0))
    return pl.pallas_call(
        _in_proj_kernel,
        grid=(t // tm, n // tn),
        in_specs=[
            pl.BlockSpec((tm, d), lambda i, j: (i, 0)),
            pl.BlockSpec((1, d), lambda i, j: (0, 0)),
            pl.BlockSpec((d, tn), lambda i, j: (0, j)),
            tab_spec, tab_spec,
        ],
        out_specs=pl.BlockSpec((tm, tn), lambda i, j: (i, j)),
        out_shape=jax.ShapeDtypeStruct((t, n), BF16),
        scratch_shapes=[pltpu.VMEM((tm, d), BF16)],
        compiler_params=pltpu.CompilerParams(
            dimension_semantics=("arbitrary", "arbitrary"), vmem_limit_bytes=VMEM_LIMIT),
        name="in_proj",
    )(x2d, g.reshape(1, d), w_bf16, c, sn)


def _fill_vt(v_ref, vt_ref, chunk):
    r = lax.broadcasted_iota(jnp.int32, (HEAD_W, HEAD_W), 0)
    c = lax.broadcasted_iota(jnp.int32, (HEAD_W, HEAD_W), 1)
    eye = jnp.where(r == c, 1.0, 0.0).astype(BF16)
    for lo in range(0, vt_ref.shape[1], chunk):
        vb = v_ref[0, lo:lo + chunk, :]
        vt_ref[:HEAD_W, lo:lo + chunk] = lax.dot_general(eye, vb, _NT, preferred_element_type=F32).astype(BF16)
    if vt_ref.shape[0] > HEAD_W:
        vt_ref[HEAD_W:, :] = jnp.ones((vt_ref.shape[0] - HEAD_W, vt_ref.shape[1]), BF16)


def _sb_stages(q_ref, k_ref, v_ref, g_ref, o_ref, vt_ref, *, gw, cb):
    _fill_vt(v_ref, vt_ref, gw)

    row = lax.broadcasted_iota(jnp.int32, (cb, 2 * cb), 0)
    col = lax.broadcasted_iota(jnp.int32, (cb, 2 * cb), 1) % cb
    suffix = jnp.where(col >= row, 1.0, 0.0).astype(BF16)
    key_pos = lax.broadcasted_iota(jnp.int32, (cb, gw), 0)
    qry_pos = lax.broadcasted_iota(jnp.int32, (cb, gw), 1)
    strict = [key_pos + r * cb < qry_pos for r in range(gw // cb)]

    per_group = gw // cb
    logits, splits, sums, weights = {}, {}, {}, {}

    def diag_mask(g, b):
        return strict[b - g * per_group] if b >= g * per_group else None

    def stage_logits(g):
        logits[g] = lax.dot_general(k_ref[0, :(g + 1) * gw, :], q_ref[0, g * gw:(g + 1) * gw, :], _NT,
                                    preferred_element_type=F32)

    def stage_split(g):
        y = logits[g]
        sp = jnp.maximum(y, 0.0) + jnp.log2(1.0 + jnp.exp2(-jnp.abs(y)))
        parts = []
        for b in range((g + 1) * per_group):
            sp_b = sp[b * cb:(b + 1) * cb]
            if diag_mask(g, b) is not None:
                sp_b = jnp.where(diag_mask(g, b), sp_b, 0.0)
            hi = sp_b.astype(BF16)
            lo = (sp_b - hi.astype(F32)).astype(BF16)
            parts.append(jnp.concatenate([hi, lo], axis=0))
        splits[g] = parts

    def stage_sums(g):
        sums[g] = [jnp.dot(suffix, part, preferred_element_type=F32) for part in splits.pop(g)]

    def stage_weights(g):
        y, block_sums = logits.pop(g), sums.pop(g)
        nblocks = (g + 1) * per_group
        w = [None] * nblocks
        carry = None
        for b in reversed(range(nblocks)):
            y_b = y[b * cb:(b + 1) * cb]
            w_b = jnp.exp2((y_b if carry is None else y_b - carry) - block_sums[b])
            if diag_mask(g, b) is not None:
                w_b = jnp.where(diag_mask(g, b), w_b, 0.0)
            w[b] = w_b.astype(BF16)
            total = block_sums[b][0:1, :]
            carry = total if carry is None else carry + total
        weights[g] = jnp.concatenate(w, axis=0)

    def stage_out(g):
        rows = slice(g * gw, (g + 1) * gw)
        out_t = jnp.dot(vt_ref[:, :(g + 1) * gw], weights.pop(g), preferred_element_type=F32)
        gate = g_ref[0, rows, :].astype(F32)
        o_ref[0, rows, :] = (out_t.T * (gate * _sigmoid(gate))).astype(BF16)

    return stage_logits, stage_split, stage_sums, stage_weights, stage_out


def _df_stages(lam_ref, sg_ref, q_ref, k_ref, v_ref, g_ref, o_ref, vt_ref, qs_ref, *, gw, lambda_init):
    seq = k_ref.shape[1]
    _fill_vt(v_ref, vt_ref, gw)

    q = q_ref[0]
    lane = lax.broadcasted_iota(jnp.int32, (seq, HEAD_W), 1)
    qs_ref[0] = jnp.where(lane < DF_QK_DIM, q, jnp.zeros_like(q))
    qs_ref[1] = jnp.where(lane >= DF_QK_DIM, q, jnp.zeros_like(q))

    lam_p = lam_ref[...]
    lam = (jnp.exp(jnp.sum(lam_p[0:1] * lam_p[1:2], axis=-1, keepdims=True))
           - jnp.exp(jnp.sum(lam_p[2:3] * lam_p[3:4], axis=-1, keepdims=True))
           + lambda_init)
    causal = lax.broadcasted_iota(jnp.int32, (gw, gw), 0) <= lax.broadcasted_iota(jnp.int32, (gw, gw), 1)
    scores, probs = {}, {}

    def stage_scores(g):
        keys = (g + 1) * gw
        for c in range(2):
            s = lax.dot_general(k_ref[0, :keys, :], qs_ref[c, g * gw:(g + 1) * gw, :], _NT,
                                preferred_element_type=F32)
            s_diag = jnp.where(causal, s[keys - gw:], -jnp.inf)
            s = s_diag if g == 0 else jnp.concatenate([s[:keys - gw], s_diag], axis=0)
            scores[g, c] = (s, jnp.max(s, axis=0, keepdims=True))

    def stage_probs(g):
        for c in range(2):
            s, m = scores.pop((g, c))
            probs[g, c] = jnp.exp2(s - m).astype(BF16)

    def stage_out(g):
        keys = (g + 1) * gw
        rows = slice(g * gw, (g + 1) * gw)
        maps = []
        for c in range(2):
            pv = jnp.dot(vt_ref[:, :keys], probs.pop((g, c)), preferred_element_type=F32)
            maps.append(pv[:HEAD_W] * (1.0 / pv[HEAD_W:HEAD_W + 1]))
        o = maps[0] - lam * maps[1]
        ms = jnp.mean(o * o, axis=0, keepdims=True)
        on = (o * lax.rsqrt(ms + SUBLN_EPS)).T
        gate = g_ref[0, rows, :].astype(F32)
        o_ref[0, rows, :] = (((on * sg_ref[...]) * (1.0 - lambda_init)) * (gate * _sigmoid(gate))).astype(BF16)

    return stage_scores, stage_probs, stage_out


def _attn_kernel(lam_ref, sg_ref, sb_q, sb_k, sb_v, sb_g, df_q, df_k, df_v, df_g, sb_o, df_o,
                 sb_vt, df_vt, qs_ref, *, gw, cb, lambda_init):
    sb = _sb_stages(sb_q, sb_k, sb_v, sb_g, sb_o, sb_vt, gw=gw, cb=cb)
    df = _df_stages(lam_ref, sg_ref, df_q, df_k, df_v, df_g, df_o, df_vt, qs_ref, gw=gw, lambda_init=lambda_init)
    schedule = ((0, sb[0]), (0, df[0]), (1, sb[1]), (1, df[1]), (2, sb[2]), (2, df[2]), (3, sb[3]), (4, sb[4]))
    ngroups = sb_k.shape[1] // gw
    for step in range(ngroups + max(lag for lag, _ in schedule)):
        for lag, stage in schedule:
            if 0 <= step - lag < ngroups:
                stage(step - lag)


def _attention(proj3d, lam_params, subln_g, lambda_init):
    b, s, _ = proj3d.shape
    nsec = SEC_W // HEAD_W
    assert SB_HEADS == DF_HEADS == nsec
    head_block = lambda sec: pl.BlockSpec((1, s, HEAD_W), lambda bi, h: (bi, 0, sec * nsec + h))
    whole = lambda a: pl.BlockSpec(a.shape, lambda bi, h: (0, 0))
    subln_g = subln_g.reshape(1, HEAD_W)
    out_block = pl.BlockSpec((1, s, HEAD_W), lambda bi, h: (bi, 0, h))
    out_shape = jax.ShapeDtypeStruct((b, s, SEC_W), BF16)
    kern = functools.partial(_attn_kernel, gw=ATTN_GW, cb=SB_CUMSUM_ROWS, lambda_init=lambda_init)
    return pl.pallas_call(
        kern,
        grid=(b, nsec),
        in_specs=[whole(lam_params), whole(subln_g)]
        + [head_block(sec) for sec in (SEC_SB_Q, SEC_SB_K, SEC_SB_V, SEC_SB_G, SEC_DF_Q, SEC_DF_K, SEC_DF_V, SEC_DF_G)],
        out_specs=[out_block, out_block],
        out_shape=[out_shape, out_shape],
        scratch_shapes=[pltpu.VMEM((HEAD_W, s), BF16),
                        pltpu.VMEM((HEAD_W + BF16_SUBLANES, s), BF16),
                        pltpu.VMEM((2, s, HEAD_W), BF16)],
        compiler_params=pltpu.CompilerParams(
            dimension_semantics=("arbitrary", "arbitrary"), vmem_limit_bytes=VMEM_LIMIT),
        name="attention",
    )(lam_params, subln_g, *([proj3d] * 8))


def _tail_kernel(x_ref, msb_ref, mdf_ref, p_ref, wo_sb_ref, wo_df_ref, gp_ref, wg_ref, wp_ref, gf_ref, o_ref,
                 *, final_norm):
    h = (x_ref[...]
         + jnp.dot(msb_ref[...], wo_sb_ref[...], preferred_element_type=F32)
         + jnp.dot(mdf_ref[...], wo_df_ref[...], preferred_element_type=F32))
    ms = jnp.mean(h * h, axis=-1, keepdims=True)
    hn = ((h * lax.rsqrt(ms + NORM_EPS)) * gp_ref[...]).astype(BF16)
    gate = _sigmoid(jnp.dot(hn, wg_ref[...], preferred_element_type=F32))
    pp = jnp.dot(p_ref[...].astype(BF16), wp_ref[...], preferred_element_type=F32)
    h = h + gate * pp
    if final_norm:
        ms = jnp.mean(h * h, axis=-1, keepdims=True)
        h = (h * lax.rsqrt(ms + NORM_EPS)) * gf_ref[...]
    o_ref[...] = h


def _tail(x2d, msb, mdf, p2d, w_out, g_ple, w_gate, w_ple, g_final, final_norm):
    t, d = x2d.shape
    tm = TAIL_TM
    dsb, ddf, dp = msb.shape[1], mdf.shape[1], p2d.shape[1]
    wo_sb, wo_df = w_out[:dsb], w_out[dsb:]
    rows = lambda width: pl.BlockSpec((tm, width), lambda i: (i, 0))
    whole = lambda a: pl.BlockSpec(a.shape, lambda i: (0, 0), pipeline_mode=pl.Buffered(1))
    g_ple, g_final = g_ple.reshape(1, d), g_final.reshape(1, d)
    return pl.pallas_call(
        functools.partial(_tail_kernel, final_norm=final_norm),
        grid=(t // tm,),
        in_specs=[rows(d), rows(dsb), rows(ddf), rows(dp),
                  whole(wo_sb), whole(wo_df), whole(g_ple), whole(w_gate), whole(w_ple), whole(g_final)],
        out_specs=rows(d),
        out_shape=jax.ShapeDtypeStruct((t, d), F32),
        compiler_params=pltpu.CompilerParams(
            dimension_semantics=("arbitrary",), vmem_limit_bytes=VMEM_LIMIT),
        name="tail",
    )(x2d, msb, mdf, p2d, wo_sb, wo_df, g_ple, w_gate, w_ple, g_final)


def kernel(x, p, norm_mix_g, w_in, lambda_q1, lambda_k1, lambda_q2, lambda_k2, subln_g, w_out, norm_ple_g,
           w_ple_gate, w_ple_proj, norm_final_g):
    b, s, d = x.shape
    depth = w_in.shape[0]
    h = x.reshape(b * s, d)
    for i in range(depth):
        lambda_init = 0.8 - 0.6 * math.exp(-0.3 * i)
        proj = _in_proj(h, norm_mix_g[i], w_in[i].astype(BF16), s).reshape(b, s, -1)
        lam_params = jnp.stack([lambda_q1[i], lambda_k1[i], lambda_q2[i], lambda_k2[i]])
        msb, mdf = _attention(proj, lam_params, subln_g[i], lambda_init)
        msb, mdf = msb.reshape(b * s, -1), mdf.reshape(b * s, -1)
        h = _tail(h, msb, mdf, p[i].reshape(b * s, -1), w_out[i].astype(BF16), norm_ple_g[i],
                  w_ple_gate[i].astype(BF16), w_ple_proj[i].astype(BF16), norm_final_g,
                  final_norm=(i == depth - 1))
    return h.reshape(b, s, d)
```

```python
import functools
import math

import jax
import jax.numpy as jnp
from jax import lax
from jax.experimental import pallas as pl
from jax.experimental.pallas import tpu as pltpu

F32 = jnp.float32
BF16 = jnp.bfloat16

SB_HEADS = 8
DF_HEADS = 8
HEAD_W = 128
BF16_SUBLANES = 16
DF_QK_DIM = 64
ROPE_DIM = DF_QK_DIM // 4
ROPE_THETA = 500000.0
NORM_EPS = 1e-6
SUBLN_EPS = 1e-5

SEC_W = 1024
SEC_SB_Q, SEC_SB_K, SEC_SB_V, SEC_SB_G, SEC_DF_Q, SEC_DF_K, SEC_DF_V, SEC_DF_G = range(8)

IN_PROJ_TM = 1024
IN_PROJ_TN = SEC_W
IN_PROJ_CHUNK = 256
ATTN_GW = 256
SB_CUMSUM_ROWS = 128
TAIL_TM = 512
TAIL_CHUNK = 256
VMEM_LIMIT = 56 * 1024 * 1024

_NT = (((1,), (1,)), ((), ()))
_LOG2E = math.log2(math.e)
SB_Q_SCALE = HEAD_W ** -0.5 * _LOG2E
DF_Q_SCALE = DF_QK_DIM ** -0.5 * _LOG2E


def _sigmoid(x):
    return 1.0 / (1.0 + jnp.exp(-x))


def _in_proj_kernel(x_ref, g_ref, w_ref, cos_ref, sin_ref, o_ref, hn_ref):
    j = pl.program_id(1)

    @pl.when(j == 0)
    def _():
        x = x_ref[...]
        ms = jnp.mean(x * x, axis=-1, keepdims=True)
        hn_ref[...] = ((x * lax.rsqrt(ms + NORM_EPS)) * g_ref[...]).astype(BF16)

    is_rope = jnp.logical_or(j == SEC_DF_Q, j == SEC_DF_K)
    is_sb_q = j == SEC_SB_Q
    nchunks = IN_PROJ_TN // IN_PROJ_CHUNK

    def project(epilogue):
        chunk = lambda n: jnp.dot(hn_ref[...], w_ref[:, n * IN_PROJ_CHUNK:(n + 1) * IN_PROJ_CHUNK],
                                  preferred_element_type=F32)
        nxt = chunk(0)
        for n in range(nchunks):
            acc, nxt = nxt, (chunk(n + 1) if n + 1 < nchunks else None)
            epilogue(acc, n * IN_PROJ_CHUNK)

    @pl.when(is_rope)
    def _():
        qk_scale = jnp.where(j == SEC_DF_Q, DF_Q_SCALE, 1.0).astype(F32)
        c, s = cos_ref[...] * qk_scale, sin_ref[...] * qk_scale
        lane = lax.broadcasted_iota(jnp.int32, c.shape, 1)
        partner = jnp.where(lane % DF_QK_DIM < ROPE_DIM, lane ^ (ROPE_DIM // 2), lane)

        def rope(acc, col):
            for h in range(IN_PROJ_CHUNK // HEAD_W):
                xh = acc[:, h * HEAD_W:(h + 1) * HEAD_W]
                rot = xh * c + jnp.take_along_axis(xh * s, partner, axis=1)
                o_ref[:, col + h * HEAD_W:col + (h + 1) * HEAD_W] = rot.astype(BF16)

        project(rope)

    @pl.when(is_sb_q)
    def _():
        def scaled(acc, col):
            o_ref[:, col:col + IN_PROJ_CHUNK] = (acc * SB_Q_SCALE).astype(BF16)

        project(scaled)

    @pl.when(jnp.logical_not(jnp.logical_or(is_rope, is_sb_q)))
    def _():
        def plain(acc, col):
            o_ref[:, col:col + IN_PROJ_CHUNK] = acc.astype(BF16)

        project(plain)


def _rope_tables(seq):
    half = ROPE_DIM // 2
    inv_freq = ROPE_THETA ** (-jnp.arange(0, ROPE_DIM, 2, dtype=F32) / ROPE_DIM)
    ang = jnp.arange(seq, dtype=F32)[:, None] * inv_freq[None, :]
    cos, sin = jnp.cos(ang), jnp.sin(ang)
    lane = jnp.arange(HEAD_W)
    within = lane % DF_QK_DIM
    freq = within % half
    first = within < half
    second = jnp.logical_and(within >= half, within < ROPE_DIM)
    cos_l, sin_l = cos[:, freq], sin[:, freq]
    c = jnp.where(jnp.logical_or(first, second)[None, :], cos_l, 1.0)
    sn = jnp.where(first[None, :], sin_l, jnp.where(second[None, :], -sin_l, 0.0))
    return c.astype(F32), sn.astype(F32)


def _in_proj(x2d, g, w_bf16, seq):
    t, d = x2d.shape
    n = w_bf16.shape[1]
    tm, tn = IN_PROJ_TM, IN_PROJ_TN
    c, sn = _rope_tables(seq)
    pos_blocks = seq // tm
    tab_spec = pl.BlockSpec((tm, HEAD_W), lambda i, j: (i % pos_blocks, 0))
    return pl.pallas_call(
        _in_proj_kernel,
        grid=(t // tm, n // tn),
        in_specs=[
            pl.BlockSpec((tm, d), lambda i, j: (i, 0)),
            pl.BlockSpec((1, d), lambda i, j: (0, 0)),
            pl.BlockSpec((d, tn), lambda i, j: (0, j)),
            tab_spec, tab_spec,
        ],
        out_specs=pl.BlockSpec((tm, tn), lambda i, j: (i, j)),
        out_shape=jax.ShapeDtypeStruct((t, n), BF16),
        scratch_shapes=[pltpu.VMEM((tm, d), BF16)],
        compiler_params=pltpu.CompilerParams(
            dimension_semantics=("arbitrary", "arbitrary"), vmem_limit_bytes=VMEM_LIMIT),
        name="in_proj",
    )(x2d, g.reshape(1, d), w_bf16, c, sn)


def _fill_vt(v_ref, vt_ref, chunk):
    r = lax.broadcasted_iota(jnp.int32, (HEAD_W, HEAD_W), 0)
    c = lax.broadcasted_iota(jnp.int32, (HEAD_W, HEAD_W), 1)
    eye = jnp.where(r == c, 1.0, 0.0).astype(BF16)
    for lo in range(0, vt_ref.shape[1], chunk):
        vb = v_ref[0, lo:lo + chunk, :]
        vt_ref[:HEAD_W, lo:lo + chunk] = lax.dot_general(eye, vb, _NT, preferred_element_type=F32).astype(BF16)
    if vt_ref.shape[0] > HEAD_W:
        vt_ref[HEAD_W:, :] = jnp.ones((vt_ref.shape[0] - HEAD_W, vt_ref.shape[1]), BF16)


def _sb_stages(q_ref, k_ref, v_ref, g_ref, o_ref, vt_ref, *, gw, cb):
    _fill_vt(v_ref, vt_ref, gw)

    row = lax.broadcasted_iota(jnp.int32, (cb, 2 * cb), 0)
    col = lax.broadcasted_iota(jnp.int32, (cb, 2 * cb), 1) % cb
    suffix = jnp.where(col >= row, 1.0, 0.0).astype(BF16)
    key_pos = lax.broadcasted_iota(jnp.int32, (cb, gw), 0)
    qry_pos = lax.broadcasted_iota(jnp.int32, (cb, gw), 1)
    strict = [key_pos + r * cb < qry_pos for r in range(gw // cb)]

    per_group = gw // cb
    logits, splits, sums, weights = {}, {}, {}, {}

    def diag_mask(g, b):
        return strict[b - g * per_group] if b >= g * per_group else None

    def stage_logits(g):
        logits[g] = lax.dot_general(k_ref[0, :(g + 1) * gw, :], q_ref[0, g * gw:(g + 1) * gw, :], _NT,
                                    preferred_element_type=F32)

    def stage_split(g):
        y = logits[g]
        sp = jnp.maximum(y, 0.0) + jnp.log2(1.0 + jnp.exp2(-jnp.abs(y)))
        parts = []
        for b in range((g + 1) * per_group):
            sp_b = sp[b * cb:(b + 1) * cb]
            if diag_mask(g, b) is not None:
                sp_b = jnp.where(diag_mask(g, b), sp_b, 0.0)
            hi = sp_b.astype(BF16)
            lo = (sp_b - hi.astype(F32)).astype(BF16)
            parts.append(jnp.concatenate([hi, lo], axis=0))
        splits[g] = parts

    def stage_sums(g):
        sums[g] = [jnp.dot(suffix, part, preferred_element_type=F32) for part in splits.pop(g)]

    def stage_weights(g):
        y, block_sums = logits.pop(g), sums.pop(g)
        nblocks = (g + 1) * per_group
        w = [None] * nblocks
        carry = None
        for b in reversed(range(nblocks)):
            y_b = y[b * cb:(b + 1) * cb]
            w_b = jnp.exp2((y_b if carry is None else y_b - carry) - block_sums[b])
            if diag_mask(g, b) is not None:
                w_b = jnp.where(diag_mask(g, b), w_b, 0.0)
            w[b] = w_b.astype(BF16)
            total = block_sums[b][0:1, :]
            carry = total if carry is None else carry + total
        weights[g] = jnp.concatenate(w, axis=0)

    def stage_out(g):
        rows = slice(g * gw, (g + 1) * gw)
        out_t = jnp.dot(vt_ref[:, :(g + 1) * gw], weights.pop(g), preferred_element_type=F32)
        gate = g_ref[0, rows, :].astype(F32)
        o_ref[0, rows, :] = (out_t.T * (gate * _sigmoid(gate))).astype(BF16)

    return stage_logits, stage_split, stage_sums, stage_weights, stage_out


def _df_stages(lam_ref, sg_ref, q_ref, k_ref, v_ref, g_ref, o_ref, vt_ref, qs_ref, *, gw, lambda_init):
    seq = k_ref.shape[1]
    _fill_vt(v_ref, vt_ref, gw)

    q = q_ref[0]
    lane = lax.broadcasted_iota(jnp.int32, (seq, HEAD_W), 1)
    qs_ref[0] = jnp.where(lane < DF_QK_DIM, q, jnp.zeros_like(q))
    qs_ref[1] = jnp.where(lane >= DF_QK_DIM, q, jnp.zeros_like(q))

    lam_p = lam_ref[...]
    lam = (jnp.exp(jnp.sum(lam_p[0:1] * lam_p[1:2], axis=-1, keepdims=True))
           - jnp.exp(jnp.sum(lam_p[2:3] * lam_p[3:4], axis=-1, keepdims=True))
           + lambda_init)
    causal = lax.broadcasted_iota(jnp.int32, (gw, gw), 0) <= lax.broadcasted_iota(jnp.int32, (gw, gw), 1)
    scores, probs = {}, {}

    def stage_scores(g):
        keys = (g + 1) * gw
        for c in range(2):
            s = lax.dot_general(k_ref[0, :keys, :], qs_ref[c, g * gw:(g + 1) * gw, :], _NT,
                                preferred_element_type=F32)
            s_diag = jnp.where(causal, s[keys - gw:], -jnp.inf)
            s = s_diag if g == 0 else jnp.concatenate([s[:keys - gw], s_diag], axis=0)
            scores[g, c] = (s, jnp.max(s, axis=0, keepdims=True))

    def stage_probs(g):
        for c in range(2):
            s, m = scores.pop((g, c))
            probs[g, c] = jnp.exp2(s - m).astype(BF16)

    def stage_out(g):
        keys = (g + 1) * gw
        rows = slice(g * gw, (g + 1) * gw)
        maps = []
        for c in range(2):
            pv = jnp.dot(vt_ref[:, :keys], probs.pop((g, c)), preferred_element_type=F32)
            maps.append(pv[:HEAD_W] * (1.0 / pv[HEAD_W:HEAD_W + 1]))
        o = maps[0] - lam * maps[1]
        ms = jnp.mean(o * o, axis=0, keepdims=True)
        on = (o * lax.rsqrt(ms + SUBLN_EPS)).T
        gate = g_ref[0, rows, :].astype(F32)
        o_ref[0, rows, :] = (((on * sg_ref[...]) * (1.0 - lambda_init)) * (gate * _sigmoid(gate))).astype(BF16)

    return stage_scores, stage_probs, stage_out


def _attn_kernel(lam_ref, sg_ref, sb_q, sb_k, sb_v, sb_g, df_q, df_k, df_v, df_g, sb_o, df_o,
                 sb_vt, df_vt, qs_ref, *, gw, cb, lambda_init):
    sb = _sb_stages(sb_q, sb_k, sb_v, sb_g, sb_o, sb_vt, gw=gw, cb=cb)
    df = _df_stages(lam_ref, sg_ref, df_q, df_k, df_v, df_g, df_o, df_vt, qs_ref, gw=gw, lambda_init=lambda_init)
    schedule = ((0, sb[0]), (0, df[0]), (1, sb[1]), (1, df[1]), (2, sb[2]), (2, df[2]), (3, sb[3]), (4, sb[4]))
    ngroups = sb_k.shape[1] // gw
    for step in range(ngroups + max(lag for lag, _ in schedule)):
        for lag, stage in schedule:
            if 0 <= step - lag < ngroups:
                stage(step - lag)


def _attention(proj3d, lam_params, subln_g, lambda_init):
    b, s, _ = proj3d.shape
    nsec = SEC_W // HEAD_W
    assert SB_HEADS == DF_HEADS == nsec
    head_block = lambda sec: pl.BlockSpec((1, s, HEAD_W), lambda bi, h: (bi, 0, sec * nsec + h))
    whole = lambda a: pl.BlockSpec(a.shape, lambda bi, h: (0, 0))
    subln_g = subln_g.reshape(1, HEAD_W)
    out_block = pl.BlockSpec((1, s, HEAD_W), lambda bi, h: (bi, 0, h))
    out_shape = jax.ShapeDtypeStruct((b, s, SEC_W), BF16)
    kern = functools.partial(_attn_kernel, gw=ATTN_GW, cb=SB_CUMSUM_ROWS, lambda_init=lambda_init)
    return pl.pallas_call(
        kern,
        grid=(b, nsec),
        in_specs=[whole(lam_params), whole(subln_g)]
        + [head_block(sec) for sec in (SEC_SB_Q, SEC_SB_K, SEC_SB_V, SEC_SB_G, SEC_DF_Q, SEC_DF_K, SEC_DF_V, SEC_DF_G)],
        out_specs=[out_block, out_block],
        out_shape=[out_shape, out_shape],
        scratch_shapes=[pltpu.VMEM((HEAD_W, s), BF16),
                        pltpu.VMEM((HEAD_W + BF16_SUBLANES, s), BF16),
                        pltpu.VMEM((2, s, HEAD_W), BF16)],
        compiler_params=pltpu.CompilerParams(
            dimension_semantics=("arbitrary", "arbitrary"), vmem_limit_bytes=VMEM_LIMIT),
        name="attention",
    )(lam_params, subln_g, *([proj3d] * 8))


def _tail_kernel(x_ref, msb_ref, mdf_ref, p_ref, wo_sb_ref, wo_df_ref, gp_ref, wg_ref, wp_ref, gf_ref, o_ref,
                 *, final_norm):
    d = x_ref.shape[1]
    nchunks = d // TAIL_CHUNK
    cols = lambda n: slice(n * TAIL_CHUNK, (n + 1) * TAIL_CHUNK)

    def pipelined(matmul, epilogue):
        nxt = matmul(0)
        for n in range(nchunks):
            acc, nxt = nxt, (matmul(n + 1) if n + 1 < nchunks else None)
            epilogue(acc, n)

    def rms_scale(sum_sq):
        return lax.rsqrt(sum_sq * (1.0 / d) + NORM_EPS)

    h, sum_sq = [None] * nchunks, []

    def out_proj(n):
        return (jnp.dot(msb_ref[...], wo_sb_ref[:, cols(n)], preferred_element_type=F32)
                + jnp.dot(mdf_ref[...], wo_df_ref[:, cols(n)], preferred_element_type=F32))

    def residual(acc, n):
        h[n] = x_ref[:, cols(n)] + acc
        sum_sq.append(jnp.sum(h[n] * h[n], axis=-1, keepdims=True))

    pipelined(out_proj, residual)
    scale = rms_scale(sum(sum_sq))
    hn = jnp.concatenate([((h[n] * scale) * gp_ref[:, cols(n)]).astype(BF16) for n in range(nchunks)], axis=1)

    p_bf16 = p_ref[...].astype(BF16)
    sum_sq = []

    def gate_proj(n):
        return (jnp.dot(hn, wg_ref[:, cols(n)], preferred_element_type=F32),
                jnp.dot(p_bf16, wp_ref[:, cols(n)], preferred_element_type=F32))

    def gated_add(acc, n):
        gate, pp = acc
        h[n] = h[n] + _sigmoid(gate) * pp
        sum_sq.append(jnp.sum(h[n] * h[n], axis=-1, keepdims=True))

    pipelined(gate_proj, gated_add)
    if final_norm:
        scale = rms_scale(sum(sum_sq))
        for n in range(nchunks):
            o_ref[:, cols(n)] = (h[n] * scale) * gf_ref[:, cols(n)]
    else:
        for n in range(nchunks):
            o_ref[:, cols(n)] = h[n]


def _tail(x2d, msb, mdf, p2d, w_out, g_ple, w_gate, w_ple, g_final, final_norm):
    t, d = x2d.shape
    tm = TAIL_TM
    dsb, ddf, dp = msb.shape[1], mdf.shape[1], p2d.shape[1]
    wo_sb, wo_df = w_out[:dsb], w_out[dsb:]
    rows = lambda width: pl.BlockSpec((tm, width), lambda i: (i, 0))
    whole = lambda a: pl.BlockSpec(a.shape, lambda i: (0, 0), pipeline_mode=pl.Buffered(1))
    g_ple, g_final = g_ple.reshape(1, d), g_final.reshape(1, d)
    return pl.pallas_call(
        functools.partial(_tail_kernel, final_norm=final_norm),
        grid=(t // tm,),
        in_specs=[rows(d), rows(dsb), rows(ddf), rows(dp),
                  whole(wo_sb), whole(wo_df), whole(g_ple), whole(w_gate), whole(w_ple), whole(g_final)],
        out_specs=rows(d),
        out_shape=jax.ShapeDtypeStruct((t, d), F32),
        compiler_params=pltpu.CompilerParams(
            dimension_semantics=("arbitrary",), vmem_limit_bytes=VMEM_LIMIT),
        name="tail",
    )(x2d, msb, mdf, p2d, wo_sb, wo_df, g_ple, w_gate, w_ple, g_final)


def kernel(x, p, norm_mix_g, w_in, lambda_q1, lambda_k1, lambda_q2, lambda_k2, subln_g, w_out, norm_ple_g,
           w_ple_gate, w_ple_proj, norm_final_g):
    b, s, d = x.shape
    depth = w_in.shape[0]
    h = x.reshape(b * s, d)
    for i in range(depth):
        lambda_init = 0.8 - 0.6 * math.exp(-0.3 * i)
        proj = _in_proj(h, norm_mix_g[i], w_in[i].astype(BF16), s).reshape(b, s, -1)
        lam_params = jnp.stack([lambda_q1[i], lambda_k1[i], lambda_q2[i], lambda_k2[i]])
        msb, mdf = _attention(proj, lam_params, subln_g[i], lambda_init)
        msb, mdf = msb.reshape(b * s, -1), mdf.reshape(b * s, -1)
        h = _tail(h, msb, mdf, p[i].reshape(b * s, -1), w_out[i].astype(BF16), norm_ple_g[i],
                  w_ple_gate[i].astype(BF16), w_ple_proj[i].astype(BF16), norm_final_g,
                  final_norm=(i == depth - 1))
    return h.reshape(b, s, d)
```

```python
import functools
import math

import jax
import jax.numpy as jnp
from jax import lax
from jax.experimental import pallas as pl
from jax.experimental.pallas import tpu as pltpu

F32 = jnp.float32
BF16 = jnp.bfloat16

SB_HEADS = 8
DF_HEADS = 8
HEAD_W = 128
BF16_SUBLANES = 16
DF_QK_DIM = 64
ROPE_DIM = DF_QK_DIM // 4
ROPE_THETA = 500000.0
NORM_EPS = 1e-6
SUBLN_EPS = 1e-5

SEC_W = 1024
SEC_SB_Q, SEC_SB_K, SEC_SB_V, SEC_SB_G, SEC_DF_Q, SEC_DF_K, SEC_DF_V, SEC_DF_G = range(8)

IN_PROJ_TM = 1024
IN_PROJ_TN = SEC_W
IN_PROJ_CHUNK = 256
ATTN_GW = 256
SB_CUMSUM_ROWS = 128
TAIL_TM = 512
TAIL_CHUNK = 256
VMEM_LIMIT = 56 * 1024 * 1024

_NT = (((1,), (1,)), ((), ()))
_LOG2E = math.log2(math.e)
SB_Q_SCALE = HEAD_W ** -0.5 * _LOG2E
DF_Q_SCALE = DF_QK_DIM ** -0.5 * _LOG2E


def _sigmoid(x):
    return 1.0 / (1.0 + jnp.exp(-x))


def _in_proj_kernel(x_ref, g_ref, w_ref, cos_ref, sin_ref, o_ref, hn_ref):
    j = pl.program_id(1)

    @pl.when(j == 0)
    def _():
        x = x_ref[...]
        ms = jnp.mean(x * x, axis=-1, keepdims=True)
        hn_ref[...] = ((x * lax.rsqrt(ms + NORM_EPS)) * g_ref[...]).astype(BF16)

    is_rope = jnp.logical_or(j == SEC_DF_Q, j == SEC_DF_K)
    is_sb_q = j == SEC_SB_Q
    nchunks = IN_PROJ_TN // IN_PROJ_CHUNK

    def project(epilogue):
        chunk = lambda n: jnp.dot(hn_ref[...], w_ref[:, n * IN_PROJ_CHUNK:(n + 1) * IN_PROJ_CHUNK].astype(BF16),
                                  preferred_element_type=F32)
        nxt = chunk(0)
        for n in range(nchunks):
            acc, nxt = nxt, (chunk(n + 1) if n + 1 < nchunks else None)
            epilogue(acc, n * IN_PROJ_CHUNK)

    @pl.when(is_rope)
    def _():
        qk_scale = jnp.where(j == SEC_DF_Q, DF_Q_SCALE, 1.0).astype(F32)
        c, s = cos_ref[...] * qk_scale, sin_ref[...] * qk_scale
        lane = lax.broadcasted_iota(jnp.int32, c.shape, 1)
        partner = jnp.where(lane % DF_QK_DIM < ROPE_DIM, lane ^ (ROPE_DIM // 2), lane)

        def rope(acc, col):
            for h in range(IN_PROJ_CHUNK // HEAD_W):
                xh = acc[:, h * HEAD_W:(h + 1) * HEAD_W]
                rot = xh * c + jnp.take_along_axis(xh * s, partner, axis=1)
                o_ref[:, col + h * HEAD_W:col + (h + 1) * HEAD_W] = rot.astype(BF16)

        project(rope)

    @pl.when(is_sb_q)
    def _():
        def scaled(acc, col):
            o_ref[:, col:col + IN_PROJ_CHUNK] = (acc * SB_Q_SCALE).astype(BF16)

        project(scaled)

    @pl.when(jnp.logical_not(jnp.logical_or(is_rope, is_sb_q)))
    def _():
        def plain(acc, col):
            o_ref[:, col:col + IN_PROJ_CHUNK] = acc.astype(BF16)

        project(plain)


def _rope_tables(seq):
    half = ROPE_DIM // 2
    inv_freq = ROPE_THETA ** (-jnp.arange(0, ROPE_DIM, 2, dtype=F32) / ROPE_DIM)
    ang = jnp.arange(seq, dtype=F32)[:, None] * inv_freq[None, :]
    cos, sin = jnp.cos(ang), jnp.sin(ang)
    lane = jnp.arange(HEAD_W)
    within = lane % DF_QK_DIM
    freq = within % half
    first = within < half
    second = jnp.logical_and(within >= half, within < ROPE_DIM)
    cos_l, sin_l = cos[:, freq], sin[:, freq]
    c = jnp.where(jnp.logical_or(first, second)[None, :], cos_l, 1.0)
    sn = jnp.where(first[None, :], sin_l, jnp.where(second[None, :], -sin_l, 0.0))
    return c.astype(F32), sn.astype(F32)


def _in_proj(x2d, g, w, seq):
    t, d = x2d.shape
    n = w.shape[1]
    tm, tn = IN_PROJ_TM, IN_PROJ_TN
    c, sn = _rope_tables(seq)
    pos_blocks = seq // tm
    tab_spec = pl.BlockSpec((tm, HEAD_W), lambda i, j: (i % pos_blocks, 0))
    return pl.pallas_call(
        _in_proj_kernel,
        grid=(t // tm, n // tn),
        in_specs=[
            pl.BlockSpec((tm, d), lambda i, j: (i, 0)),
            pl.BlockSpec((1, d), lambda i, j: (0, 0)),
            pl.BlockSpec((d, tn), lambda i, j: (0, j)),
            tab_spec, tab_spec,
        ],
        out_specs=pl.BlockSpec((tm, tn), lambda i, j: (i, j)),
        out_shape=jax.ShapeDtypeStruct((t, n), BF16),
        scratch_shapes=[pltpu.VMEM((tm, d), BF16)],
        compiler_params=pltpu.CompilerParams(
            dimension_semantics=("arbitrary", "arbitrary"), vmem_limit_bytes=VMEM_LIMIT),
        name="in_proj",
    )(x2d, g.reshape(1, d), w, c, sn)


def _fill_vt(v_ref, vt_ref, chunk):
    r = lax.broadcasted_iota(jnp.int32, (HEAD_W, HEAD_W), 0)
    c = lax.broadcasted_iota(jnp.int32, (HEAD_W, HEAD_W), 1)
    eye = jnp.where(r == c, 1.0, 0.0).astype(BF16)
    for lo in range(0, vt_ref.shape[1], chunk):
        vb = v_ref[0, lo:lo + chunk, :]
        vt_ref[:HEAD_W, lo:lo + chunk] = lax.dot_general(eye, vb, _NT, preferred_element_type=F32).astype(BF16)
    if vt_ref.shape[0] > HEAD_W:
        vt_ref[HEAD_W:, :] = jnp.ones((vt_ref.shape[0] - HEAD_W, vt_ref.shape[1]), BF16)


def _sb_stages(q_ref, k_ref, v_ref, g_ref, o_ref, vt_ref, *, gw, cb):
    _fill_vt(v_ref, vt_ref, gw)

    row = lax.broadcasted_iota(jnp.int32, (cb, 2 * cb), 0)
    col = lax.broadcasted_iota(jnp.int32, (cb, 2 * cb), 1) % cb
    suffix = jnp.where(col >= row, 1.0, 0.0).astype(BF16)
    key_pos = lax.broadcasted_iota(jnp.int32, (cb, gw), 0)
    qry_pos = lax.broadcasted_iota(jnp.int32, (cb, gw), 1)
    strict = [key_pos + r * cb < qry_pos for r in range(gw // cb)]

    per_group = gw // cb
    logits, splits, sums, weights = {}, {}, {}, {}

    def diag_mask(g, b):
        return strict[b - g * per_group] if b >= g * per_group else None

    def stage_logits(g):
        logits[g] = lax.dot_general(k_ref[0, :(g + 1) * gw, :], q_ref[0, g * gw:(g + 1) * gw, :], _NT,
                                    preferred_element_type=F32)

    def stage_split(g):
        y = logits[g]
        sp = jnp.maximum(y, 0.0) + jnp.log2(1.0 + jnp.exp2(-jnp.abs(y)))
        parts = []
        for b in range((g + 1) * per_group):
            sp_b = sp[b * cb:(b + 1) * cb]
            if diag_mask(g, b) is not None:
                sp_b = jnp.where(diag_mask(g, b), sp_b, 0.0)
            hi = sp_b.astype(BF16)
            lo = (sp_b - hi.astype(F32)).astype(BF16)
            parts.append(jnp.concatenate([hi, lo], axis=0))
        splits[g] = parts

    def stage_sums(g):
        sums[g] = [jnp.dot(suffix, part, preferred_element_type=F32) for part in splits.pop(g)]

    def stage_weights(g):
        y, block_sums = logits.pop(g), sums.pop(g)
        nblocks = (g + 1) * per_group
        w = [None] * nblocks
        carry = None
        for b in reversed(range(nblocks)):
            y_b = y[b * cb:(b + 1) * cb]
            w_b = jnp.exp2((y_b if carry is None else y_b - carry) - block_sums[b])
            if diag_mask(g, b) is not None:
                w_b = jnp.where(diag_mask(g, b), w_b, 0.0)
            w[b] = w_b.astype(BF16)
            total = block_sums[b][0:1, :]
            carry = total if carry is None else carry + total
        weights[g] = jnp.concatenate(w, axis=0)

    def stage_out(g):
        rows = slice(g * gw, (g + 1) * gw)
        out_t = jnp.dot(vt_ref[:, :(g + 1) * gw], weights.pop(g), preferred_element_type=F32)
        gate = g_ref[0, rows, :].astype(F32)
        o_ref[0, rows, :] = (out_t.T * (gate * _sigmoid(gate))).astype(BF16)

    return stage_logits, stage_split, stage_sums, stage_weights, stage_out


def _df_stages(lam_ref, sg_ref, q_ref, k_ref, v_ref, g_ref, o_ref, vt_ref, qs_ref, *, gw, lambda_init):
    seq = k_ref.shape[1]
    _fill_vt(v_ref, vt_ref, gw)

    q = q_ref[0]
    lane = lax.broadcasted_iota(jnp.int32, (seq, HEAD_W), 1)
    qs_ref[0] = jnp.where(lane < DF_QK_DIM, q, jnp.zeros_like(q))
    qs_ref[1] = jnp.where(lane >= DF_QK_DIM, q, jnp.zeros_like(q))

    lam_p = lam_ref[...]
    lam = (jnp.exp(jnp.sum(lam_p[0:1] * lam_p[1:2], axis=-1, keepdims=True))
           - jnp.exp(jnp.sum(lam_p[2:3] * lam_p[3:4], axis=-1, keepdims=True))
           + lambda_init)
    causal = lax.broadcasted_iota(jnp.int32, (gw, gw), 0) <= lax.broadcasted_iota(jnp.int32, (gw, gw), 1)
    scores, probs = {}, {}

    def stage_scores(g):
        keys = (g + 1) * gw
        for c in range(2):
            s = lax.dot_general(k_ref[0, :keys, :], qs_ref[c, g * gw:(g + 1) * gw, :], _NT,
                                preferred_element_type=F32)
            s_diag = jnp.where(causal, s[keys - gw:], -jnp.inf)
            s = s_diag if g == 0 else jnp.concatenate([s[:keys - gw], s_diag], axis=0)
            scores[g, c] = (s, jnp.max(s, axis=0, keepdims=True))

    def stage_probs(g):
        for c in range(2):
            s, m = scores.pop((g, c))
            probs[g, c] = jnp.exp2(s - m).astype(BF16)

    def stage_out(g):
        keys = (g + 1) * gw
        rows = slice(g * gw, (g + 1) * gw)
        maps = []
        for c in range(2):
            pv = jnp.dot(vt_ref[:, :keys], probs.pop((g, c)), preferred_element_type=F32)
            maps.append(pv[:HEAD_W] * (1.0 / pv[HEAD_W:HEAD_W + 1]))
        o = maps[0] - lam * maps[1]
        ms = jnp.mean(o * o, axis=0, keepdims=True)
        on = (o * lax.rsqrt(ms + SUBLN_EPS)).T
        gate = g_ref[0, rows, :].astype(F32)
        o_ref[0, rows, :] = (((on * sg_ref[...]) * (1.0 - lambda_init)) * (gate * _sigmoid(gate))).astype(BF16)

    return stage_scores, stage_probs, stage_out


def _attn_kernel(lam_ref, sg_ref, sb_q, sb_k, sb_v, sb_g, df_q, df_k, df_v, df_g, sb_o, df_o,
                 sb_vt, df_vt, qs_ref, *, gw, cb, lambda_init):
    sb = _sb_stages(sb_q, sb_k, sb_v, sb_g, sb_o, sb_vt, gw=gw, cb=cb)
    df = _df_stages(lam_ref, sg_ref, df_q, df_k, df_v, df_g, df_o, df_vt, qs_ref, gw=gw, lambda_init=lambda_init)
    schedule = ((0, sb[0]), (0, df[0]), (1, sb[1]), (1, df[1]), (2, sb[2]), (2, df[2]), (3, sb[3]), (4, sb[4]))
    ngroups = sb_k.shape[1] // gw
    for step in range(ngroups + max(lag for lag, _ in schedule)):
        for lag, stage in schedule:
            if 0 <= step - lag < ngroups:
                stage(step - lag)


def _attention(proj3d, lam_params, subln_g, lambda_init):
    b, s, _ = proj3d.shape
    nsec = SEC_W // HEAD_W
    assert SB_HEADS == DF_HEADS == nsec
    head_block = lambda sec: pl.BlockSpec((1, s, HEAD_W), lambda bi, h: (bi, 0, sec * nsec + h))
    whole = lambda a: pl.BlockSpec(a.shape, lambda bi, h: (0, 0))
    subln_g = subln_g.reshape(1, HEAD_W)
    out_block = pl.BlockSpec((1, s, HEAD_W), lambda bi, h: (bi, 0, h))
    out_shape = jax.ShapeDtypeStruct((b, s, SEC_W), BF16)
    kern = functools.partial(_attn_kernel, gw=ATTN_GW, cb=SB_CUMSUM_ROWS, lambda_init=lambda_init)
    return pl.pallas_call(
        kern,
        grid=(b, nsec),
        in_specs=[whole(lam_params), whole(subln_g)]
        + [head_block(sec) for sec in (SEC_SB_Q, SEC_SB_K, SEC_SB_V, SEC_SB_G, SEC_DF_Q, SEC_DF_K, SEC_DF_V, SEC_DF_G)],
        out_specs=[out_block, out_block],
        out_shape=[out_shape, out_shape],
        scratch_shapes=[pltpu.VMEM((HEAD_W, s), BF16),
                        pltpu.VMEM((HEAD_W + BF16_SUBLANES, s), BF16),
                        pltpu.VMEM((2, s, HEAD_W), BF16)],
        compiler_params=pltpu.CompilerParams(
            dimension_semantics=("arbitrary", "arbitrary"), vmem_limit_bytes=VMEM_LIMIT),
        name="attention",
    )(lam_params, subln_g, *([proj3d] * 8))


def _tail_kernel(x_ref, msb_ref, mdf_ref, p_ref, wo_sb_ref, wo_df_ref, gp_ref, wg_ref, wp_ref, gf_ref, o_ref,
                 *, final_norm):
    d = x_ref.shape[1]
    nchunks = d // TAIL_CHUNK
    cols = lambda n: slice(n * TAIL_CHUNK, (n + 1) * TAIL_CHUNK)

    def pipelined(matmul, epilogue):
        nxt = matmul(0)
        for n in range(nchunks):
            acc, nxt = nxt, (matmul(n + 1) if n + 1 < nchunks else None)
            epilogue(acc, n)

    def rms_scale(sum_sq):
        return lax.rsqrt(sum_sq * (1.0 / d) + NORM_EPS)

    h, sum_sq = [None] * nchunks, []

    def out_proj(n):
        return (jnp.dot(msb_ref[...], wo_sb_ref[:, cols(n)], preferred_element_type=F32)
                + jnp.dot(mdf_ref[...], wo_df_ref[:, cols(n)], preferred_element_type=F32))

    def residual(acc, n):
        h[n] = x_ref[:, cols(n)] + acc
        sum_sq.append(jnp.sum(h[n] * h[n], axis=-1, keepdims=True))

    pipelined(out_proj, residual)
    scale = rms_scale(sum(sum_sq))
    hn = jnp.concatenate([((h[n] * scale) * gp_ref[:, cols(n)]).astype(BF16) for n in range(nchunks)], axis=1)

    p_bf16 = p_ref[...].astype(BF16)
    sum_sq = []

    def gate_proj(n):
        return (jnp.dot(hn, wg_ref[:, cols(n)], preferred_element_type=F32),
                jnp.dot(p_bf16, wp_ref[:, cols(n)], preferred_element_type=F32))

    def gated_add(acc, n):
        gate, pp = acc
        h[n] = h[n] + _sigmoid(gate) * pp
        sum_sq.append(jnp.sum(h[n] * h[n], axis=-1, keepdims=True))

    pipelined(gate_proj, gated_add)
    if final_norm:
        scale = rms_scale(sum(sum_sq))
        for n in range(nchunks):
            o_ref[:, cols(n)] = (h[n] * scale) * gf_ref[:, cols(n)]
    else:
        for n in range(nchunks):
            o_ref[:, cols(n)] = h[n]


def _tail(x2d, msb, mdf, p2d, w_out, g_ple, w_gate, w_ple, g_final, final_norm):
    t, d = x2d.shape
    tm = TAIL_TM
    dsb, ddf, dp = msb.shape[1], mdf.shape[1], p2d.shape[1]
    wo_sb, wo_df = w_out[:dsb], w_out[dsb:]
    rows = lambda width: pl.BlockSpec((tm, width), lambda i: (i, 0))
    whole = lambda a: pl.BlockSpec(a.shape, lambda i: (0, 0), pipeline_mode=pl.Buffered(1))
    g_ple, g_final = g_ple.reshape(1, d), g_final.reshape(1, d)
    return pl.pallas_call(
        functools.partial(_tail_kernel, final_norm=final_norm),
        grid=(t // tm,),
        in_specs=[rows(d), rows(dsb), rows(ddf), rows(dp),
                  whole(wo_sb), whole(wo_df), whole(g_ple), whole(w_gate), whole(w_ple), whole(g_final)],
        out_specs=rows(d),
        out_shape=jax.ShapeDtypeStruct((t, d), F32),
        compiler_params=pltpu.CompilerParams(
            dimension_semantics=("arbitrary",), vmem_limit_bytes=VMEM_LIMIT),
        name="tail",
    )(x2d, msb, mdf, p2d, wo_sb, wo_df, g_ple, w_gate, w_ple, g_final)


def kernel(x, p, norm_mix_g, w_in, lambda_q1, lambda_k1, lambda_q2, lambda_k2, subln_g, w_out, norm_ple_g,
           w_ple_gate, w_ple_proj, norm_final_g):
    b, s, d = x.shape
    depth = w_in.shape[0]
    h = x.reshape(b * s, d)
    for i in range(depth):
        lambda_init = 0.8 - 0.6 * math.exp(-0.3 * i)
        proj = _in_proj(h, norm_mix_g[i], w_in[i], s).reshape(b, s, -1)
        lam_params = jnp.stack([lambda_q1[i], lambda_k1[i], lambda_q2[i], lambda_k2[i]])
        msb, mdf = _attention(proj, lam_params, subln_g[i], lambda_init)
        msb, mdf = msb.reshape(b * s, -1), mdf.reshape(b * s, -1)
        h = _tail(h, msb, mdf, p[i].reshape(b * s, -1), w_out[i].astype(BF16), norm_ple_g[i],
                  w_ple_gate[i].astype(BF16), w_ple_proj[i].astype(BF16), norm_final_g,
                  final_norm=(i == depth - 1))
    return h.reshape(b, s, d)
```

```python
import functools
import math

import jax
import jax.numpy as jnp
from jax import lax
from jax.experimental import pallas as pl
from jax.experimental.pallas import tpu as pltpu

F32 = jnp.float32
BF16 = jnp.bfloat16

SB_HEADS = 8
DF_HEADS = 8
HEAD_W = 128
BF16_SUBLANES = 16
DF_QK_DIM = 64
ROPE_DIM = DF_QK_DIM // 4
ROPE_THETA = 500000.0
NORM_EPS = 1e-6
SUBLN_EPS = 1e-5

SEC_W = 1024
SEC_SB_Q, SEC_SB_K, SEC_SB_V, SEC_SB_G, SEC_DF_Q, SEC_DF_K, SEC_DF_V, SEC_DF_G = range(8)
assert SEC_SB_Q == 0

IN_PROJ_TM = 1024
IN_PROJ_TN = SEC_W
IN_PROJ_CHUNK = 256
IN_PROJ_NORM_ROWS = 256
ATTN_GW = 256
SB_CUMSUM_ROWS = 128
TAIL_TM = 512
TAIL_CHUNK = 256
VMEM_LIMIT = 56 * 1024 * 1024

_NT = (((1,), (1,)), ((), ()))
_LOG2E = math.log2(math.e)
SB_Q_SCALE = HEAD_W ** -0.5 * _LOG2E
DF_Q_SCALE = DF_QK_DIM ** -0.5 * _LOG2E


def _sigmoid(x):
    return 1.0 / (1.0 + jnp.exp(-x))


def _in_proj_kernel(x_ref, g_ref, w_ref, cos_ref, sin_ref, o_ref, hn_ref):
    j = pl.program_id(1)
    is_rope = jnp.logical_or(j == SEC_DF_Q, j == SEC_DF_K)
    is_sb_q = j == SEC_SB_Q
    nchunks = IN_PROJ_TN // IN_PROJ_CHUNK

    def project(epilogue, first=0):
        chunk = lambda n: jnp.dot(hn_ref[...], w_ref[:, n * IN_PROJ_CHUNK:(n + 1) * IN_PROJ_CHUNK],
                                  preferred_element_type=F32)
        nxt = chunk(first)
        for n in range(first, nchunks):
            acc, nxt = nxt, (chunk(n + 1) if n + 1 < nchunks else None)
            epilogue(acc, n * IN_PROJ_CHUNK)

    @pl.when(is_rope)
    def _():
        qk_scale = jnp.where(j == SEC_DF_Q, DF_Q_SCALE, 1.0).astype(F32)
        c, s = cos_ref[...] * qk_scale, sin_ref[...] * qk_scale
        lane = lax.broadcasted_iota(jnp.int32, c.shape, 1)
        partner = jnp.where(lane % DF_QK_DIM < ROPE_DIM, lane ^ (ROPE_DIM // 2), lane)

        def rope(acc, col):
            for h in range(IN_PROJ_CHUNK // HEAD_W):
                xh = acc[:, h * HEAD_W:(h + 1) * HEAD_W]
                rot = xh * c + jnp.take_along_axis(xh * s, partner, axis=1)
                o_ref[:, col + h * HEAD_W:col + (h + 1) * HEAD_W] = rot.astype(BF16)

        project(rope)

    @pl.when(is_sb_q)
    def _():
        def norm(r):
            rows = slice(r * IN_PROJ_NORM_ROWS, (r + 1) * IN_PROJ_NORM_ROWS)
            x = x_ref[rows, :]
            ms = jnp.mean(x * x, axis=-1, keepdims=True)
            hn_ref[rows, :] = ((x * lax.rsqrt(ms + NORM_EPS)) * g_ref[...]).astype(BF16)

        nrows = x_ref.shape[0] // IN_PROJ_NORM_ROWS
        norm(0)
        for r in range(nrows):
            rows = slice(r * IN_PROJ_NORM_ROWS, (r + 1) * IN_PROJ_NORM_ROWS)
            acc = jnp.dot(hn_ref[rows, :], w_ref[:, :IN_PROJ_CHUNK], preferred_element_type=F32)
            if r + 1 < nrows:
                norm(r + 1)
            o_ref[rows, :IN_PROJ_CHUNK] = (acc * SB_Q_SCALE).astype(BF16)

        def scaled(acc, col):
            o_ref[:, col:col + IN_PROJ_CHUNK] = (acc * SB_Q_SCALE).astype(BF16)

        project(scaled, first=1)

    @pl.when(jnp.logical_not(jnp.logical_or(is_rope, is_sb_q)))
    def _():
        def plain(acc, col):
            o_ref[:, col:col + IN_PROJ_CHUNK] = acc.astype(BF16)

        project(plain)


def _rope_tables(seq):
    half = ROPE_DIM // 2
    inv_freq = ROPE_THETA ** (-jnp.arange(0, ROPE_DIM, 2, dtype=F32) / ROPE_DIM)
    ang = jnp.arange(seq, dtype=F32)[:, None] * inv_freq[None, :]
    cos, sin = jnp.cos(ang), jnp.sin(ang)
    lane = jnp.arange(HEAD_W)
    within = lane % DF_QK_DIM
    freq = within % half
    first = within < half
    second = jnp.logical_and(within >= half, within < ROPE_DIM)
    cos_l, sin_l = cos[:, freq], sin[:, freq]
    c = jnp.where(jnp.logical_or(first, second)[None, :], cos_l, 1.0)
    sn = jnp.where(first[None, :], sin_l, jnp.where(second[None, :], -sin_l, 0.0))
    return c.astype(F32), sn.astype(F32)


def _in_proj(x2d, g, w_bf16, seq):
    t, d = x2d.shape
    n = w_bf16.shape[1]
    tm, tn = IN_PROJ_TM, IN_PROJ_TN
    c, sn = _rope_tables(seq)
    pos_blocks = seq // tm
    tab_spec = pl.BlockSpec((tm, HEAD_W), lambda i, j: (i % pos_blocks, 0))
    return pl.pallas_call(
        _in_proj_kernel,
        grid=(t // tm, n // tn),
        in_specs=[
            pl.BlockSpec((tm, d), lambda i, j: (i, 0)),
            pl.BlockSpec((1, d), lambda i, j: (0, 0)),
            pl.BlockSpec((d, tn), lambda i, j: (0, j)),
            tab_spec, tab_spec,
        ],
        out_specs=pl.BlockSpec((tm, tn), lambda i, j: (i, j)),
        out_shape=jax.ShapeDtypeStruct((t, n), BF16),
        scratch_shapes=[pltpu.VMEM((tm, d), BF16)],
        compiler_params=pltpu.CompilerParams(
            dimension_semantics=("arbitrary", "arbitrary"), vmem_limit_bytes=VMEM_LIMIT),
        name="in_proj",
    )(x2d, g.reshape(1, d), w_bf16, c, sn)


def _fill_vt(v_ref, vt_ref, chunk):
    r = lax.broadcasted_iota(jnp.int32, (HEAD_W, HEAD_W), 0)
    c = lax.broadcasted_iota(jnp.int32, (HEAD_W, HEAD_W), 1)
    eye = jnp.where(r == c, 1.0, 0.0).astype(BF16)
    for lo in range(0, vt_ref.shape[1], chunk):
        vb = v_ref[0, lo:lo + chunk, :]
        vt_ref[:HEAD_W, lo:lo + chunk] = lax.dot_general(eye, vb, _NT, preferred_element_type=F32).astype(BF16)
    if vt_ref.shape[0] > HEAD_W:
        vt_ref[HEAD_W:, :] = jnp.ones((vt_ref.shape[0] - HEAD_W, vt_ref.shape[1]), BF16)


def _sb_stages(q_ref, k_ref, v_ref, g_ref, o_ref, vt_ref, *, gw, cb):
    _fill_vt(v_ref, vt_ref, gw)

    row = lax.broadcasted_iota(jnp.int32, (cb, 2 * cb), 0)
    col = lax.broadcasted_iota(jnp.int32, (cb, 2 * cb), 1) % cb
    suffix = jnp.where(col >= row, 1.0, 0.0).astype(BF16)
    key_pos = lax.broadcasted_iota(jnp.int32, (cb, gw), 0)
    qry_pos = lax.broadcasted_iota(jnp.int32, (cb, gw), 1)
    strict = [key_pos + r * cb < qry_pos for r in range(gw // cb)]

    per_group = gw // cb
    logits, splits, sums, weights = {}, {}, {}, {}

    def diag_mask(g, b):
        return strict[b - g * per_group] if b >= g * per_group else None

    def stage_logits(g):
        logits[g] = lax.dot_general(k_ref[0, :(g + 1) * gw, :], q_ref[0, g * gw:(g + 1) * gw, :], _NT,
                                    preferred_element_type=F32)

    def stage_split(g):
        y = logits[g]
        sp = jnp.maximum(y, 0.0) + jnp.log2(1.0 + jnp.exp2(-jnp.abs(y)))
        parts = []
        for b in range((g + 1) * per_group):
            sp_b = sp[b * cb:(b + 1) * cb]
            if diag_mask(g, b) is not None:
                sp_b = jnp.where(diag_mask(g, b), sp_b, 0.0)
            hi = sp_b.astype(BF16)
            lo = (sp_b - hi.astype(F32)).astype(BF16)
            parts.append(jnp.concatenate([hi, lo], axis=0))
        splits[g] = parts

    def stage_sums(g):
        sums[g] = [jnp.dot(suffix, part, preferred_element_type=F32) for part in splits.pop(g)]

    def stage_weights(g):
        y, block_sums = logits.pop(g), sums.pop(g)
        nblocks = (g + 1) * per_group
        w = [None] * nblocks
        carry = None
        for b in reversed(range(nblocks)):
            y_b = y[b * cb:(b + 1) * cb]
            w_b = jnp.exp2((y_b if carry is None else y_b - carry) - block_sums[b])
            if diag_mask(g, b) is not None:
                w_b = jnp.where(diag_mask(g, b), w_b, 0.0)
            w[b] = w_b.astype(BF16)
            total = block_sums[b][0:1, :]
            carry = total if carry is None else carry + total
        weights[g] = jnp.concatenate(w, axis=0)

    def stage_out(g):
        rows = slice(g * gw, (g + 1) * gw)
        out_t = jnp.dot(vt_ref[:, :(g + 1) * gw], weights.pop(g), preferred_element_type=F32)
        gate = g_ref[0, rows, :].astype(F32)
        o_ref[0, rows, :] = (out_t.T * (gate * _sigmoid(gate))).astype(BF16)

    return stage_logits, stage_split, stage_sums, stage_weights, stage_out


def _df_stages(lam_ref, sg_ref, q_ref, k_ref, v_ref, g_ref, o_ref, vt_ref, qs_ref, *, gw, lambda_init):
    seq = k_ref.shape[1]
    _fill_vt(v_ref, vt_ref, gw)

    q = q_ref[0]
    lane = lax.broadcasted_iota(jnp.int32, (seq, HEAD_W), 1)
    qs_ref[0] = jnp.where(lane < DF_QK_DIM, q, jnp.zeros_like(q))
    qs_ref[1] = jnp.where(lane >= DF_QK_DIM, q, jnp.zeros_like(q))

    lam_p = lam_ref[...]
    lam = (jnp.exp(jnp.sum(lam_p[0:1] * lam_p[1:2], axis=-1, keepdims=True))
           - jnp.exp(jnp.sum(lam_p[2:3] * lam_p[3:4], axis=-1, keepdims=True))
           + lambda_init)
    causal = lax.broadcasted_iota(jnp.int32, (gw, gw), 0) <= lax.broadcasted_iota(jnp.int32, (gw, gw), 1)
    scores, probs = {}, {}

    def stage_scores(g):
        keys = (g + 1) * gw
        for c in range(2):
            s = lax.dot_general(k_ref[0, :keys, :], qs_ref[c, g * gw:(g + 1) * gw, :], _NT,
                                preferred_element_type=F32)
            s_diag = jnp.where(causal, s[keys - gw:], -jnp.inf)
            s = s_diag if g == 0 else jnp.concatenate([s[:keys - gw], s_diag], axis=0)
            scores[g, c] = (s, jnp.max(s, axis=0, keepdims=True))

    def stage_probs(g):
        for c in range(2):
            s, m = scores.pop((g, c))
            probs[g, c] = jnp.exp2(s - m).astype(BF16)

    def stage_out(g):
        keys = (g + 1) * gw
        rows = slice(g * gw, (g + 1) * gw)
        maps = []
        for c in range(2):
            pv = jnp.dot(vt_ref[:, :keys], probs.pop((g, c)), preferred_element_type=F32)
            maps.append(pv[:HEAD_W] * (1.0 / pv[HEAD_W:HEAD_W + 1]))
        o = maps[0] - lam * maps[1]
        ms = jnp.mean(o * o, axis=0, keepdims=True)
        on = (o * lax.rsqrt(ms + SUBLN_EPS)).T
        gate = g_ref[0, rows, :].astype(F32)
        o_ref[0, rows, :] = (((on * sg_ref[...]) * (1.0 - lambda_init)) * (gate * _sigmoid(gate))).astype(BF16)

    return stage_scores, stage_probs, stage_out


def _attn_kernel(lam_ref, sg_ref, sb_q, sb_k, sb_v, sb_g, df_q, df_k, df_v, df_g, sb_o, df_o,
                 sb_vt, df_vt, qs_ref, *, gw, cb, lambda_init):
    sb = _sb_stages(sb_q, sb_k, sb_v, sb_g, sb_o, sb_vt, gw=gw, cb=cb)
    df = _df_stages(lam_ref, sg_ref, df_q, df_k, df_v, df_g, df_o, df_vt, qs_ref, gw=gw, lambda_init=lambda_init)
    schedule = ((0, sb[0]), (0, df[0]), (1, sb[1]), (1, df[1]), (2, sb[2]), (2, df[2]), (3, sb[3]), (4, sb[4]))
    ngroups = sb_k.shape[1] // gw
    for step in range(ngroups + max(lag for lag, _ in schedule)):
        for lag, stage in schedule:
            if 0 <= step - lag < ngroups:
                stage(step - lag)


def _attention(proj3d, lam_params, subln_g, lambda_init):
    b, s, _ = proj3d.shape
    nsec = SEC_W // HEAD_W
    assert SB_HEADS == DF_HEADS == nsec
    head_block = lambda sec: pl.BlockSpec((1, s, HEAD_W), lambda bi, h: (bi, 0, sec * nsec + h))
    whole = lambda a: pl.BlockSpec(a.shape, lambda bi, h: (0, 0))
    subln_g = subln_g.reshape(1, HEAD_W)
    out_block = pl.BlockSpec((1, s, HEAD_W), lambda bi, h: (bi, 0, h))
    out_shape = jax.ShapeDtypeStruct((b, s, SEC_W), BF16)
    kern = functools.partial(_attn_kernel, gw=ATTN_GW, cb=SB_CUMSUM_ROWS, lambda_init=lambda_init)
    return pl.pallas_call(
        kern,
        grid=(b, nsec),
        in_specs=[whole(lam_params), whole(subln_g)]
        + [head_block(sec) for sec in (SEC_SB_Q, SEC_SB_K, SEC_SB_V, SEC_SB_G, SEC_DF_Q, SEC_DF_K, SEC_DF_V, SEC_DF_G)],
        out_specs=[out_block, out_block],
        out_shape=[out_shape, out_shape],
        scratch_shapes=[pltpu.VMEM((HEAD_W, s), BF16),
                        pltpu.VMEM((HEAD_W + BF16_SUBLANES, s), BF16),
                        pltpu.VMEM((2, s, HEAD_W), BF16)],
        compiler_params=pltpu.CompilerParams(
            dimension_semantics=("arbitrary", "arbitrary"), vmem_limit_bytes=VMEM_LIMIT),
        name="attention",
    )(lam_params, subln_g, *([proj3d] * 8))


def _tail_kernel(x_ref, msb_ref, mdf_ref, p_ref, wo_sb_ref, wo_df_ref, gp_ref, wg_ref, wp_ref, gf_ref, o_ref,
                 *, final_norm):
    d = x_ref.shape[1]
    nchunks = d // TAIL_CHUNK
    cols = lambda n: slice(n * TAIL_CHUNK, (n + 1) * TAIL_CHUNK)

    def pipelined(matmul, epilogue):
        nxt = matmul(0)
        for n in range(nchunks):
            acc, nxt = nxt, (matmul(n + 1) if n + 1 < nchunks else None)
            epilogue(acc, n)

    def rms_scale(sum_sq):
        return lax.rsqrt(sum_sq * (1.0 / d) + NORM_EPS)

    h, sum_sq = [None] * nchunks, []

    def out_proj(n):
        return (jnp.dot(msb_ref[...], wo_sb_ref[:, cols(n)], preferred_element_type=F32)
                + jnp.dot(mdf_ref[...], wo_df_ref[:, cols(n)], preferred_element_type=F32))

    def residual(acc, n):
        h[n] = x_ref[:, cols(n)] + acc
        sum_sq.append(jnp.sum(h[n] * h[n], axis=-1, keepdims=True))

    pipelined(out_proj, residual)
    scale = rms_scale(sum(sum_sq))
    hn = jnp.concatenate([((h[n] * scale) * gp_ref[:, cols(n)]).astype(BF16) for n in range(nchunks)], axis=1)

    p_bf16 = p_ref[...].astype(BF16)
    sum_sq = []

    def gate_proj(n):
        return (jnp.dot(hn, wg_ref[:, cols(n)], preferred_element_type=F32),
                jnp.dot(p_bf16, wp_ref[:, cols(n)], preferred_element_type=F32))

    def gated_add(acc, n):
        gate, pp = acc
        h[n] = h[n] + _sigmoid(gate) * pp
        sum_sq.append(jnp.sum(h[n] * h[n], axis=-1, keepdims=True))

    pipelined(gate_proj, gated_add)
    if final_norm:
        scale = rms_scale(sum(sum_sq))
        for n in range(nchunks):
            o_ref[:, cols(n)] = (h[n] * scale) * gf_ref[:, cols(n)]
    else:
        for n in range(nchunks):
            o_ref[:, cols(n)] = h[n]


def _tail(x2d, msb, mdf, p2d, w_out, g_ple, w_gate, w_ple, g_final, final_norm):
    t, d = x2d.shape
    tm = TAIL_TM
    dsb, ddf, dp = msb.shape[1], mdf.shape[1], p2d.shape[1]
    wo_sb, wo_df = w_out[:dsb], w_out[dsb:]
    rows = lambda width: pl.BlockSpec((tm, width), lambda i: (i, 0))
    whole = lambda a: pl.BlockSpec(a.shape, lambda i: (0, 0), pipeline_mode=pl.Buffered(1))
    g_ple, g_final = g_ple.reshape(1, d), g_final.reshape(1, d)
    return pl.pallas_call(
        functools.partial(_tail_kernel, final_norm=final_norm),
        grid=(t // tm,),
        in_specs=[rows(d), rows(dsb), rows(ddf), rows(dp),
                  whole(wo_sb), whole(wo_df), whole(g_ple), whole(w_gate), whole(w_ple), whole(g_final)],
        out_specs=rows(d),
        out_shape=jax.ShapeDtypeStruct((t, d), F32),
        compiler_params=pltpu.CompilerParams(
            dimension_semantics=("arbitrary",), vmem_limit_bytes=VMEM_LIMIT),
        name="tail",
    )(x2d, msb, mdf, p2d, wo_sb, wo_df, g_ple, w_gate, w_ple, g_final)


def kernel(x, p, norm_mix_g, w_in, lambda_q1, lambda_k1, lambda_q2, lambda_k2, subln_g, w_out, norm_ple_g,
           w_ple_gate, w_ple_proj, norm_final_g):
    b, s, d = x.shape
    depth = w_in.shape[0]
    h = x.reshape(b * s, d)
    for i in range(depth):
        lambda_init = 0.8 - 0.6 * math.exp(-0.3 * i)
        proj = _in_proj(h, norm_mix_g[i], w_in[i].astype(BF16), s).reshape(b, s, -1)
        lam_params = jnp.stack([lambda_q1[i], lambda_k1[i], lambda_q2[i], lambda_k2[i]])
        msb, mdf = _attention(proj, lam_params, subln_g[i], lambda_init)
        msb, mdf = msb.reshape(b * s, -1), mdf.reshape(b * s, -1)
        h = _tail(h, msb, mdf, p[i].reshape(b * s, -1), w_out[i].astype(BF16), norm_ple_g[i],
                  w_ple_gate[i].astype(BF16), w_ple_proj[i].astype(BF16), norm_final_g,
                  final_norm=(i == depth - 1))
    return h.reshape(b, s, d)
```

```python
import functools
import math

import jax
import jax.numpy as jnp
from jax import lax
from jax.experimental import pallas as pl
from jax.experimental.pallas import tpu as pltpu

F32 = jnp.float32
BF16 = jnp.bfloat16

SB_HEADS = 8
DF_HEADS = 8
HEAD_W = 128
BF16_SUBLANES = 16
DF_QK_DIM = 64
ROPE_DIM = DF_QK_DIM // 4
ROPE_THETA = 500000.0
NORM_EPS = 1e-6
SUBLN_EPS = 1e-5

SEC_W = 1024
SEC_SB_Q, SEC_SB_K, SEC_SB_V, SEC_SB_G, SEC_DF_Q, SEC_DF_K, SEC_DF_V, SEC_DF_G = range(8)

NORM_TM = 1024
PROJ_ROW_SPLIT = 2
ATTN_GW = 256
SB_CUMSUM_ROWS = 128
TAIL_TM = 512
TAIL_CHUNK = 256
VMEM_LIMIT = 56 * 1024 * 1024

_NT = (((1,), (1,)), ((), ()))
_LOG2E = math.log2(math.e)
SB_Q_SCALE = HEAD_W ** -0.5 * _LOG2E
DF_Q_SCALE = DF_QK_DIM ** -0.5 * _LOG2E


def _sigmoid(x):
    return 1.0 / (1.0 + jnp.exp(-x))


def _norm_kernel(x_ref, g_ref, o_ref):
    x = x_ref[...]
    ms = jnp.mean(x * x, axis=-1, keepdims=True)
    o_ref[...] = ((x * lax.rsqrt(ms + NORM_EPS)) * g_ref[...]).astype(BF16)


def _norm(x2d, g):
    t, d = x2d.shape
    assert t % NORM_TM == 0
    return pl.pallas_call(
        _norm_kernel,
        grid=(t // NORM_TM,),
        in_specs=[pl.BlockSpec((NORM_TM, d), lambda i: (i, 0)), pl.BlockSpec((1, d), lambda i: (0, 0))],
        out_specs=pl.BlockSpec((NORM_TM, d), lambda i: (i, 0)),
        out_shape=jax.ShapeDtypeStruct((t, d), BF16),
        compiler_params=pltpu.CompilerParams(dimension_semantics=("arbitrary",), vmem_limit_bytes=VMEM_LIMIT),
        name="norm",
    )(x2d, g.reshape(1, d))


def _rope_tables(seq):
    half = ROPE_DIM // 2
    inv_freq = ROPE_THETA ** (-jnp.arange(0, ROPE_DIM, 2, dtype=F32) / ROPE_DIM)
    ang = jnp.arange(seq, dtype=F32)[:, None] * inv_freq[None, :]
    cos, sin = jnp.cos(ang), jnp.sin(ang)
    lane = jnp.arange(HEAD_W)
    within = lane % DF_QK_DIM
    freq = within % half
    first = within < half
    second = jnp.logical_and(within >= half, within < ROPE_DIM)
    cos_l, sin_l = cos[:, freq], sin[:, freq]
    c = jnp.where(jnp.logical_or(first, second)[None, :], cos_l, 1.0)
    sn = jnp.where(first[None, :], sin_l, jnp.where(second[None, :], -sin_l, 0.0))
    return c.astype(F32), sn.astype(F32)


def _head_major(w):
    d = w.shape[0]
    nsec = w.shape[1] // SEC_W
    nheads = SEC_W // HEAD_W
    return w.reshape(d, nsec, nheads, HEAD_W).transpose(2, 0, 1, 3).reshape(nheads, d, nsec * HEAD_W)


def _proj_piece(hn_ref, w_ref, cos_ref, sin_ref, piece):
    nrows = hn_ref.shape[1] // PROJ_ROW_SPLIT
    r0 = (piece % PROJ_ROW_SPLIT) * nrows
    pair = piece // PROJ_ROW_SPLIT
    c0 = pair * 2 * HEAD_W
    acc = jnp.dot(hn_ref[0, r0:r0 + nrows, :], w_ref[0, :, c0:c0 + 2 * HEAD_W], preferred_element_type=F32)
    blocks = []
    for k in range(2):
        sec = 2 * pair + k
        xh = acc[:, k * HEAD_W:(k + 1) * HEAD_W]
        if sec in (SEC_DF_Q, SEC_DF_K):
            c, s = cos_ref[r0:r0 + nrows, :], sin_ref[r0:r0 + nrows, :]
            if sec == SEC_DF_Q:
                c, s = c * DF_Q_SCALE, s * DF_Q_SCALE
            lane = lax.broadcasted_iota(jnp.int32, c.shape, 1)
            partner = jnp.where(lane % DF_QK_DIM < ROPE_DIM, lane ^ (ROPE_DIM // 2), lane)
            xh = xh * c + jnp.take_along_axis(xh * s, partner, axis=1)
        elif sec == SEC_SB_Q:
            xh = xh * SB_Q_SCALE
        blocks.append(xh.astype(BF16))
    return r0, nrows, c0, jnp.concatenate(blocks, axis=1)


def _fill_vt(v_ref, vt_ref, chunk):
    r = lax.broadcasted_iota(jnp.int32, (HEAD_W, HEAD_W), 0)
    c = lax.broadcasted_iota(jnp.int32, (HEAD_W, HEAD_W), 1)
    eye = jnp.where(r == c, 1.0, 0.0).astype(BF16)
    for lo in range(0, vt_ref.shape[1], chunk):
        vb = v_ref[0, lo:lo + chunk, :]
        vt_ref[:HEAD_W, lo:lo + chunk] = lax.dot_general(eye, vb, _NT, preferred_element_type=F32).astype(BF16)
    if vt_ref.shape[0] > HEAD_W:
        vt_ref[HEAD_W:, :] = jnp.ones((vt_ref.shape[0] - HEAD_W, vt_ref.shape[1]), BF16)


def _sb_stages(q_ref, k_ref, v_ref, g_ref, o_ref, vt_ref, *, gw, cb):
    _fill_vt(v_ref, vt_ref, gw)

    row = lax.broadcasted_iota(jnp.int32, (cb, 2 * cb), 0)
    col = lax.broadcasted_iota(jnp.int32, (cb, 2 * cb), 1) % cb
    suffix = jnp.where(col >= row, 1.0, 0.0).astype(BF16)
    key_pos = lax.broadcasted_iota(jnp.int32, (cb, gw), 0)
    qry_pos = lax.broadcasted_iota(jnp.int32, (cb, gw), 1)
    strict = [key_pos + r * cb < qry_pos for r in range(gw // cb)]

    per_group = gw // cb
    logits, splits, sums, weights = {}, {}, {}, {}

    def diag_mask(g, b):
        return strict[b - g * per_group] if b >= g * per_group else None

    def stage_logits(g):
        logits[g] = lax.dot_general(k_ref[0, :(g + 1) * gw, :], q_ref[0, g * gw:(g + 1) * gw, :], _NT,
                                    preferred_element_type=F32)

    def stage_split(g):
        y = logits[g]
        sp = jnp.maximum(y, 0.0) + jnp.log2(1.0 + jnp.exp2(-jnp.abs(y)))
        parts = []
        for b in range((g + 1) * per_group):
            sp_b = sp[b * cb:(b + 1) * cb]
            if diag_mask(g, b) is not None:
                sp_b = jnp.where(diag_mask(g, b), sp_b, 0.0)
            hi = sp_b.astype(BF16)
            lo = (sp_b - hi.astype(F32)).astype(BF16)
            parts.append(jnp.concatenate([hi, lo], axis=0))
        splits[g] = parts

    def stage_sums(g):
        sums[g] = [jnp.dot(suffix, part, preferred_element_type=F32) for part in splits.pop(g)]

    def stage_weights(g):
        y, block_sums = logits.pop(g), sums.pop(g)
        nblocks = (g + 1) * per_group
        w = [None] * nblocks
        carry = None
        for b in reversed(range(nblocks)):
            y_b = y[b * cb:(b + 1) * cb]
            w_b = jnp.exp2((y_b if carry is None else y_b - carry) - block_sums[b])
            if diag_mask(g, b) is not None:
                w_b = jnp.where(diag_mask(g, b), w_b, 0.0)
            w[b] = w_b.astype(BF16)
            total = block_sums[b][0:1, :]
            carry = total if carry is None else carry + total
        weights[g] = jnp.concatenate(w, axis=0)

    def stage_out(g):
        rows = slice(g * gw, (g + 1) * gw)
        out_t = jnp.dot(vt_ref[:, :(g + 1) * gw], weights.pop(g), preferred_element_type=F32)
        gate = g_ref[0, rows, :].astype(F32)
        o_ref[0, rows, :] = (out_t.T * (gate * _sigmoid(gate))).astype(BF16)

    return stage_logits, stage_split, stage_sums, stage_weights, stage_out


def _df_stages(lam_ref, sg_ref, q_ref, k_ref, v_ref, g_ref, o_ref, vt_ref, qs_ref, *, gw, lambda_init):
    seq = k_ref.shape[1]
    _fill_vt(v_ref, vt_ref, gw)

    q = q_ref[0]
    lane = lax.broadcasted_iota(jnp.int32, (seq, HEAD_W), 1)
    qs_ref[0] = jnp.where(lane < DF_QK_DIM, q, jnp.zeros_like(q))
    qs_ref[1] = jnp.where(lane >= DF_QK_DIM, q, jnp.zeros_like(q))

    lam_p = lam_ref[...]
    lam = (jnp.exp(jnp.sum(lam_p[0:1] * lam_p[1:2], axis=-1, keepdims=True))
           - jnp.exp(jnp.sum(lam_p[2:3] * lam_p[3:4], axis=-1, keepdims=True))
           + lambda_init)
    causal = lax.broadcasted_iota(jnp.int32, (gw, gw), 0) <= lax.broadcasted_iota(jnp.int32, (gw, gw), 1)
    scores, probs = {}, {}

    def stage_scores(g):
        keys = (g + 1) * gw
        for c in range(2):
            s = lax.dot_general(k_ref[0, :keys, :], qs_ref[c, g * gw:(g + 1) * gw, :], _NT,
                                preferred_element_type=F32)
            s_diag = jnp.where(causal, s[keys - gw:], -jnp.inf)
            s = s_diag if g == 0 else jnp.concatenate([s[:keys - gw], s_diag], axis=0)
            scores[g, c] = (s, jnp.max(s, axis=0, keepdims=True))

    def stage_probs(g):
        for c in range(2):
            s, m = scores.pop((g, c))
            probs[g, c] = jnp.exp2(s - m).astype(BF16)

    def stage_out(g):
        keys = (g + 1) * gw
        rows = slice(g * gw, (g + 1) * gw)
        maps = []
        for c in range(2):
            pv = jnp.dot(vt_ref[:, :keys], probs.pop((g, c)), preferred_element_type=F32)
            maps.append(pv[:HEAD_W] * (1.0 / pv[HEAD_W:HEAD_W + 1]))
        o = maps[0] - lam * maps[1]
        ms = jnp.mean(o * o, axis=0, keepdims=True)
        on = (o * lax.rsqrt(ms + SUBLN_EPS)).T
        gate = g_ref[0, rows, :].astype(F32)
        o_ref[0, rows, :] = (((on * sg_ref[...]) * (1.0 - lambda_init)) * (gate * _sigmoid(gate))).astype(BF16)

    return stage_scores, stage_probs, stage_out


def _attn_kernel(lam_ref, sg_ref, cos_ref, sin_ref, hn_ref, w_ref, sb_o, df_o,
                 proj_ref, sb_vt, df_vt, qs_ref, *, gw, cb, lambda_init):
    @pl.when(pl.program_id(0) == 0)
    def _():
        proj_ref[...] = jnp.zeros_like(proj_ref)

    section = lambda sec: proj_ref.at[:, :, pl.ds(sec * HEAD_W, HEAD_W)]
    sb = _sb_stages(section(SEC_SB_Q), section(SEC_SB_K), section(SEC_SB_V), section(SEC_SB_G), sb_o, sb_vt,
                    gw=gw, cb=cb)
    df = _df_stages(lam_ref, sg_ref, section(SEC_DF_Q), section(SEC_DF_K), section(SEC_DF_V), section(SEC_DF_G),
                    df_o, df_vt, qs_ref, gw=gw, lambda_init=lambda_init)
    schedule = ((0, sb[0]), (0, df[0]), (1, sb[1]), (1, df[1]), (2, sb[2]), (2, df[2]), (3, sb[3]), (4, sb[4]))
    ngroups = proj_ref.shape[1] // gw
    npieces = PROJ_ROW_SPLIT * (proj_ref.shape[2] // (2 * HEAD_W))
    assert npieces <= ngroups
    next_pair = []
    for step in range(ngroups + max(lag for lag, _ in schedule)):
        for lag, stage in schedule:
            if 0 <= step - lag < ngroups:
                stage(step - lag)
        if step < npieces:
            next_pair.append(_proj_piece(hn_ref, w_ref, cos_ref, sin_ref, step))
    for r0, nrows, c0, block in next_pair:
        proj_ref[0, r0:r0 + nrows, c0:c0 + 2 * HEAD_W] = block


def _attention(hn3d, w_heads, lam_params, subln_g, lambda_init):
    b, s, d = hn3d.shape
    nheads, _, head_cols = w_heads.shape
    assert SB_HEADS == DF_HEADS == nheads and head_cols == 8 * HEAD_W and s % ATTN_GW == 0
    npairs = b * nheads
    cos, sin = _rope_tables(s)
    subln_g = subln_g.reshape(1, HEAD_W)
    whole = lambda a: pl.BlockSpec(a.shape, lambda t: (0,) * a.ndim, pipeline_mode=pl.Buffered(1))
    attended = lambda t: jnp.maximum(t - 1, 0)
    projected = lambda t: jnp.minimum(t, npairs - 1)
    out_block = pl.BlockSpec((1, s, HEAD_W), lambda t: (attended(t) // nheads, 0, attended(t) % nheads))
    out_shape = jax.ShapeDtypeStruct((b, s, nheads * HEAD_W), BF16)
    kern = functools.partial(_attn_kernel, gw=ATTN_GW, cb=SB_CUMSUM_ROWS, lambda_init=lambda_init)
    return pl.pallas_call(
        kern,
        grid=(npairs + 1,),
        in_specs=[
            whole(lam_params), whole(subln_g), whole(cos), whole(sin),
            pl.BlockSpec((1, s, d), lambda t: (projected(t) // nheads, 0, 0), pipeline_mode=pl.Buffered(1)),
            pl.BlockSpec((1, d, head_cols), lambda t: (projected(t) % nheads, 0, 0)),
        ],
        out_specs=[out_block, out_block],
        out_shape=[out_shape, out_shape],
        scratch_shapes=[pltpu.VMEM((1, s, head_cols), BF16),
                        pltpu.VMEM((HEAD_W, s), BF16),
                        pltpu.VMEM((HEAD_W + BF16_SUBLANES, s), BF16),
                        pltpu.VMEM((2, s, HEAD_W), BF16)],
        compiler_params=pltpu.CompilerParams(dimension_semantics=("arbitrary",), vmem_limit_bytes=VMEM_LIMIT),
        name="attention",
    )(lam_params, subln_g, cos, sin, hn3d, w_heads)


def _tail_kernel(x_ref, msb_ref, mdf_ref, p_ref, wo_sb_ref, wo_df_ref, gp_ref, wg_ref, wp_ref, gf_ref, o_ref,
                 *, final_norm):
    d = x_ref.shape[1]
    nchunks = d // TAIL_CHUNK
    cols = lambda n: slice(n * TAIL_CHUNK, (n + 1) * TAIL_CHUNK)

    def pipelined(matmul, epilogue):
        nxt = matmul(0)
        for n in range(nchunks):
            acc, nxt = nxt, (matmul(n + 1) if n + 1 < nchunks else None)
            epilogue(acc, n)

    def rms_scale(sum_sq):
        return lax.rsqrt(sum_sq * (1.0 / d) + NORM_EPS)

    h, sum_sq = [None] * nchunks, []

    def out_proj(n):
        return (jnp.dot(msb_ref[...], wo_sb_ref[:, cols(n)], preferred_element_type=F32)
                + jnp.dot(mdf_ref[...], wo_df_ref[:, cols(n)], preferred_element_type=F32))

    def residual(acc, n):
        h[n] = x_ref[:, cols(n)] + acc
        sum_sq.append(jnp.sum(h[n] * h[n], axis=-1, keepdims=True))

    pipelined(out_proj, residual)
    scale = rms_scale(sum(sum_sq))
    hn = jnp.concatenate([((h[n] * scale) * gp_ref[:, cols(n)]).astype(BF16) for n in range(nchunks)], axis=1)

    p_bf16 = p_ref[...].astype(BF16)
    sum_sq = []

    def gate_proj(n):
        return (jnp.dot(hn, wg_ref[:, cols(n)], preferred_element_type=F32),
                jnp.dot(p_bf16, wp_ref[:, cols(n)], preferred_element_type=F32))

    def gated_add(acc, n):
        gate, pp = acc
        h[n] = h[n] + _sigmoid(gate) * pp
        sum_sq.append(jnp.sum(h[n] * h[n], axis=-1, keepdims=True))

    pipelined(gate_proj, gated_add)
    if final_norm:
        scale = rms_scale(sum(sum_sq))
        for n in range(nchunks):
            o_ref[:, cols(n)] = (h[n] * scale) * gf_ref[:, cols(n)]
    else:
        for n in range(nchunks):
            o_ref[:, cols(n)] = h[n]


def _tail(x2d, msb, mdf, p2d, w_out, g_ple, w_gate, w_ple, g_final, final_norm):
    t, d = x2d.shape
    tm = TAIL_TM
    assert t % tm == 0 and d % TAIL_CHUNK == 0
    dsb, ddf, dp = msb.shape[1], mdf.shape[1], p2d.shape[1]
    wo_sb, wo_df = w_out[:dsb], w_out[dsb:]
    rows = lambda width: pl.BlockSpec((tm, width), lambda i: (i, 0))
    whole = lambda a: pl.BlockSpec(a.shape, lambda i: (0, 0), pipeline_mode=pl.Buffered(1))
    g_ple, g_final = g_ple.reshape(1, d), g_final.reshape(1, d)
    return pl.pallas_call(
        functools.partial(_tail_kernel, final_norm=final_norm),
        grid=(t // tm,),
        in_specs=[rows(d), rows(dsb), rows(ddf), rows(dp),
                  whole(wo_sb), whole(wo_df), whole(g_ple), whole(w_gate), whole(w_ple), whole(g_final)],
        out_specs=rows(d),
        out_shape=jax.ShapeDtypeStruct((t, d), F32),
        compiler_params=pltpu.CompilerParams(
            dimension_semantics=("arbitrary",), vmem_limit_bytes=VMEM_LIMIT),
        name="tail",
    )(x2d, msb, mdf, p2d, wo_sb, wo_df, g_ple, w_gate, w_ple, g_final)


def kernel(x, p, norm_mix_g, w_in, lambda_q1, lambda_k1, lambda_q2, lambda_k2, subln_g, w_out, norm_ple_g,
           w_ple_gate, w_ple_proj, norm_final_g):
    b, s, d = x.shape
    depth = w_in.shape[0]
    h = x.reshape(b * s, d)
    for i in range(depth):
        lambda_init = 0.8 - 0.6 * math.exp(-0.3 * i)
        hn = _norm(h, norm_mix_g[i]).reshape(b, s, d)
        lam_params = jnp.stack([lambda_q1[i], lambda_k1[i], lambda_q2[i], lambda_k2[i]])
        msb, mdf = _attention(hn, _head_major(w_in[i]).astype(BF16), lam_params, subln_g[i], lambda_init)
        msb, mdf = msb.reshape(b * s, -1), mdf.reshape(b * s, -1)
        h = _tail(h, msb, mdf, p[i].reshape(b * s, -1), w_out[i].astype(BF16), norm_ple_g[i],
                  w_ple_gate[i].astype(BF16), w_ple_proj[i].astype(BF16), norm_final_g,
                  final_norm=(i == depth - 1))
    return h.reshape(b, s, d)
```

```python
import functools
import math

import jax
import jax.numpy as jnp
from jax import lax
from jax.experimental import pallas as pl
from jax.experimental.pallas import tpu as pltpu

F32 = jnp.float32
BF16 = jnp.bfloat16

SB_HEADS = 8
DF_HEADS = 8
HEAD_W = 128
BF16_SUBLANES = 16
DF_QK_DIM = 64
ROPE_DIM = DF_QK_DIM // 4
ROPE_THETA = 500000.0
NORM_EPS = 1e-6
SUBLN_EPS = 1e-5

SEC_W = 1024
SEC_SB_Q, SEC_SB_K, SEC_SB_V, SEC_SB_G, SEC_DF_Q, SEC_DF_K, SEC_DF_V, SEC_DF_G = range(8)

NORM_TM = 1024
PROJ_ROW_SPLIT = 2
ATTN_GW = 256
SB_CUMSUM_ROWS = 128
TAIL_TM = 512
TAIL_CHUNK = 256
VMEM_LIMIT = 56 * 1024 * 1024

_NT = (((1,), (1,)), ((), ()))
_LOG2E = math.log2(math.e)
SB_Q_SCALE = HEAD_W ** -0.5 * _LOG2E
DF_Q_SCALE = DF_QK_DIM ** -0.5 * _LOG2E


def _sigmoid(x):
    return 1.0 / (1.0 + jnp.exp(-x))


def _norm_kernel(x_ref, g_ref, o_ref):
    x = x_ref[...]
    ms = jnp.mean(x * x, axis=-1, keepdims=True)
    o_ref[...] = ((x * lax.rsqrt(ms + NORM_EPS)) * g_ref[...]).astype(BF16)


def _norm(x2d, g):
    t, d = x2d.shape
    assert t % NORM_TM == 0
    return pl.pallas_call(
        _norm_kernel,
        grid=(t // NORM_TM,),
        in_specs=[pl.BlockSpec((NORM_TM, d), lambda i: (i, 0)), pl.BlockSpec((1, d), lambda i: (0, 0))],
        out_specs=pl.BlockSpec((NORM_TM, d), lambda i: (i, 0)),
        out_shape=jax.ShapeDtypeStruct((t, d), BF16),
        compiler_params=pltpu.CompilerParams(dimension_semantics=("arbitrary",), vmem_limit_bytes=VMEM_LIMIT),
        name="norm",
    )(x2d, g.reshape(1, d))


def _rope_tables(seq):
    half = ROPE_DIM // 2
    inv_freq = ROPE_THETA ** (-jnp.arange(0, ROPE_DIM, 2, dtype=F32) / ROPE_DIM)
    ang = jnp.arange(seq, dtype=F32)[:, None] * inv_freq[None, :]
    cos, sin = jnp.cos(ang), jnp.sin(ang)
    lane = jnp.arange(HEAD_W)
    within = lane % DF_QK_DIM
    freq = within % half
    first = within < half
    second = jnp.logical_and(within >= half, within < ROPE_DIM)
    cos_l, sin_l = cos[:, freq], sin[:, freq]
    c = jnp.where(jnp.logical_or(first, second)[None, :], cos_l, 1.0)
    sn = jnp.where(first[None, :], sin_l, jnp.where(second[None, :], -sin_l, 0.0))
    return c.astype(F32), sn.astype(F32)


def _proj_piece(hn_ref, w_refs, cos_ref, sin_ref, piece):
    nrows = hn_ref.shape[1] // PROJ_ROW_SPLIT
    r0 = (piece % PROJ_ROW_SPLIT) * nrows
    pair = piece // PROJ_ROW_SPLIT
    c0 = pair * 2 * HEAD_W
    w_pair = jnp.concatenate([w_refs[2 * pair][...], w_refs[2 * pair + 1][...]], axis=1)
    acc = jnp.dot(hn_ref[0, r0:r0 + nrows, :], w_pair, preferred_element_type=F32)
    blocks = []
    for k in range(2):
        sec = 2 * pair + k
        xh = acc[:, k * HEAD_W:(k + 1) * HEAD_W]
        if sec in (SEC_DF_Q, SEC_DF_K):
            c, s = cos_ref[r0:r0 + nrows, :], sin_ref[r0:r0 + nrows, :]
            if sec == SEC_DF_Q:
                c, s = c * DF_Q_SCALE, s * DF_Q_SCALE
            lane = lax.broadcasted_iota(jnp.int32, c.shape, 1)
            partner = jnp.where(lane % DF_QK_DIM < ROPE_DIM, lane ^ (ROPE_DIM // 2), lane)
            xh = xh * c + jnp.take_along_axis(xh * s, partner, axis=1)
        elif sec == SEC_SB_Q:
            xh = xh * SB_Q_SCALE
        blocks.append(xh.astype(BF16))
    return r0, nrows, c0, jnp.concatenate(blocks, axis=1)


def _fill_vt(v_ref, vt_ref, chunk):
    r = lax.broadcasted_iota(jnp.int32, (HEAD_W, HEAD_W), 0)
    c = lax.broadcasted_iota(jnp.int32, (HEAD_W, HEAD_W), 1)
    eye = jnp.where(r == c, 1.0, 0.0).astype(BF16)
    for lo in range(0, vt_ref.shape[1], chunk):
        vb = v_ref[0, lo:lo + chunk, :]
        vt_ref[:HEAD_W, lo:lo + chunk] = lax.dot_general(eye, vb, _NT, preferred_element_type=F32).astype(BF16)
    if vt_ref.shape[0] > HEAD_W:
        vt_ref[HEAD_W:, :] = jnp.ones((vt_ref.shape[0] - HEAD_W, vt_ref.shape[1]), BF16)


def _sb_stages(q_ref, k_ref, v_ref, g_ref, o_ref, vt_ref, *, gw, cb):
    _fill_vt(v_ref, vt_ref, gw)

    row = lax.broadcasted_iota(jnp.int32, (cb, 2 * cb), 0)
    col = lax.broadcasted_iota(jnp.int32, (cb, 2 * cb), 1) % cb
    suffix = jnp.where(col >= row, 1.0, 0.0).astype(BF16)
    key_pos = lax.broadcasted_iota(jnp.int32, (cb, gw), 0)
    qry_pos = lax.broadcasted_iota(jnp.int32, (cb, gw), 1)
    strict = [key_pos + r * cb < qry_pos for r in range(gw // cb)]

    per_group = gw // cb
    logits, splits, sums, weights = {}, {}, {}, {}

    def diag_mask(g, b):
        return strict[b - g * per_group] if b >= g * per_group else None

    def stage_logits(g):
        logits[g] = lax.dot_general(k_ref[0, :(g + 1) * gw, :], q_ref[0, g * gw:(g + 1) * gw, :], _NT,
                                    preferred_element_type=F32)

    def stage_split(g):
        y = logits[g]
        sp = jnp.maximum(y, 0.0) + jnp.log2(1.0 + jnp.exp2(-jnp.abs(y)))
        parts = []
        for b in range((g + 1) * per_group):
            sp_b = sp[b * cb:(b + 1) * cb]
            if diag_mask(g, b) is not None:
                sp_b = jnp.where(diag_mask(g, b), sp_b, 0.0)
            hi = sp_b.astype(BF16)
            lo = (sp_b - hi.astype(F32)).astype(BF16)
            parts.append(jnp.concatenate([hi, lo], axis=0))
        splits[g] = parts

    def stage_sums(g):
        sums[g] = [jnp.dot(suffix, part, preferred_element_type=F32) for part in splits.pop(g)]

    def stage_weights(g):
        y, block_sums = logits.pop(g), sums.pop(g)
        nblocks = (g + 1) * per_group
        w = [None] * nblocks
        carry = None
        for b in reversed(range(nblocks)):
            y_b = y[b * cb:(b + 1) * cb]
            w_b = jnp.exp2((y_b if carry is None else y_b - carry) - block_sums[b])
            if diag_mask(g, b) is not None:
                w_b = jnp.where(diag_mask(g, b), w_b, 0.0)
            w[b] = w_b.astype(BF16)
            total = block_sums[b][0:1, :]
            carry = total if carry is None else carry + total
        weights[g] = jnp.concatenate(w, axis=0)

    def stage_out(g):
        rows = slice(g * gw, (g + 1) * gw)
        out_t = jnp.dot(vt_ref[:, :(g + 1) * gw], weights.pop(g), preferred_element_type=F32)
        gate = g_ref[0, rows, :].astype(F32)
        o_ref[0, rows, :] = (out_t.T * (gate * _sigmoid(gate))).astype(BF16)

    return stage_logits, stage_split, stage_sums, stage_weights, stage_out


def _df_stages(lam_ref, sg_ref, q_ref, k_ref, v_ref, g_ref, o_ref, vt_ref, qs_ref, *, gw, lambda_init):
    seq = k_ref.shape[1]
    _fill_vt(v_ref, vt_ref, gw)

    q = q_ref[0]
    lane = lax.broadcasted_iota(jnp.int32, (seq, HEAD_W), 1)
    qs_ref[0] = jnp.where(lane < DF_QK_DIM, q, jnp.zeros_like(q))
    qs_ref[1] = jnp.where(lane >= DF_QK_DIM, q, jnp.zeros_like(q))

    lam_p = lam_ref[...]
    lam = (jnp.exp(jnp.sum(lam_p[0:1] * lam_p[1:2], axis=-1, keepdims=True))
           - jnp.exp(jnp.sum(lam_p[2:3] * lam_p[3:4], axis=-1, keepdims=True))
           + lambda_init)
    causal = lax.broadcasted_iota(jnp.int32, (gw, gw), 0) <= lax.broadcasted_iota(jnp.int32, (gw, gw), 1)
    scores, probs = {}, {}

    def stage_scores(g):
        keys = (g + 1) * gw
        for c in range(2):
            s = lax.dot_general(k_ref[0, :keys, :], qs_ref[c, g * gw:(g + 1) * gw, :], _NT,
                                preferred_element_type=F32)
            s_diag = jnp.where(causal, s[keys - gw:], -jnp.inf)
            s = s_diag if g == 0 else jnp.concatenate([s[:keys - gw], s_diag], axis=0)
            scores[g, c] = (s, jnp.max(s, axis=0, keepdims=True))

    def stage_probs(g):
        for c in range(2):
            s, m = scores.pop((g, c))
            probs[g, c] = jnp.exp2(s - m).astype(BF16)

    def stage_out(g):
        keys = (g + 1) * gw
        rows = slice(g * gw, (g + 1) * gw)
        maps = []
        for c in range(2):
            pv = jnp.dot(vt_ref[:, :keys], probs.pop((g, c)), preferred_element_type=F32)
            maps.append(pv[:HEAD_W] * (1.0 / pv[HEAD_W:HEAD_W + 1]))
        o = maps[0] - lam * maps[1]
        ms = jnp.mean(o * o, axis=0, keepdims=True)
        on = (o * lax.rsqrt(ms + SUBLN_EPS)).T
        gate = g_ref[0, rows, :].astype(F32)
        o_ref[0, rows, :] = (((on * sg_ref[...]) * (1.0 - lambda_init)) * (gate * _sigmoid(gate))).astype(BF16)

    return stage_scores, stage_probs, stage_out


def _attn_kernel(lam_ref, sg_ref, cos_ref, sin_ref, hn_ref, *rest, gw, cb, lambda_init):
    nsec = len(rest) - 6
    w_refs = rest[:nsec]
    sb_o, df_o, proj_ref, sb_vt, df_vt, qs_ref = rest[nsec:]

    @pl.when(pl.program_id(0) == 0)
    def _():
        proj_ref[...] = jnp.zeros_like(proj_ref)

    section = lambda sec: proj_ref.at[:, :, pl.ds(sec * HEAD_W, HEAD_W)]
    sb = _sb_stages(section(SEC_SB_Q), section(SEC_SB_K), section(SEC_SB_V), section(SEC_SB_G), sb_o, sb_vt,
                    gw=gw, cb=cb)
    df = _df_stages(lam_ref, sg_ref, section(SEC_DF_Q), section(SEC_DF_K), section(SEC_DF_V), section(SEC_DF_G),
                    df_o, df_vt, qs_ref, gw=gw, lambda_init=lambda_init)
    schedule = ((0, sb[0]), (0, df[0]), (1, sb[1]), (1, df[1]), (2, sb[2]), (2, df[2]), (3, sb[3]), (4, sb[4]))
    ngroups = proj_ref.shape[1] // gw
    npieces = PROJ_ROW_SPLIT * (proj_ref.shape[2] // (2 * HEAD_W))
    assert npieces <= ngroups
    next_pair = []
    for step in range(ngroups + max(lag for lag, _ in schedule)):
        for lag, stage in schedule:
            if 0 <= step - lag < ngroups:
                stage(step - lag)
        if step < npieces:
            next_pair.append(_proj_piece(hn_ref, w_refs, cos_ref, sin_ref, step))
    for r0, nrows, c0, block in next_pair:
        proj_ref[0, r0:r0 + nrows, c0:c0 + 2 * HEAD_W] = block


def _attention(hn3d, w, lam_params, subln_g, lambda_init):
    b, s, d = hn3d.shape
    nheads = SEC_W // HEAD_W
    nsec = w.shape[1] // SEC_W
    head_cols = nsec * HEAD_W
    assert SB_HEADS == DF_HEADS == nheads and nsec == 8 and s % ATTN_GW == 0
    npairs = b * nheads
    cos, sin = _rope_tables(s)
    subln_g = subln_g.reshape(1, HEAD_W)
    whole = lambda a: pl.BlockSpec(a.shape, lambda t: (0,) * a.ndim, pipeline_mode=pl.Buffered(1))
    attended = lambda t: jnp.maximum(t - 1, 0)
    projected = lambda t: jnp.minimum(t, npairs - 1)
    out_block = pl.BlockSpec((1, s, HEAD_W), lambda t: (attended(t) // nheads, 0, attended(t) % nheads))
    out_shape = jax.ShapeDtypeStruct((b, s, nheads * HEAD_W), BF16)
    kern = functools.partial(_attn_kernel, gw=ATTN_GW, cb=SB_CUMSUM_ROWS, lambda_init=lambda_init)
    return pl.pallas_call(
        kern,
        grid=(npairs + 1,),
        in_specs=[
            whole(lam_params), whole(subln_g), whole(cos), whole(sin),
            pl.BlockSpec((1, s, d), lambda t: (projected(t) // nheads, 0, 0), pipeline_mode=pl.Buffered(1)),
        ] + [pl.BlockSpec((d, HEAD_W), lambda t, sec=sec: (0, sec * nheads + projected(t) % nheads))
             for sec in range(nsec)],
        out_specs=[out_block, out_block],
        out_shape=[out_shape, out_shape],
        scratch_shapes=[pltpu.VMEM((1, s, head_cols), BF16),
                        pltpu.VMEM((HEAD_W, s), BF16),
                        pltpu.VMEM((HEAD_W + BF16_SUBLANES, s), BF16),
                        pltpu.VMEM((2, s, HEAD_W), BF16)],
        compiler_params=pltpu.CompilerParams(dimension_semantics=("arbitrary",), vmem_limit_bytes=VMEM_LIMIT),
        name="attention",
    )(lam_params, subln_g, cos, sin, hn3d, *([w] * nsec))


def _tail_kernel(x_ref, msb_ref, mdf_ref, p_ref, wo_sb_ref, wo_df_ref, gp_ref, wg_ref, wp_ref, gf_ref, o_ref,
                 *, final_norm):
    d = x_ref.shape[1]
    nchunks = d // TAIL_CHUNK
    cols = lambda n: slice(n * TAIL_CHUNK, (n + 1) * TAIL_CHUNK)

    def pipelined(matmul, epilogue):
        nxt = matmul(0)
        for n in range(nchunks):
            acc, nxt = nxt, (matmul(n + 1) if n + 1 < nchunks else None)
            epilogue(acc, n)

    def rms_scale(sum_sq):
        return lax.rsqrt(sum_sq * (1.0 / d) + NORM_EPS)

    h, sum_sq = [None] * nchunks, []

    def out_proj(n):
        return (jnp.dot(msb_ref[...], wo_sb_ref[:, cols(n)], preferred_element_type=F32)
                + jnp.dot(mdf_ref[...], wo_df_ref[:, cols(n)], preferred_element_type=F32))

    def residual(acc, n):
        h[n] = x_ref[:, cols(n)] + acc
        sum_sq.append(jnp.sum(h[n] * h[n], axis=-1, keepdims=True))

    pipelined(out_proj, residual)
    scale = rms_scale(sum(sum_sq))
    hn = jnp.concatenate([((h[n] * scale) * gp_ref[:, cols(n)]).astype(BF16) for n in range(nchunks)], axis=1)

    p_bf16 = p_ref[...].astype(BF16)
    sum_sq = []

    def gate_proj(n):
        return (jnp.dot(hn, wg_ref[:, cols(n)], preferred_element_type=F32),
                jnp.dot(p_bf16, wp_ref[:, cols(n)], preferred_element_type=F32))

    def gated_add(acc, n):
        gate, pp = acc
        h[n] = h[n] + _sigmoid(gate) * pp
        sum_sq.append(jnp.sum(h[n] * h[n], axis=-1, keepdims=True))

    pipelined(gate_proj, gated_add)
    if final_norm:
        scale = rms_scale(sum(sum_sq))
        for n in range(nchunks):
            o_ref[:, cols(n)] = (h[n] * scale) * gf_ref[:, cols(n)]
    else:
        for n in range(nchunks):
            o_ref[:, cols(n)] = h[n]


def _tail(x2d, msb, mdf, p2d, w_out, g_ple, w_gate, w_ple, g_final, final_norm):
    t, d = x2d.shape
    tm = TAIL_TM
    assert t % tm == 0 and d % TAIL_CHUNK == 0
    dsb, ddf, dp = msb.shape[1], mdf.shape[1], p2d.shape[1]
    wo_sb, wo_df = w_out[:dsb], w_out[dsb:]
    rows = lambda width: pl.BlockSpec((tm, width), lambda i: (i, 0))
    whole = lambda a: pl.BlockSpec(a.shape, lambda i: (0, 0), pipeline_mode=pl.Buffered(1))
    g_ple, g_final = g_ple.reshape(1, d), g_final.reshape(1, d)
    return pl.pallas_call(
        functools.partial(_tail_kernel, final_norm=final_norm),
        grid=(t // tm,),
        in_specs=[rows(d), rows(dsb), rows(ddf), rows(dp),
                  whole(wo_sb), whole(wo_df), whole(g_ple), whole(w_gate), whole(w_ple), whole(g_final)],
        out_specs=rows(d),
        out_shape=jax.ShapeDtypeStruct((t, d), F32),
        compiler_params=pltpu.CompilerParams(
            dimension_semantics=("arbitrary",), vmem_limit_bytes=VMEM_LIMIT),
        name="tail",
    )(x2d, msb, mdf, p2d, wo_sb, wo_df, g_ple, w_gate, w_ple, g_final)


def kernel(x, p, norm_mix_g, w_in, lambda_q1, lambda_k1, lambda_q2, lambda_k2, subln_g, w_out, norm_ple_g,
           w_ple_gate, w_ple_proj, norm_final_g):
    b, s, d = x.shape
    depth = w_in.shape[0]
    h = x.reshape(b * s, d)
    for i in range(depth):
        lambda_init = 0.8 - 0.6 * math.exp(-0.3 * i)
        hn = _norm(h, norm_mix_g[i]).reshape(b, s, d)
        lam_params = jnp.stack([lambda_q1[i], lambda_k1[i], lambda_q2[i], lambda_k2[i]])
        msb, mdf = _attention(hn, w_in[i].astype(BF16), lam_params, subln_g[i], lambda_init)
        msb, mdf = msb.reshape(b * s, -1), mdf.reshape(b * s, -1)
        h = _tail(h, msb, mdf, p[i].reshape(b * s, -1), w_out[i].astype(BF16), norm_ple_g[i],
                  w_ple_gate[i].astype(BF16), w_ple_proj[i].astype(BF16), norm_final_g,
                  final_norm=(i == depth - 1))
    return h.reshape(b, s, d)
```

```python
import functools
import math

import jax
import jax.numpy as jnp
from jax import lax
from jax.experimental import pallas as pl
from jax.experimental.pallas import tpu as pltpu

F32 = jnp.float32
BF16 = jnp.bfloat16

SB_HEADS = 8
DF_HEADS = 8
HEAD_W = 128
BF16_SUBLANES = 16
DF_QK_DIM = 64
ROPE_DIM = DF_QK_DIM // 4
ROPE_THETA = 500000.0
NORM_EPS = 1e-6
SUBLN_EPS = 1e-5

SEC_W = 1024
SEC_SB_Q, SEC_SB_K, SEC_SB_V, SEC_SB_G, SEC_DF_Q, SEC_DF_K, SEC_DF_V, SEC_DF_G = range(8)

NORM_TM = 1024
PROJ_ROW_SPLIT = 2
ATTN_GW = 256
SB_CUMSUM_ROWS = 128
TAIL_TM = 512
TAIL_CHUNK = 256
VMEM_LIMIT = 56 * 1024 * 1024
ATTN_VMEM_LIMIT = 60 * 1024 * 1024

_NT = (((1,), (1,)), ((), ()))
_LOG2E = math.log2(math.e)
SB_Q_SCALE = HEAD_W ** -0.5 * _LOG2E
DF_Q_SCALE = DF_QK_DIM ** -0.5 * _LOG2E


def _sigmoid(x):
    return 1.0 / (1.0 + jnp.exp(-x))


def _norm_kernel(x_ref, g_ref, o_ref):
    x = x_ref[...]
    ms = jnp.mean(x * x, axis=-1, keepdims=True)
    o_ref[...] = ((x * lax.rsqrt(ms + NORM_EPS)) * g_ref[...]).astype(BF16)


def _norm(x2d, g):
    t, d = x2d.shape
    assert t % NORM_TM == 0
    return pl.pallas_call(
        _norm_kernel,
        grid=(t // NORM_TM,),
        in_specs=[pl.BlockSpec((NORM_TM, d), lambda i: (i, 0)), pl.BlockSpec((1, d), lambda i: (0, 0))],
        out_specs=pl.BlockSpec((NORM_TM, d), lambda i: (i, 0)),
        out_shape=jax.ShapeDtypeStruct((t, d), BF16),
        compiler_params=pltpu.CompilerParams(dimension_semantics=("arbitrary",), vmem_limit_bytes=VMEM_LIMIT),
        name="norm",
    )(x2d, g.reshape(1, d))


def _rope_tables(seq):
    half = ROPE_DIM // 2
    inv_freq = ROPE_THETA ** (-jnp.arange(0, ROPE_DIM, 2, dtype=F32) / ROPE_DIM)
    ang = jnp.arange(seq, dtype=F32)[:, None] * inv_freq[None, :]
    cos, sin = jnp.cos(ang), jnp.sin(ang)
    lane = jnp.arange(HEAD_W)
    within = lane % DF_QK_DIM
    freq = within % half
    first = within < half
    second = jnp.logical_and(within >= half, within < ROPE_DIM)
    cos_l, sin_l = cos[:, freq], sin[:, freq]
    c = jnp.where(jnp.logical_or(first, second)[None, :], cos_l, 1.0)
    sn = jnp.where(first[None, :], sin_l, jnp.where(second[None, :], -sin_l, 0.0))
    return c.astype(F32), sn.astype(F32)


def _proj_piece(hn_ref, w_refs, cos_ref, sin_ref, piece):
    nrows = hn_ref.shape[1] // PROJ_ROW_SPLIT
    r0 = (piece % PROJ_ROW_SPLIT) * nrows
    pair = piece // PROJ_ROW_SPLIT
    c0 = pair * 2 * HEAD_W
    w_pair = jnp.concatenate([w_refs[2 * pair][...], w_refs[2 * pair + 1][...]], axis=1)
    acc = jnp.dot(hn_ref[0, r0:r0 + nrows, :], w_pair, preferred_element_type=F32)
    blocks = []
    for k in range(2):
        sec = 2 * pair + k
        xh = acc[:, k * HEAD_W:(k + 1) * HEAD_W]
        if sec in (SEC_DF_Q, SEC_DF_K):
            c, s = cos_ref[r0:r0 + nrows, :], sin_ref[r0:r0 + nrows, :]
            if sec == SEC_DF_Q:
                c, s = c * DF_Q_SCALE, s * DF_Q_SCALE
            lane = lax.broadcasted_iota(jnp.int32, c.shape, 1)
            partner = jnp.where(lane % DF_QK_DIM < ROPE_DIM, lane ^ (ROPE_DIM // 2), lane)
            xh = xh * c + jnp.take_along_axis(xh * s, partner, axis=1)
        elif sec == SEC_SB_Q:
            xh = xh * SB_Q_SCALE
        blocks.append(xh.astype(BF16))
    return r0, nrows, c0, jnp.concatenate(blocks, axis=1)


def _fill_vt(v_ref, vt_ref, chunk):
    r = lax.broadcasted_iota(jnp.int32, (HEAD_W, HEAD_W), 0)
    c = lax.broadcasted_iota(jnp.int32, (HEAD_W, HEAD_W), 1)
    eye = jnp.where(r == c, 1.0, 0.0).astype(BF16)
    for lo in range(0, vt_ref.shape[1], chunk):
        vb = v_ref[0, lo:lo + chunk, :]
        vt_ref[:HEAD_W, lo:lo + chunk] = lax.dot_general(eye, vb, _NT, preferred_element_type=F32).astype(BF16)
    if vt_ref.shape[0] > HEAD_W:
        vt_ref[HEAD_W:, :] = jnp.ones((vt_ref.shape[0] - HEAD_W, vt_ref.shape[1]), BF16)


def _sb_stages(q_ref, k_ref, v_ref, g_ref, o_ref, vt_ref, *, gw, cb):
    _fill_vt(v_ref, vt_ref, gw)

    row = lax.broadcasted_iota(jnp.int32, (cb, 2 * cb), 0)
    col = lax.broadcasted_iota(jnp.int32, (cb, 2 * cb), 1) % cb
    suffix = jnp.where(col >= row, 1.0, 0.0).astype(BF16)
    key_pos = lax.broadcasted_iota(jnp.int32, (cb, gw), 0)
    qry_pos = lax.broadcasted_iota(jnp.int32, (cb, gw), 1)
    strict = [key_pos + r * cb < qry_pos for r in range(gw // cb)]

    per_group = gw // cb
    logits, splits, sums, weights = {}, {}, {}, {}

    def diag_mask(g, b):
        return strict[b - g * per_group] if b >= g * per_group else None

    def stage_logits(g):
        logits[g] = lax.dot_general(k_ref[0, :(g + 1) * gw, :], q_ref[0, g * gw:(g + 1) * gw, :], _NT,
                                    preferred_element_type=F32)

    def stage_split(g):
        y = logits[g]
        sp = jnp.maximum(y, 0.0) + jnp.log2(1.0 + jnp.exp2(-jnp.abs(y)))
        parts = []
        for b in range((g + 1) * per_group):
            sp_b = sp[b * cb:(b + 1) * cb]
            if diag_mask(g, b) is not None:
                sp_b = jnp.where(diag_mask(g, b), sp_b, 0.0)
            hi = sp_b.astype(BF16)
            lo = (sp_b - hi.astype(F32)).astype(BF16)
            parts.append(jnp.concatenate([hi, lo], axis=0))
        splits[g] = parts

    def stage_sums(g):
        sums[g] = [jnp.dot(suffix, part, preferred_element_type=F32) for part in splits.pop(g)]

    def stage_weights(g):
        y, block_sums = logits.pop(g), sums.pop(g)
        nblocks = (g + 1) * per_group
        w = [None] * nblocks
        carry = None
        for b in reversed(range(nblocks)):
            y_b = y[b * cb:(b + 1) * cb]
            w_b = jnp.exp2((y_b if carry is None else y_b - carry) - block_sums[b])
            if diag_mask(g, b) is not None:
                w_b = jnp.where(diag_mask(g, b), w_b, 0.0)
            w[b] = w_b.astype(BF16)
            total = block_sums[b][0:1, :]
            carry = total if carry is None else carry + total
        weights[g] = jnp.concatenate(w, axis=0)

    def stage_out(g):
        rows = slice(g * gw, (g + 1) * gw)
        out_t = jnp.dot(vt_ref[:, :(g + 1) * gw], weights.pop(g), preferred_element_type=F32)
        gate = g_ref[0, rows, :].astype(F32)
        o_ref[0, rows, :] = (out_t.T * (gate * _sigmoid(gate))).astype(BF16)

    return stage_logits, stage_split, stage_sums, stage_weights, stage_out


def _df_stages(lam_ref, sg_ref, q_ref, k_ref, v_ref, g_ref, o_ref, vt_ref, qs_ref, *, gw, lambda_init):
    seq = k_ref.shape[1]
    _fill_vt(v_ref, vt_ref, gw)

    q = q_ref[0]
    lane = lax.broadcasted_iota(jnp.int32, (seq, HEAD_W), 1)
    qs_ref[0] = jnp.where(lane < DF_QK_DIM, q, jnp.zeros_like(q))
    qs_ref[1] = jnp.where(lane >= DF_QK_DIM, q, jnp.zeros_like(q))

    lam_p = lam_ref[...]
    lam = (jnp.exp(jnp.sum(lam_p[0:1] * lam_p[1:2], axis=-1, keepdims=True))
           - jnp.exp(jnp.sum(lam_p[2:3] * lam_p[3:4], axis=-1, keepdims=True))
           + lambda_init)
    causal = lax.broadcasted_iota(jnp.int32, (gw, gw), 0) <= lax.broadcasted_iota(jnp.int32, (gw, gw), 1)
    scores, probs = {}, {}

    def stage_scores(g):
        keys = (g + 1) * gw
        for c in range(2):
            s = lax.dot_general(k_ref[0, :keys, :], qs_ref[c, g * gw:(g + 1) * gw, :], _NT,
                                preferred_element_type=F32)
            s_diag = jnp.where(causal, s[keys - gw:], -jnp.inf)
            s = s_diag if g == 0 else jnp.concatenate([s[:keys - gw], s_diag], axis=0)
            scores[g, c] = (s, jnp.max(s, axis=0, keepdims=True))

    def stage_probs(g):
        for c in range(2):
            s, m = scores.pop((g, c))
            probs[g, c] = jnp.exp2(s - m).astype(BF16)

    def stage_out(g):
        keys = (g + 1) * gw
        rows = slice(g * gw, (g + 1) * gw)
        maps = []
        for c in range(2):
            pv = jnp.dot(vt_ref[:, :keys], probs.pop((g, c)), preferred_element_type=F32)
            maps.append(pv[:HEAD_W] * (1.0 / pv[HEAD_W:HEAD_W + 1]))
        o = maps[0] - lam * maps[1]
        ms = jnp.mean(o * o, axis=0, keepdims=True)
        on = (o * lax.rsqrt(ms + SUBLN_EPS)).T
        gate = g_ref[0, rows, :].astype(F32)
        o_ref[0, rows, :] = (((on * sg_ref[...]) * (1.0 - lambda_init)) * (gate * _sigmoid(gate))).astype(BF16)

    return stage_scores, stage_probs, stage_out


def _attn_kernel(lam_ref, sg_ref, cos_ref, sin_ref, hn_ref, *rest, gw, cb, lambda_init):
    nsec = len(rest) - 6
    w_refs = rest[:nsec]
    sb_o, df_o, proj_ref, sb_vt, df_vt, qs_ref = rest[nsec:]

    @pl.when(pl.program_id(0) == 0)
    def _():
        proj_ref[...] = jnp.zeros_like(proj_ref)

    section = lambda sec: proj_ref.at[:, :, pl.ds(sec * HEAD_W, HEAD_W)]
    sb = _sb_stages(section(SEC_SB_Q), section(SEC_SB_K), section(SEC_SB_V), section(SEC_SB_G), sb_o, sb_vt,
                    gw=gw, cb=cb)
    df = _df_stages(lam_ref, sg_ref, section(SEC_DF_Q), section(SEC_DF_K), section(SEC_DF_V), section(SEC_DF_G),
                    df_o, df_vt, qs_ref, gw=gw, lambda_init=lambda_init)
    schedule = ((0, sb[0]), (0, df[0]), (1, sb[1]), (1, df[1]), (2, sb[2]), (2, df[2]), (3, sb[3]), (4, sb[4]))
    ngroups = proj_ref.shape[1] // gw
    npieces = PROJ_ROW_SPLIT * (proj_ref.shape[2] // (2 * HEAD_W))
    assert npieces <= ngroups
    next_pair = []
    for step in range(ngroups + max(lag for lag, _ in schedule)):
        for lag, stage in schedule:
            if 0 <= step - lag < ngroups:
                stage(step - lag)
        if step < npieces:
            next_pair.append(_proj_piece(hn_ref, w_refs, cos_ref, sin_ref, step))
    for r0, nrows, c0, block in next_pair:
        proj_ref[0, r0:r0 + nrows, c0:c0 + 2 * HEAD_W] = block


def _attention(hn3d, w, lam_params, subln_g, lambda_init):
    b, s, d = hn3d.shape
    nheads = SEC_W // HEAD_W
    nsec = w.shape[1] // SEC_W
    head_cols = nsec * HEAD_W
    assert SB_HEADS == DF_HEADS == nheads and nsec == 8 and s % ATTN_GW == 0
    npairs = b * nheads
    cos, sin = _rope_tables(s)
    subln_g = subln_g.reshape(1, HEAD_W)
    whole = lambda a: pl.BlockSpec(a.shape, lambda t: (0,) * a.ndim, pipeline_mode=pl.Buffered(1))
    attended = lambda t: jnp.maximum(t - 1, 0)
    projected = lambda t: jnp.minimum(t, npairs - 1)
    out_block = pl.BlockSpec((1, s, HEAD_W), lambda t: (attended(t) // nheads, 0, attended(t) % nheads))
    out_shape = jax.ShapeDtypeStruct((b, s, nheads * HEAD_W), BF16)
    kern = functools.partial(_attn_kernel, gw=ATTN_GW, cb=SB_CUMSUM_ROWS, lambda_init=lambda_init)
    return pl.pallas_call(
        kern,
        grid=(npairs + 1,),
        in_specs=[
            whole(lam_params), whole(subln_g), whole(cos), whole(sin),
            pl.BlockSpec((1, s, d), lambda t: (projected(t) // nheads, 0, 0)),
        ] + [pl.BlockSpec((d, HEAD_W), lambda t, sec=sec: (0, sec * nheads + projected(t) % nheads))
             for sec in range(nsec)],
        out_specs=[out_block, out_block],
        out_shape=[out_shape, out_shape],
        scratch_shapes=[pltpu.VMEM((1, s, head_cols), BF16),
                        pltpu.VMEM((HEAD_W, s), BF16),
                        pltpu.VMEM((HEAD_W + BF16_SUBLANES, s), BF16),
                        pltpu.VMEM((2, s, HEAD_W), BF16)],
        compiler_params=pltpu.CompilerParams(dimension_semantics=("arbitrary",), vmem_limit_bytes=ATTN_VMEM_LIMIT),
        name="attention",
    )(lam_params, subln_g, cos, sin, hn3d, *([w] * nsec))


def _tail_kernel(x_ref, msb_ref, mdf_ref, p_ref, wo_sb_ref, wo_df_ref, gp_ref, wg_ref, wp_ref, gf_ref, o_ref,
                 *, final_norm):
    d = x_ref.shape[1]
    nchunks = d // TAIL_CHUNK
    cols = lambda n: slice(n * TAIL_CHUNK, (n + 1) * TAIL_CHUNK)

    def pipelined(matmul, epilogue):
        nxt = matmul(0)
        for n in range(nchunks):
            acc, nxt = nxt, (matmul(n + 1) if n + 1 < nchunks else None)
            epilogue(acc, n)

    def rms_scale(sum_sq):
        return lax.rsqrt(sum_sq * (1.0 / d) + NORM_EPS)

    h, sum_sq = [None] * nchunks, []

    def out_proj(n):
        return (jnp.dot(msb_ref[...], wo_sb_ref[:, cols(n)], preferred_element_type=F32)
                + jnp.dot(mdf_ref[...], wo_df_ref[:, cols(n)], preferred_element_type=F32))

    def residual(acc, n):
        h[n] = x_ref[:, cols(n)] + acc
        sum_sq.append(jnp.sum(h[n] * h[n], axis=-1, keepdims=True))

    pipelined(out_proj, residual)
    scale = rms_scale(sum(sum_sq))
    hn = jnp.concatenate([((h[n] * scale) * gp_ref[:, cols(n)]).astype(BF16) for n in range(nchunks)], axis=1)

    p_bf16 = p_ref[...].astype(BF16)
    sum_sq = []

    def gate_proj(n):
        return (jnp.dot(hn, wg_ref[:, cols(n)], preferred_element_type=F32),
                jnp.dot(p_bf16, wp_ref[:, cols(n)], preferred_element_type=F32))

    def gated_add(acc, n):
        gate, pp = acc
        h[n] = h[n] + _sigmoid(gate) * pp
        sum_sq.append(jnp.sum(h[n] * h[n], axis=-1, keepdims=True))

    pipelined(gate_proj, gated_add)
    if final_norm:
        scale = rms_scale(sum(sum_sq))
        for n in range(nchunks):
            o_ref[:, cols(n)] = (h[n] * scale) * gf_ref[:, cols(n)]
    else:
        for n in range(nchunks):
            o_ref[:, cols(n)] = h[n]


def _tail(x2d, msb, mdf, p2d, w_out, g_ple, w_gate, w_ple, g_final, final_norm):
    t, d = x2d.shape
    tm = TAIL_TM
    assert t % tm == 0 and d % TAIL_CHUNK == 0
    dsb, ddf, dp = msb.shape[1], mdf.shape[1], p2d.shape[1]
    wo_sb, wo_df = w_out[:dsb], w_out[dsb:]
    rows = lambda width: pl.BlockSpec((tm, width), lambda i: (i, 0))
    whole = lambda a: pl.BlockSpec(a.shape, lambda i: (0, 0), pipeline_mode=pl.Buffered(1))
    g_ple, g_final = g_ple.reshape(1, d), g_final.reshape(1, d)
    return pl.pallas_call(
        functools.partial(_tail_kernel, final_norm=final_norm),
        grid=(t // tm,),
        in_specs=[rows(d), rows(dsb), rows(ddf), rows(dp),
                  whole(wo_sb), whole(wo_df), whole(g_ple), whole(w_gate), whole(w_ple), whole(g_final)],
        out_specs=rows(d),
        out_shape=jax.ShapeDtypeStruct((t, d), F32),
        compiler_params=pltpu.CompilerParams(
            dimension_semantics=("arbitrary",), vmem_limit_bytes=VMEM_LIMIT),
        name="tail",
    )(x2d, msb, mdf, p2d, wo_sb, wo_df, g_ple, w_gate, w_ple, g_final)


def kernel(x, p, norm_mix_g, w_in, lambda_q1, lambda_k1, lambda_q2, lambda_k2, subln_g, w_out, norm_ple_g,
           w_ple_gate, w_ple_proj, norm_final_g):
    b, s, d = x.shape
    depth = w_in.shape[0]
    h = x.reshape(b * s, d)
    for i in range(depth):
        lambda_init = 0.8 - 0.6 * math.exp(-0.3 * i)
        hn = _norm(h, norm_mix_g[i]).reshape(b, s, d)
        lam_params = jnp.stack([lambda_q1[i], lambda_k1[i], lambda_q2[i], lambda_k2[i]])
        msb, mdf = _attention(hn, w_in[i].astype(BF16), lam_params, subln_g[i], lambda_init)
        msb, mdf = msb.reshape(b * s, -1), mdf.reshape(b * s, -1)
        h = _tail(h, msb, mdf, p[i].reshape(b * s, -1), w_out[i].astype(BF16), norm_ple_g[i],
                  w_ple_gate[i].astype(BF16), w_ple_proj[i].astype(BF16), norm_final_g,
                  final_norm=(i == depth - 1))
    return h.reshape(b, s, d)
```

```python
import functools
import math

import jax
import jax.numpy as jnp
from jax import lax
from jax.experimental import pallas as pl
from jax.experimental.pallas import tpu as pltpu

F32 = jnp.float32
BF16 = jnp.bfloat16

SB_HEADS = 8
DF_HEADS = 8
HEAD_W = 128
DF_QK_DIM = 64
ROPE_DIM = DF_QK_DIM // 4
ROPE_THETA = 500000.0
NORM_EPS = 1e-6
SUBLN_EPS = 1e-5

SEC_W = 1024
SEC_SB_Q, SEC_SB_K, SEC_SB_V, SEC_SB_G, SEC_DF_Q, SEC_DF_K, SEC_DF_V, SEC_DF_G = range(8)

NORM_TM = 1024
PROJ_ROW_SPLIT = 2
ATTN_GW = 256
SB_CUMSUM_ROWS = 128
TAIL_TM = 512
TAIL_CHUNK = 256
VMEM_LIMIT = 56 * 1024 * 1024
ATTN_VMEM_LIMIT = 60 * 1024 * 1024

_NT = (((1,), (1,)), ((), ()))
_LOG2E = math.log2(math.e)
SB_Q_SCALE = HEAD_W ** -0.5 * _LOG2E
DF_Q_SCALE = DF_QK_DIM ** -0.5 * _LOG2E


def _sigmoid(x):
    return 1.0 / (1.0 + jnp.exp(-x))


def _norm_kernel(x_ref, g_ref, o_ref):
    x = x_ref[...]
    ms = jnp.mean(x * x, axis=-1, keepdims=True)
    o_ref[...] = ((x * lax.rsqrt(ms + NORM_EPS)) * g_ref[...]).astype(BF16)


def _norm(x2d, g):
    t, d = x2d.shape
    assert t % NORM_TM == 0
    return pl.pallas_call(
        _norm_kernel,
        grid=(t // NORM_TM,),
        in_specs=[pl.BlockSpec((NORM_TM, d), lambda i: (i, 0)), pl.BlockSpec((1, d), lambda i: (0, 0))],
        out_specs=pl.BlockSpec((NORM_TM, d), lambda i: (i, 0)),
        out_shape=jax.ShapeDtypeStruct((t, d), BF16),
        compiler_params=pltpu.CompilerParams(dimension_semantics=("arbitrary",), vmem_limit_bytes=VMEM_LIMIT),
        name="norm",
    )(x2d, g.reshape(1, d))


def _rope_tables(seq):
    half = ROPE_DIM // 2
    inv_freq = ROPE_THETA ** (-jnp.arange(0, ROPE_DIM, 2, dtype=F32) / ROPE_DIM)
    ang = jnp.arange(seq, dtype=F32)[:, None] * inv_freq[None, :]
    cos, sin = jnp.cos(ang), jnp.sin(ang)
    lane = jnp.arange(HEAD_W)
    within = lane % DF_QK_DIM
    freq = within % half
    first = within < half
    second = jnp.logical_and(within >= half, within < ROPE_DIM)
    cos_l, sin_l = cos[:, freq], sin[:, freq]
    c = jnp.where(jnp.logical_or(first, second)[None, :], cos_l, 1.0)
    sn = jnp.where(first[None, :], sin_l, jnp.where(second[None, :], -sin_l, 0.0))
    return c.astype(F32), sn.astype(F32)


def _proj_piece(hn_ref, w_refs, cos_ref, sin_ref, piece):
    nrows = hn_ref.shape[1] // PROJ_ROW_SPLIT
    r0 = (piece % PROJ_ROW_SPLIT) * nrows
    pair = piece // PROJ_ROW_SPLIT
    c0 = pair * 2 * HEAD_W
    w_pair = jnp.concatenate([w_refs[2 * pair][...], w_refs[2 * pair + 1][...]], axis=1)
    acc = jnp.dot(hn_ref[0, r0:r0 + nrows, :], w_pair, preferred_element_type=F32)
    blocks = []
    for k in range(2):
        sec = 2 * pair + k
        xh = acc[:, k * HEAD_W:(k + 1) * HEAD_W]
        if sec in (SEC_DF_Q, SEC_DF_K):
            c, s = cos_ref[r0:r0 + nrows, :], sin_ref[r0:r0 + nrows, :]
            if sec == SEC_DF_Q:
                c, s = c * DF_Q_SCALE, s * DF_Q_SCALE
            lane = lax.broadcasted_iota(jnp.int32, c.shape, 1)
            partner = jnp.where(lane % DF_QK_DIM < ROPE_DIM, lane ^ (ROPE_DIM // 2), lane)
            xh = xh * c + jnp.take_along_axis(xh * s, partner, axis=1)
        elif sec == SEC_SB_Q:
            xh = xh * SB_Q_SCALE
        blocks.append(xh.astype(BF16))
    return r0, nrows, c0, jnp.concatenate(blocks, axis=1)


def _fill_vt(v_ref, vt_ref, chunk):
    for lo in range(0, vt_ref.shape[1], chunk):
        vt_ref[:, lo:lo + chunk] = v_ref[0, lo:lo + chunk, :].astype(F32).T.astype(BF16)


def _sb_stages(q_ref, k_ref, v_ref, g_ref, o_ref, vt_ref, *, gw, cb):
    _fill_vt(v_ref, vt_ref, gw)

    row = lax.broadcasted_iota(jnp.int32, (cb, 2 * cb), 0)
    col = lax.broadcasted_iota(jnp.int32, (cb, 2 * cb), 1) % cb
    suffix = jnp.where(col >= row, 1.0, 0.0).astype(BF16)
    key_pos = lax.broadcasted_iota(jnp.int32, (cb, gw), 0)
    qry_pos = lax.broadcasted_iota(jnp.int32, (cb, gw), 1)
    strict = [key_pos + r * cb < qry_pos for r in range(gw // cb)]

    per_group = gw // cb
    logits, splits, sums, weights = {}, {}, {}, {}

    def diag_mask(g, b):
        return strict[b - g * per_group] if b >= g * per_group else None

    def stage_logits(g):
        logits[g] = lax.dot_general(k_ref[0, :(g + 1) * gw, :], q_ref[0, g * gw:(g + 1) * gw, :], _NT,
                                    preferred_element_type=F32)

    def stage_split(g):
        y = logits[g]
        sp = jnp.maximum(y, 0.0) + jnp.log2(1.0 + jnp.exp2(-jnp.abs(y)))
        parts = []
        for b in range((g + 1) * per_group):
            sp_b = sp[b * cb:(b + 1) * cb]
            if diag_mask(g, b) is not None:
                sp_b = jnp.where(diag_mask(g, b), sp_b, 0.0)
            hi = sp_b.astype(BF16)
            lo = (sp_b - hi.astype(F32)).astype(BF16)
            parts.append(jnp.concatenate([hi, lo], axis=0))
        splits[g] = parts

    def stage_sums(g):
        sums[g] = [jnp.dot(suffix, part, preferred_element_type=F32) for part in splits.pop(g)]

    def stage_weights(g):
        y, block_sums = logits.pop(g), sums.pop(g)
        nblocks = (g + 1) * per_group
        w = [None] * nblocks
        carry = None
        for b in reversed(range(nblocks)):
            y_b = y[b * cb:(b + 1) * cb]
            w_b = jnp.exp2((y_b if carry is None else y_b - carry) - block_sums[b])
            if diag_mask(g, b) is not None:
                w_b = jnp.where(diag_mask(g, b), w_b, 0.0)
            w[b] = w_b.astype(BF16)
            total = block_sums[b][0:1, :]
            carry = total if carry is None else carry + total
        weights[g] = jnp.concatenate(w, axis=0)

    def stage_out(g):
        rows = slice(g * gw, (g + 1) * gw)
        out_t = jnp.dot(vt_ref[:, :(g + 1) * gw], weights.pop(g), preferred_element_type=F32)
        gate = g_ref[0, rows, :].astype(F32)
        o_ref[0, rows, :] = (out_t.T * (gate * _sigmoid(gate))).astype(BF16)

    return stage_logits, stage_split, stage_sums, stage_weights, stage_out


def _df_stages(lam_ref, sg_ref, q_ref, k_ref, v_ref, g_ref, o_ref, vt_ref, qs_ref, *, gw, lambda_init):
    seq = k_ref.shape[1]
    _fill_vt(v_ref, vt_ref, gw)

    q = q_ref[0]
    lane = lax.broadcasted_iota(jnp.int32, (seq, HEAD_W), 1)
    qs_ref[0] = jnp.where(lane < DF_QK_DIM, q, jnp.zeros_like(q))
    qs_ref[1] = jnp.where(lane >= DF_QK_DIM, q, jnp.zeros_like(q))

    lam_p = lam_ref[...]
    lam = (jnp.exp(jnp.sum(lam_p[0:1] * lam_p[1:2], axis=-1, keepdims=True))
           - jnp.exp(jnp.sum(lam_p[2:3] * lam_p[3:4], axis=-1, keepdims=True))
           + lambda_init)
    causal = lax.broadcasted_iota(jnp.int32, (gw, gw), 0) <= lax.broadcasted_iota(jnp.int32, (gw, gw), 1)
    scores, probs = {}, {}

    def stage_scores(g):
        keys = (g + 1) * gw
        for c in range(2):
            s = lax.dot_general(k_ref[0, :keys, :], qs_ref[c, g * gw:(g + 1) * gw, :], _NT,
                                preferred_element_type=F32)
            s_diag = jnp.where(causal, s[keys - gw:], -jnp.inf)
            s = s_diag if g == 0 else jnp.concatenate([s[:keys - gw], s_diag], axis=0)
            scores[g, c] = (s, jnp.max(s, axis=0, keepdims=True))

    def stage_probs(g):
        for c in range(2):
            s, m = scores.pop((g, c))
            p = jnp.exp2(s - m)
            probs[g, c] = (p.astype(BF16), jnp.sum(p, axis=0, keepdims=True))

    def stage_out(g):
        keys = (g + 1) * gw
        rows = slice(g * gw, (g + 1) * gw)
        maps = []
        for c in range(2):
            p, denom = probs.pop((g, c))
            pv = jnp.dot(vt_ref[:, :keys], p, preferred_element_type=F32)
            maps.append(pv * (1.0 / denom))
        o = maps[0] - lam * maps[1]
        ms = jnp.mean(o * o, axis=0, keepdims=True)
        on = (o * lax.rsqrt(ms + SUBLN_EPS)).T
        gate = g_ref[0, rows, :].astype(F32)
        o_ref[0, rows, :] = (((on * sg_ref[...]) * (1.0 - lambda_init)) * (gate * _sigmoid(gate))).astype(BF16)

    return stage_scores, stage_probs, stage_out


def _attn_kernel(lam_ref, sg_ref, cos_ref, sin_ref, hn_ref, *rest, gw, cb, lambda_init):
    nsec = len(rest) - 6
    w_refs = rest[:nsec]
    sb_o, df_o, proj_ref, sb_vt, df_vt, qs_ref = rest[nsec:]

    @pl.when(pl.program_id(0) == 0)
    def _():
        proj_ref[...] = jnp.zeros_like(proj_ref)

    section = lambda sec: proj_ref.at[:, :, pl.ds(sec * HEAD_W, HEAD_W)]
    sb = _sb_stages(section(SEC_SB_Q), section(SEC_SB_K), section(SEC_SB_V), section(SEC_SB_G), sb_o, sb_vt,
                    gw=gw, cb=cb)
    df = _df_stages(lam_ref, sg_ref, section(SEC_DF_Q), section(SEC_DF_K), section(SEC_DF_V), section(SEC_DF_G),
                    df_o, df_vt, qs_ref, gw=gw, lambda_init=lambda_init)
    schedule = ((0, sb[0]), (0, df[0]), (1, sb[1]), (1, df[1]), (2, sb[2]), (2, df[2]), (3, sb[3]), (4, sb[4]))
    ngroups = proj_ref.shape[1] // gw
    npieces = PROJ_ROW_SPLIT * (proj_ref.shape[2] // (2 * HEAD_W))
    assert npieces <= ngroups
    next_pair = []
    for step in range(ngroups + max(lag for lag, _ in schedule)):
        for lag, stage in schedule:
            if 0 <= step - lag < ngroups:
                stage(step - lag)
        if step < npieces:
            next_pair.append(_proj_piece(hn_ref, w_refs, cos_ref, sin_ref, step))
    for r0, nrows, c0, block in next_pair:
        proj_ref[0, r0:r0 + nrows, c0:c0 + 2 * HEAD_W] = block


def _attention(hn3d, w, lam_params, subln_g, lambda_init):
    b, s, d = hn3d.shape
    nheads = SEC_W // HEAD_W
    nsec = w.shape[1] // SEC_W
    head_cols = nsec * HEAD_W
    assert SB_HEADS == DF_HEADS == nheads and nsec == 8 and s % ATTN_GW == 0
    npairs = b * nheads
    cos, sin = _rope_tables(s)
    subln_g = subln_g.reshape(1, HEAD_W)
    whole = lambda a: pl.BlockSpec(a.shape, lambda t: (0,) * a.ndim, pipeline_mode=pl.Buffered(1))
    attended = lambda t: jnp.maximum(t - 1, 0)
    projected = lambda t: jnp.minimum(t, npairs - 1)
    out_block = pl.BlockSpec((1, s, HEAD_W), lambda t: (attended(t) // nheads, 0, attended(t) % nheads))
    out_shape = jax.ShapeDtypeStruct((b, s, nheads * HEAD_W), BF16)
    kern = functools.partial(_attn_kernel, gw=ATTN_GW, cb=SB_CUMSUM_ROWS, lambda_init=lambda_init)
    return pl.pallas_call(
        kern,
        grid=(npairs + 1,),
        in_specs=[
            whole(lam_params), whole(subln_g), whole(cos), whole(sin),
            pl.BlockSpec((1, s, d), lambda t: (projected(t) // nheads, 0, 0)),
        ] + [pl.BlockSpec((d, HEAD_W), lambda t, sec=sec: (0, sec * nheads + projected(t) % nheads))
             for sec in range(nsec)],
        out_specs=[out_block, out_block],
        out_shape=[out_shape, out_shape],
        scratch_shapes=[pltpu.VMEM((1, s, head_cols), BF16),
                        pltpu.VMEM((HEAD_W, s), BF16),
                        pltpu.VMEM((HEAD_W, s), BF16),
                        pltpu.VMEM((2, s, HEAD_W), BF16)],
        compiler_params=pltpu.CompilerParams(dimension_semantics=("arbitrary",), vmem_limit_bytes=ATTN_VMEM_LIMIT),
        name="attention",
    )(lam_params, subln_g, cos, sin, hn3d, *([w] * nsec))


def _tail_kernel(x_ref, msb_ref, mdf_ref, p_ref, wo_sb_ref, wo_df_ref, gp_ref, wg_ref, wp_ref, gf_ref, o_ref,
                 *, final_norm):
    d = x_ref.shape[1]
    nchunks = d // TAIL_CHUNK
    cols = lambda n: slice(n * TAIL_CHUNK, (n + 1) * TAIL_CHUNK)

    def pipelined(matmul, epilogue):
        nxt = matmul(0)
        for n in range(nchunks):
            acc, nxt = nxt, (matmul(n + 1) if n + 1 < nchunks else None)
            epilogue(acc, n)

    def rms_scale(sum_sq):
        return lax.rsqrt(sum_sq * (1.0 / d) + NORM_EPS)

    h, sum_sq = [None] * nchunks, []

    def out_proj(n):
        return (jnp.dot(msb_ref[...], wo_sb_ref[:, cols(n)], preferred_element_type=F32)
                + jnp.dot(mdf_ref[...], wo_df_ref[:, cols(n)], preferred_element_type=F32))

    def residual(acc, n):
        h[n] = x_ref[:, cols(n)] + acc
        sum_sq.append(jnp.sum(h[n] * h[n], axis=-1, keepdims=True))

    pipelined(out_proj, residual)
    scale = rms_scale(sum(sum_sq))
    hn = jnp.concatenate([((h[n] * scale) * gp_ref[:, cols(n)]).astype(BF16) for n in range(nchunks)], axis=1)

    p_bf16 = p_ref[...].astype(BF16)
    sum_sq = []

    def gate_proj(n):
        return (jnp.dot(hn, wg_ref[:, cols(n)], preferred_element_type=F32),
                jnp.dot(p_bf16, wp_ref[:, cols(n)], preferred_element_type=F32))

    def gated_add(acc, n):
        gate, pp = acc
        h[n] = h[n] + _sigmoid(gate) * pp
        sum_sq.append(jnp.sum(h[n] * h[n], axis=-1, keepdims=True))

    pipelined(gate_proj, gated_add)
    if final_norm:
        scale = rms_scale(sum(sum_sq))
        for n in range(nchunks):
            o_ref[:, cols(n)] = (h[n] * scale) * gf_ref[:, cols(n)]
    else:
        for n in range(nchunks):
            o_ref[:, cols(n)] = h[n]


def _tail(x2d, msb, mdf, p2d, w_out, g_ple, w_gate, w_ple, g_final, final_norm):
    t, d = x2d.shape
    tm = TAIL_TM
    assert t % tm == 0 and d % TAIL_CHUNK == 0
    dsb, ddf, dp = msb.shape[1], mdf.shape[1], p2d.shape[1]
    wo_sb, wo_df = w_out[:dsb], w_out[dsb:]
    rows = lambda width: pl.BlockSpec((tm, width), lambda i: (i, 0))
    whole = lambda a: pl.BlockSpec(a.shape, lambda i: (0, 0), pipeline_mode=pl.Buffered(1))
    g_ple, g_final = g_ple.reshape(1, d), g_final.reshape(1, d)
    return pl.pallas_call(
        functools.partial(_tail_kernel, final_norm=final_norm),
        grid=(t // tm,),
        in_specs=[rows(d), rows(dsb), rows(ddf), rows(dp),
                  whole(wo_sb), whole(wo_df), whole(g_ple), whole(w_gate), whole(w_ple), whole(g_final)],
        out_specs=rows(d),
        out_shape=jax.ShapeDtypeStruct((t, d), F32),
        compiler_params=pltpu.CompilerParams(
            dimension_semantics=("arbitrary",), vmem_limit_bytes=VMEM_LIMIT),
        name="tail",
    )(x2d, msb, mdf, p2d, wo_sb, wo_df, g_ple, w_gate, w_ple, g_final)


def kernel(x, p, norm_mix_g, w_in, lambda_q1, lambda_k1, lambda_q2, lambda_k2, subln_g, w_out, norm_ple_g,
           w_ple_gate, w_ple_proj, norm_final_g):
    b, s, d = x.shape
    depth = w_in.shape[0]
    h = x.reshape(b * s, d)
    for i in range(depth):
        lambda_init = 0.8 - 0.6 * math.exp(-0.3 * i)
        hn = _norm(h, norm_mix_g[i]).reshape(b, s, d)
        lam_params = jnp.stack([lambda_q1[i], lambda_k1[i], lambda_q2[i], lambda_k2[i]])
        msb, mdf = _attention(hn, w_in[i].astype(BF16), lam_params, subln_g[i], lambda_init)
        msb, mdf = msb.reshape(b * s, -1), mdf.reshape(b * s, -1)
        h = _tail(h, msb, mdf, p[i].reshape(b * s, -1), w_out[i].astype(BF16), norm_ple_g[i],
                  w_ple_gate[i].astype(BF16), w_ple_proj[i].astype(BF16), norm_final_g,
                  final_norm=(i == depth - 1))
    return h.reshape(b, s, d)
```

```python
import functools
import math

import jax
import jax.numpy as jnp
from jax import lax
from jax.experimental import pallas as pl
from jax.experimental.pallas import tpu as pltpu

F32 = jnp.float32
BF16 = jnp.bfloat16

SB_HEADS = 8
DF_HEADS = 8
HEAD_W = 128
BF16_SUBLANES = 16
DF_QK_DIM = 64
ROPE_DIM = DF_QK_DIM // 4
ROPE_THETA = 500000.0
NORM_EPS = 1e-6
SUBLN_EPS = 1e-5

SEC_W = 1024
SEC_SB_Q, SEC_SB_K, SEC_SB_V, SEC_SB_G, SEC_DF_Q, SEC_DF_K, SEC_DF_V, SEC_DF_G = range(8)

NORM_TM = 1024
PROJ_ROW_SPLIT = 2
PROJ_FIRST_STEP = 2
ATTN_GW = 256
SB_CUMSUM_ROWS = 128
TAIL_TM = 512
TAIL_CHUNK = 256
VMEM_LIMIT = 56 * 1024 * 1024
ATTN_VMEM_LIMIT = 60 * 1024 * 1024

_NT = (((1,), (1,)), ((), ()))
_LOG2E = math.log2(math.e)
SB_Q_SCALE = HEAD_W ** -0.5 * _LOG2E
DF_Q_SCALE = DF_QK_DIM ** -0.5 * _LOG2E


def _sigmoid(x):
    return 1.0 / (1.0 + jnp.exp(-x))


def _norm_kernel(x_ref, g_ref, o_ref):
    x = x_ref[...]
    ms = jnp.mean(x * x, axis=-1, keepdims=True)
    o_ref[...] = ((x * lax.rsqrt(ms + NORM_EPS)) * g_ref[...]).astype(BF16)


def _norm(x2d, g):
    t, d = x2d.shape
    assert t % NORM_TM == 0
    return pl.pallas_call(
        _norm_kernel,
        grid=(t // NORM_TM,),
        in_specs=[pl.BlockSpec((NORM_TM, d), lambda i: (i, 0)), pl.BlockSpec((1, d), lambda i: (0, 0))],
        out_specs=pl.BlockSpec((NORM_TM, d), lambda i: (i, 0)),
        out_shape=jax.ShapeDtypeStruct((t, d), BF16),
        compiler_params=pltpu.CompilerParams(dimension_semantics=("arbitrary",), vmem_limit_bytes=VMEM_LIMIT),
        name="norm",
    )(x2d, g.reshape(1, d))


def _rope_tables(seq):
    half = ROPE_DIM // 2
    inv_freq = ROPE_THETA ** (-jnp.arange(0, ROPE_DIM, 2, dtype=F32) / ROPE_DIM)
    ang = jnp.arange(seq, dtype=F32)[:, None] * inv_freq[None, :]
    cos, sin = jnp.cos(ang), jnp.sin(ang)
    lane = jnp.arange(HEAD_W)
    within = lane % DF_QK_DIM
    freq = within % half
    first = within < half
    second = jnp.logical_and(within >= half, within < ROPE_DIM)
    cos_l, sin_l = cos[:, freq], sin[:, freq]
    c = jnp.where(jnp.logical_or(first, second)[None, :], cos_l, 1.0)
    sn = jnp.where(first[None, :], sin_l, jnp.where(second[None, :], -sin_l, 0.0))
    return c.astype(F32), sn.astype(F32)


def _proj_piece(hn_ref, w_refs, cos_ref, sin_ref, piece):
    nrows = hn_ref.shape[1] // PROJ_ROW_SPLIT
    r0 = (piece % PROJ_ROW_SPLIT) * nrows
    pair = piece // PROJ_ROW_SPLIT
    c0 = pair * 2 * HEAD_W
    w_pair = jnp.concatenate([w_refs[2 * pair][...], w_refs[2 * pair + 1][...]], axis=1)
    acc = jnp.dot(hn_ref[0, r0:r0 + nrows, :], w_pair, preferred_element_type=F32)
    blocks = []
    for k in range(2):
        sec = 2 * pair + k
        xh = acc[:, k * HEAD_W:(k + 1) * HEAD_W]
        if sec in (SEC_DF_Q, SEC_DF_K):
            c, s = cos_ref[r0:r0 + nrows, :], sin_ref[r0:r0 + nrows, :]
            if sec == SEC_DF_Q:
                c, s = c * DF_Q_SCALE, s * DF_Q_SCALE
            lane = lax.broadcasted_iota(jnp.int32, c.shape, 1)
            partner = jnp.where(lane % DF_QK_DIM < ROPE_DIM, lane ^ (ROPE_DIM // 2), lane)
            xh = xh * c + jnp.take_along_axis(xh * s, partner, axis=1)
        elif sec == SEC_SB_Q:
            xh = xh * SB_Q_SCALE
        blocks.append(xh.astype(BF16))
    return r0, nrows, c0, jnp.concatenate(blocks, axis=1)


def _fill_vt(v_ref, vt_ref, chunk):
    r = lax.broadcasted_iota(jnp.int32, (HEAD_W, HEAD_W), 0)
    c = lax.broadcasted_iota(jnp.int32, (HEAD_W, HEAD_W), 1)
    eye = jnp.where(r == c, 1.0, 0.0).astype(BF16)
    for lo in range(0, vt_ref.shape[1], chunk):
        vb = v_ref[0, lo:lo + chunk, :]
        vt_ref[:HEAD_W, lo:lo + chunk] = lax.dot_general(eye, vb, _NT, preferred_element_type=F32).astype(BF16)
    if vt_ref.shape[0] > HEAD_W:
        vt_ref[HEAD_W:, :] = jnp.ones((vt_ref.shape[0] - HEAD_W, vt_ref.shape[1]), BF16)


def _sb_stages(q_ref, k_ref, v_ref, g_ref, o_ref, vt_ref, *, gw, cb):
    _fill_vt(v_ref, vt_ref, gw)

    row = lax.broadcasted_iota(jnp.int32, (cb, 2 * cb), 0)
    col = lax.broadcasted_iota(jnp.int32, (cb, 2 * cb), 1) % cb
    suffix = jnp.where(col >= row, 1.0, 0.0).astype(BF16)
    key_pos = lax.broadcasted_iota(jnp.int32, (cb, gw), 0)
    qry_pos = lax.broadcasted_iota(jnp.int32, (cb, gw), 1)
    strict = [key_pos + r * cb < qry_pos for r in range(gw // cb)]

    per_group = gw // cb
    logits, splits, sums, weights = {}, {}, {}, {}

    def diag_mask(g, b):
        return strict[b - g * per_group] if b >= g * per_group else None

    def stage_logits(g):
        logits[g] = lax.dot_general(k_ref[0, :(g + 1) * gw, :], q_ref[0, g * gw:(g + 1) * gw, :], _NT,
                                    preferred_element_type=F32)

    def stage_split(g):
        y = logits[g]
        sp = jnp.maximum(y, 0.0) + jnp.log2(1.0 + jnp.exp2(-jnp.abs(y)))
        parts = []
        for b in range((g + 1) * per_group):
            sp_b = sp[b * cb:(b + 1) * cb]
            if diag_mask(g, b) is not None:
                sp_b = jnp.where(diag_mask(g, b), sp_b, 0.0)
            hi = sp_b.astype(BF16)
            lo = (sp_b - hi.astype(F32)).astype(BF16)
            parts.append(jnp.concatenate([hi, lo], axis=0))
        splits[g] = parts

    def stage_sums(g):
        sums[g] = [jnp.dot(suffix, part, preferred_element_type=F32) for part in splits.pop(g)]

    def stage_weights(g):
        y, block_sums = logits.pop(g), sums.pop(g)
        nblocks = (g + 1) * per_group
        w = [None] * nblocks
        carry = None
        for b in reversed(range(nblocks)):
            y_b = y[b * cb:(b + 1) * cb]
            w_b = jnp.exp2((y_b if carry is None else y_b - carry) - block_sums[b])
            if diag_mask(g, b) is not None:
                w_b = jnp.where(diag_mask(g, b), w_b, 0.0)
            w[b] = w_b.astype(BF16)
            total = block_sums[b][0:1, :]
            carry = total if carry is None else carry + total
        weights[g] = jnp.concatenate(w, axis=0)

    def stage_out(g):
        rows = slice(g * gw, (g + 1) * gw)
        out_t = jnp.dot(vt_ref[:, :(g + 1) * gw], weights.pop(g), preferred_element_type=F32)
        gate = g_ref[0, rows, :].astype(F32)
        o_ref[0, rows, :] = (out_t.T * (gate * _sigmoid(gate))).astype(BF16)

    return stage_logits, stage_split, stage_sums, stage_weights, stage_out


def _df_stages(lam_ref, sg_ref, q_ref, k_ref, v_ref, g_ref, o_ref, vt_ref, qs_ref, *, gw, lambda_init):
    seq = k_ref.shape[1]
    _fill_vt(v_ref, vt_ref, gw)

    q = q_ref[0]
    lane = lax.broadcasted_iota(jnp.int32, (seq, HEAD_W), 1)
    qs_ref[0] = jnp.where(lane < DF_QK_DIM, q, jnp.zeros_like(q))
    qs_ref[1] = jnp.where(lane >= DF_QK_DIM, q, jnp.zeros_like(q))

    lam_p = lam_ref[...]
    lam = (jnp.exp(jnp.sum(lam_p[0:1] * lam_p[1:2], axis=-1, keepdims=True))
           - jnp.exp(jnp.sum(lam_p[2:3] * lam_p[3:4], axis=-1, keepdims=True))
           + lambda_init)
    causal = lax.broadcasted_iota(jnp.int32, (gw, gw), 0) <= lax.broadcasted_iota(jnp.int32, (gw, gw), 1)
    scores, probs = {}, {}

    def stage_scores(g):
        keys = (g + 1) * gw
        for c in range(2):
            s = lax.dot_general(k_ref[0, :keys, :], qs_ref[c, g * gw:(g + 1) * gw, :], _NT,
                                preferred_element_type=F32)
            s_diag = jnp.where(causal, s[keys - gw:], -jnp.inf)
            s = s_diag if g == 0 else jnp.concatenate([s[:keys - gw], s_diag], axis=0)
            scores[g, c] = (s, jnp.max(s, axis=0, keepdims=True))

    def stage_probs(g):
        for c in range(2):
            s, m = scores.pop((g, c))
            probs[g, c] = jnp.exp2(s - m).astype(BF16)

    def stage_out(g):
        keys = (g + 1) * gw
        rows = slice(g * gw, (g + 1) * gw)
        maps = []
        for c in range(2):
            pv = jnp.dot(vt_ref[:, :keys], probs.pop((g, c)), preferred_element_type=F32)
            maps.append(pv[:HEAD_W] * (1.0 / pv[HEAD_W:HEAD_W + 1]))
        o = maps[0] - lam * maps[1]
        ms = jnp.mean(o * o, axis=0, keepdims=True)
        on = (o * lax.rsqrt(ms + SUBLN_EPS)).T
        gate = g_ref[0, rows, :].astype(F32)
        o_ref[0, rows, :] = (((on * sg_ref[...]) * (1.0 - lambda_init)) * (gate * _sigmoid(gate))).astype(BF16)

    return stage_scores, stage_probs, stage_out


def _attn_kernel(lam_ref, sg_ref, cos_ref, sin_ref, hn_ref, *rest, gw, cb, lambda_init):
    nsec = len(rest) - 6
    w_refs = rest[:nsec]
    sb_o, df_o, proj_ref, sb_vt, df_vt, qs_ref = rest[nsec:]

    @pl.when(pl.program_id(0) == 0)
    def _():
        proj_ref[...] = jnp.zeros_like(proj_ref)

    section = lambda sec: proj_ref.at[:, :, pl.ds(sec * HEAD_W, HEAD_W)]
    sb = _sb_stages(section(SEC_SB_Q), section(SEC_SB_K), section(SEC_SB_V), section(SEC_SB_G), sb_o, sb_vt,
                    gw=gw, cb=cb)
    df = _df_stages(lam_ref, sg_ref, section(SEC_DF_Q), section(SEC_DF_K), section(SEC_DF_V), section(SEC_DF_G),
                    df_o, df_vt, qs_ref, gw=gw, lambda_init=lambda_init)
    schedule = ((0, sb[0]), (0, df[0]), (1, sb[1]), (1, df[1]), (2, sb[2]), (2, df[2]), (3, sb[3]), (4, sb[4]))
    ngroups = proj_ref.shape[1] // gw
    npieces = PROJ_ROW_SPLIT * (proj_ref.shape[2] // (2 * HEAD_W))
    assert npieces + PROJ_FIRST_STEP <= ngroups + max(lag for lag, _ in schedule)
    next_pair = []
    for step in range(ngroups + max(lag for lag, _ in schedule)):
        for lag, stage in schedule:
            if 0 <= step - lag < ngroups:
                stage(step - lag)
        if PROJ_FIRST_STEP <= step < npieces + PROJ_FIRST_STEP:
            next_pair.append(_proj_piece(hn_ref, w_refs, cos_ref, sin_ref, step - PROJ_FIRST_STEP))
    for r0, nrows, c0, block in next_pair:
        proj_ref[0, r0:r0 + nrows, c0:c0 + 2 * HEAD_W] = block


def _attention(hn3d, w, lam_params, subln_g, lambda_init):
    b, s, d = hn3d.shape
    nheads = SEC_W // HEAD_W
    nsec = w.shape[1] // SEC_W
    head_cols = nsec * HEAD_W
    assert SB_HEADS == DF_HEADS == nheads and nsec == 8 and s % ATTN_GW == 0
    npairs = b * nheads
    cos, sin = _rope_tables(s)
    subln_g = subln_g.reshape(1, HEAD_W)
    whole = lambda a: pl.BlockSpec(a.shape, lambda t: (0,) * a.ndim, pipeline_mode=pl.Buffered(1))
    attended = lambda t: jnp.maximum(t - 1, 0)
    projected = lambda t: jnp.minimum(t, npairs - 1)
    out_block = pl.BlockSpec((1, s, HEAD_W), lambda t: (attended(t) // nheads, 0, attended(t) % nheads))
    out_shape = jax.ShapeDtypeStruct((b, s, nheads * HEAD_W), BF16)
    kern = functools.partial(_attn_kernel, gw=ATTN_GW, cb=SB_CUMSUM_ROWS, lambda_init=lambda_init)
    return pl.pallas_call(
        kern,
        grid=(npairs + 1,),
        in_specs=[
            whole(lam_params), whole(subln_g), whole(cos), whole(sin),
            pl.BlockSpec((1, s, d), lambda t: (projected(t) // nheads, 0, 0)),
        ] + [pl.BlockSpec((d, HEAD_W), lambda t, sec=sec: (0, sec * nheads + projected(t) % nheads))
             for sec in range(nsec)],
        out_specs=[out_block, out_block],
        out_shape=[out_shape, out_shape],
        scratch_shapes=[pltpu.VMEM((1, s, head_cols), BF16),
                        pltpu.VMEM((HEAD_W, s), BF16),
                        pltpu.VMEM((HEAD_W + BF16_SUBLANES, s), BF16),
                        pltpu.VMEM((2, s, HEAD_W), BF16)],
        compiler_params=pltpu.CompilerParams(dimension_semantics=("arbitrary",), vmem_limit_bytes=ATTN_VMEM_LIMIT),
        name="attention",
    )(lam_params, subln_g, cos, sin, hn3d, *([w] * nsec))


def _tail_kernel(x_ref, msb_ref, mdf_ref, p_ref, wo_sb_ref, wo_df_ref, gp_ref, wg_ref, wp_ref, gf_ref, o_ref,
                 *, final_norm):
    d = x_ref.shape[1]
    nchunks = d // TAIL_CHUNK
    cols = lambda n: slice(n * TAIL_CHUNK, (n + 1) * TAIL_CHUNK)

    def pipelined(matmul, epilogue):
        nxt = matmul(0)
        for n in range(nchunks):
            acc, nxt = nxt, (matmul(n + 1) if n + 1 < nchunks else None)
            epilogue(acc, n)

    def rms_scale(sum_sq):
        return lax.rsqrt(sum_sq * (1.0 / d) + NORM_EPS)

    h, sum_sq = [None] * nchunks, []

    def out_proj(n):
        return (jnp.dot(msb_ref[...], wo_sb_ref[:, cols(n)], preferred_element_type=F32)
                + jnp.dot(mdf_ref[...], wo_df_ref[:, cols(n)], preferred_element_type=F32))

    def residual(acc, n):
        h[n] = x_ref[:, cols(n)] + acc
        sum_sq.append(jnp.sum(h[n] * h[n], axis=-1, keepdims=True))

    pipelined(out_proj, residual)
    scale = rms_scale(sum(sum_sq))
    hn = jnp.concatenate([((h[n] * scale) * gp_ref[:, cols(n)]).astype(BF16) for n in range(nchunks)], axis=1)

    p_bf16 = p_ref[...].astype(BF16)
    sum_sq = []

    def gate_proj(n):
        return (jnp.dot(hn, wg_ref[:, cols(n)], preferred_element_type=F32),
                jnp.dot(p_bf16, wp_ref[:, cols(n)], preferred_element_type=F32))

    def gated_add(acc, n):
        gate, pp = acc
        h[n] = h[n] + _sigmoid(gate) * pp
        sum_sq.append(jnp.sum(h[n] * h[n], axis=-1, keepdims=True))

    pipelined(gate_proj, gated_add)
    if final_norm:
        scale = rms_scale(sum(sum_sq))
        for n in range(nchunks):
            o_ref[:, cols(n)] = (h[n] * scale) * gf_ref[:, cols(n)]
    else:
        for n in range(nchunks):
            o_ref[:, cols(n)] = h[n]


def _tail(x2d, msb, mdf, p2d, w_out, g_ple, w_gate, w_ple, g_final, final_norm):
    t, d = x2d.shape
    tm = TAIL_TM
    assert t % tm == 0 and d % TAIL_CHUNK == 0
    dsb, ddf, dp = msb.shape[1], mdf.shape[1], p2d.shape[1]
    wo_sb, wo_df = w_out[:dsb], w_out[dsb:]
    rows = lambda width: pl.BlockSpec((tm, width), lambda i: (i, 0))
    whole = lambda a: pl.BlockSpec(a.shape, lambda i: (0, 0), pipeline_mode=pl.Buffered(1))
    g_ple, g_final = g_ple.reshape(1, d), g_final.reshape(1, d)
    return pl.pallas_call(
        functools.partial(_tail_kernel, final_norm=final_norm),
        grid=(t // tm,),
        in_specs=[rows(d), rows(dsb), rows(ddf), rows(dp),
                  whole(wo_sb), whole(wo_df), whole(g_ple), whole(w_gate), whole(w_ple), whole(g_final)],
        out_specs=rows(d),
        out_shape=jax.ShapeDtypeStruct((t, d), F32),
        compiler_params=pltpu.CompilerParams(
            dimension_semantics=("arbitrary",), vmem_limit_bytes=VMEM_LIMIT),
        name="tail",
    )(x2d, msb, mdf, p2d, wo_sb, wo_df, g_ple, w_gate, w_ple, g_final)


def kernel(x, p, norm_mix_g, w_in, lambda_q1, lambda_k1, lambda_q2, lambda_k2, subln_g, w_out, norm_ple_g,
           w_ple_gate, w_ple_proj, norm_final_g):
    b, s, d = x.shape
    depth = w_in.shape[0]
    h = x.reshape(b * s, d)
    for i in range(depth):
        lambda_init = 0.8 - 0.6 * math.exp(-0.3 * i)
        hn = _norm(h, norm_mix_g[i]).reshape(b, s, d)
        lam_params = jnp.stack([lambda_q1[i], lambda_k1[i], lambda_q2[i], lambda_k2[i]])
        msb, mdf = _attention(hn, w_in[i].astype(BF16), lam_params, subln_g[i], lambda_init)
        msb, mdf = msb.reshape(b * s, -1), mdf.reshape(b * s, -1)
        h = _tail(h, msb, mdf, p[i].reshape(b * s, -1), w_out[i].astype(BF16), norm_ple_g[i],
                  w_ple_gate[i].astype(BF16), w_ple_proj[i].astype(BF16), norm_final_g,
                  final_norm=(i == depth - 1))
    return h.reshape(b, s, d)
```

```python
import functools
import math

import jax
import jax.numpy as jnp
from jax import lax
from jax.experimental import pallas as pl
from jax.experimental.pallas import tpu as pltpu

F32 = jnp.float32
BF16 = jnp.bfloat16

SB_HEADS = 8
DF_HEADS = 8
HEAD_W = 128
BF16_SUBLANES = 16
DF_QK_DIM = 64
ROPE_DIM = DF_QK_DIM // 4
ROPE_THETA = 500000.0
NORM_EPS = 1e-6
SUBLN_EPS = 1e-5

SEC_W = 1024
SEC_SB_Q, SEC_SB_K, SEC_SB_V, SEC_SB_G, SEC_DF_Q, SEC_DF_K, SEC_DF_V, SEC_DF_G = range(8)

NORM_TM = 1024
PROJ_ROW_SPLIT = 1
PROJ_FIRST_STEP = 2
ATTN_GW = 256
SB_CUMSUM_ROWS = 128
TAIL_TM = 512
TAIL_CHUNK = 256
VMEM_LIMIT = 56 * 1024 * 1024
ATTN_VMEM_LIMIT = 60 * 1024 * 1024

_NT = (((1,), (1,)), ((), ()))
_LOG2E = math.log2(math.e)
SB_Q_SCALE = HEAD_W ** -0.5 * _LOG2E
DF_Q_SCALE = DF_QK_DIM ** -0.5 * _LOG2E


def _sigmoid(x):
    return 1.0 / (1.0 + jnp.exp(-x))


def _norm_kernel(x_ref, g_ref, o_ref):
    x = x_ref[...]
    ms = jnp.mean(x * x, axis=-1, keepdims=True)
    o_ref[...] = ((x * lax.rsqrt(ms + NORM_EPS)) * g_ref[...]).astype(BF16)


def _norm(x2d, g):
    t, d = x2d.shape
    assert t % NORM_TM == 0
    return pl.pallas_call(
        _norm_kernel,
        grid=(t // NORM_TM,),
        in_specs=[pl.BlockSpec((NORM_TM, d), lambda i: (i, 0)), pl.BlockSpec((1, d), lambda i: (0, 0))],
        out_specs=pl.BlockSpec((NORM_TM, d), lambda i: (i, 0)),
        out_shape=jax.ShapeDtypeStruct((t, d), BF16),
        compiler_params=pltpu.CompilerParams(dimension_semantics=("arbitrary",), vmem_limit_bytes=VMEM_LIMIT),
        name="norm",
    )(x2d, g.reshape(1, d))


def _rope_tables(seq):
    half = ROPE_DIM // 2
    inv_freq = ROPE_THETA ** (-jnp.arange(0, ROPE_DIM, 2, dtype=F32) / ROPE_DIM)
    ang = jnp.arange(seq, dtype=F32)[:, None] * inv_freq[None, :]
    cos, sin = jnp.cos(ang), jnp.sin(ang)
    lane = jnp.arange(HEAD_W)
    within = lane % DF_QK_DIM
    freq = within % half
    first = within < half
    second = jnp.logical_and(within >= half, within < ROPE_DIM)
    cos_l, sin_l = cos[:, freq], sin[:, freq]
    c = jnp.where(jnp.logical_or(first, second)[None, :], cos_l, 1.0)
    sn = jnp.where(first[None, :], sin_l, jnp.where(second[None, :], -sin_l, 0.0))
    return c.astype(F32), sn.astype(F32)


def _proj_piece(hn_ref, w_refs, cos_ref, sin_ref, piece):
    nrows = hn_ref.shape[1] // PROJ_ROW_SPLIT
    r0 = (piece % PROJ_ROW_SPLIT) * nrows
    pair = piece // PROJ_ROW_SPLIT
    c0 = pair * 2 * HEAD_W
    w_pair = jnp.concatenate([w_refs[2 * pair][...], w_refs[2 * pair + 1][...]], axis=1)
    acc = jnp.dot(hn_ref[0, r0:r0 + nrows, :], w_pair, preferred_element_type=F32)
    blocks = []
    for k in range(2):
        sec = 2 * pair + k
        xh = acc[:, k * HEAD_W:(k + 1) * HEAD_W]
        if sec in (SEC_DF_Q, SEC_DF_K):
            c, s = cos_ref[r0:r0 + nrows, :], sin_ref[r0:r0 + nrows, :]
            if sec == SEC_DF_Q:
                c, s = c * DF_Q_SCALE, s * DF_Q_SCALE
            lane = lax.broadcasted_iota(jnp.int32, c.shape, 1)
            partner = jnp.where(lane % DF_QK_DIM < ROPE_DIM, lane ^ (ROPE_DIM // 2), lane)
            xh = xh * c + jnp.take_along_axis(xh * s, partner, axis=1)
        elif sec == SEC_SB_Q:
            xh = xh * SB_Q_SCALE
        blocks.append(xh.astype(BF16))
    return r0, nrows, c0, jnp.concatenate(blocks, axis=1)


def _fill_vt(v_ref, vt_ref, chunk):
    r = lax.broadcasted_iota(jnp.int32, (HEAD_W, HEAD_W), 0)
    c = lax.broadcasted_iota(jnp.int32, (HEAD_W, HEAD_W), 1)
    eye = jnp.where(r == c, 1.0, 0.0).astype(BF16)
    for lo in range(0, vt_ref.shape[1], chunk):
        vb = v_ref[0, lo:lo + chunk, :]
        vt_ref[:HEAD_W, lo:lo + chunk] = lax.dot_general(eye, vb, _NT, preferred_element_type=F32).astype(BF16)
    if vt_ref.shape[0] > HEAD_W:
        vt_ref[HEAD_W:, :] = jnp.ones((vt_ref.shape[0] - HEAD_W, vt_ref.shape[1]), BF16)


def _sb_stages(q_ref, k_ref, v_ref, g_ref, o_ref, vt_ref, *, gw, cb):
    _fill_vt(v_ref, vt_ref, gw)

    row = lax.broadcasted_iota(jnp.int32, (cb, 2 * cb), 0)
    col = lax.broadcasted_iota(jnp.int32, (cb, 2 * cb), 1) % cb
    suffix = jnp.where(col >= row, 1.0, 0.0).astype(BF16)
    key_pos = lax.broadcasted_iota(jnp.int32, (cb, gw), 0)
    qry_pos = lax.broadcasted_iota(jnp.int32, (cb, gw), 1)
    strict = [key_pos + r * cb < qry_pos for r in range(gw // cb)]

    per_group = gw // cb
    logits, splits, sums, weights = {}, {}, {}, {}

    def diag_mask(g, b):
        return strict[b - g * per_group] if b >= g * per_group else None

    def stage_logits(g):
        logits[g] = lax.dot_general(k_ref[0, :(g + 1) * gw, :], q_ref[0, g * gw:(g + 1) * gw, :], _NT,
                                    preferred_element_type=F32)

    def stage_split(g):
        y = logits[g]
        sp = jnp.maximum(y, 0.0) + jnp.log2(1.0 + jnp.exp2(-jnp.abs(y)))
        parts = []
        for b in range((g + 1) * per_group):
            sp_b = sp[b * cb:(b + 1) * cb]
            if diag_mask(g, b) is not None:
                sp_b = jnp.where(diag_mask(g, b), sp_b, 0.0)
            hi = sp_b.astype(BF16)
            lo = (sp_b - hi.astype(F32)).astype(BF16)
            parts.append(jnp.concatenate([hi, lo], axis=0))
        splits[g] = parts

    def stage_sums(g):
        sums[g] = [jnp.dot(suffix, part, preferred_element_type=F32) for part in splits.pop(g)]

    def stage_weights(g):
        y, block_sums = logits.pop(g), sums.pop(g)
        nblocks = (g + 1) * per_group
        w = [None] * nblocks
        carry = None
        for b in reversed(range(nblocks)):
            y_b = y[b * cb:(b + 1) * cb]
            w_b = jnp.exp2((y_b if carry is None else y_b - carry) - block_sums[b])
            if diag_mask(g, b) is not None:
                w_b = jnp.where(diag_mask(g, b), w_b, 0.0)
            w[b] = w_b.astype(BF16)
            total = block_sums[b][0:1, :]
            carry = total if carry is None else carry + total
        weights[g] = jnp.concatenate(w, axis=0)

    def stage_out(g):
        rows = slice(g * gw, (g + 1) * gw)
        out_t = jnp.dot(vt_ref[:, :(g + 1) * gw], weights.pop(g), preferred_element_type=F32)
        gate = g_ref[0, rows, :].astype(F32)
        o_ref[0, rows, :] = (out_t.T * (gate * _sigmoid(gate))).astype(BF16)

    return stage_logits, stage_split, stage_sums, stage_weights, stage_out


def _df_stages(lam_ref, sg_ref, q_ref, k_ref, v_ref, g_ref, o_ref, vt_ref, qs_ref, *, gw, lambda_init):
    seq = k_ref.shape[1]
    _fill_vt(v_ref, vt_ref, gw)

    q = q_ref[0]
    lane = lax.broadcasted_iota(jnp.int32, (seq, HEAD_W), 1)
    qs_ref[0] = jnp.where(lane < DF_QK_DIM, q, jnp.zeros_like(q))
    qs_ref[1] = jnp.where(lane >= DF_QK_DIM, q, jnp.zeros_like(q))

    lam_p = lam_ref[...]
    lam = (jnp.exp(jnp.sum(lam_p[0:1] * lam_p[1:2], axis=-1, keepdims=True))
           - jnp.exp(jnp.sum(lam_p[2:3] * lam_p[3:4], axis=-1, keepdims=True))
           + lambda_init)
    causal = lax.broadcasted_iota(jnp.int32, (gw, gw), 0) <= lax.broadcasted_iota(jnp.int32, (gw, gw), 1)
    scores, probs = {}, {}

    def stage_scores(g):
        keys = (g + 1) * gw
        for c in range(2):
            s = lax.dot_general(k_ref[0, :keys, :], qs_ref[c, g * gw:(g + 1) * gw, :], _NT,
                                preferred_element_type=F32)
            s_diag = jnp.where(causal, s[keys - gw:], -jnp.inf)
            s = s_diag if g == 0 else jnp.concatenate([s[:keys - gw], s_diag], axis=0)
            scores[g, c] = (s, jnp.max(s, axis=0, keepdims=True))

    def stage_probs(g):
        for c in range(2):
            s, m = scores.pop((g, c))
            probs[g, c] = jnp.exp2(s - m).astype(BF16)

    def stage_out(g):
        keys = (g + 1) * gw
        rows = slice(g * gw, (g + 1) * gw)
        maps = []
        for c in range(2):
            pv = jnp.dot(vt_ref[:, :keys], probs.pop((g, c)), preferred_element_type=F32)
            maps.append(pv[:HEAD_W] * (1.0 / pv[HEAD_W:HEAD_W + 1]))
        o = maps[0] - lam * maps[1]
        ms = jnp.mean(o * o, axis=0, keepdims=True)
        on = (o * lax.rsqrt(ms + SUBLN_EPS)).T
        gate = g_ref[0, rows, :].astype(F32)
        o_ref[0, rows, :] = (((on * sg_ref[...]) * (1.0 - lambda_init)) * (gate * _sigmoid(gate))).astype(BF16)

    return stage_scores, stage_probs, stage_out


def _attn_kernel(lam_ref, sg_ref, cos_ref, sin_ref, hn_ref, *rest, gw, cb, lambda_init):
    nsec = len(rest) - 6
    w_refs = rest[:nsec]
    sb_o, df_o, proj_ref, sb_vt, df_vt, qs_ref = rest[nsec:]

    @pl.when(pl.program_id(0) == 0)
    def _():
        proj_ref[...] = jnp.zeros_like(proj_ref)

    section = lambda sec: proj_ref.at[:, :, pl.ds(sec * HEAD_W, HEAD_W)]
    sb = _sb_stages(section(SEC_SB_Q), section(SEC_SB_K), section(SEC_SB_V), section(SEC_SB_G), sb_o, sb_vt,
                    gw=gw, cb=cb)
    df = _df_stages(lam_ref, sg_ref, section(SEC_DF_Q), section(SEC_DF_K), section(SEC_DF_V), section(SEC_DF_G),
                    df_o, df_vt, qs_ref, gw=gw, lambda_init=lambda_init)
    schedule = ((0, sb[0]), (0, df[0]), (1, sb[1]), (1, df[1]), (2, sb[2]), (2, df[2]), (3, sb[3]), (4, sb[4]))
    ngroups = proj_ref.shape[1] // gw
    npieces = PROJ_ROW_SPLIT * (proj_ref.shape[2] // (2 * HEAD_W))
    assert npieces + PROJ_FIRST_STEP <= ngroups + max(lag for lag, _ in schedule)
    next_pair = []
    for step in range(ngroups + max(lag for lag, _ in schedule)):
        for lag, stage in schedule:
            if 0 <= step - lag < ngroups:
                stage(step - lag)
        if PROJ_FIRST_STEP <= step < npieces + PROJ_FIRST_STEP:
            next_pair.append(_proj_piece(hn_ref, w_refs, cos_ref, sin_ref, step - PROJ_FIRST_STEP))
    for r0, nrows, c0, block in next_pair:
        proj_ref[0, r0:r0 + nrows, c0:c0 + 2 * HEAD_W] = block


def _attention(hn3d, w, lam_params, subln_g, lambda_init):
    b, s, d = hn3d.shape
    nheads = SEC_W // HEAD_W
    nsec = w.shape[1] // SEC_W
    head_cols = nsec * HEAD_W
    assert SB_HEADS == DF_HEADS == nheads and nsec == 8 and s % ATTN_GW == 0
    npairs = b * nheads
    cos, sin = _rope_tables(s)
    subln_g = subln_g.reshape(1, HEAD_W)
    whole = lambda a: pl.BlockSpec(a.shape, lambda t: (0,) * a.ndim, pipeline_mode=pl.Buffered(1))
    attended = lambda t: jnp.maximum(t - 1, 0)
    projected = lambda t: jnp.minimum(t, npairs - 1)
    out_block = pl.BlockSpec((1, s, HEAD_W), lambda t: (attended(t) // nheads, 0, attended(t) % nheads))
    out_shape = jax.ShapeDtypeStruct((b, s, nheads * HEAD_W), BF16)
    kern = functools.partial(_attn_kernel, gw=ATTN_GW, cb=SB_CUMSUM_ROWS, lambda_init=lambda_init)
    return pl.pallas_call(
        kern,
        grid=(npairs + 1,),
        in_specs=[
            whole(lam_params), whole(subln_g), whole(cos), whole(sin),
            pl.BlockSpec((1, s, d), lambda t: (projected(t) // nheads, 0, 0)),
        ] + [pl.BlockSpec((d, HEAD_W), lambda t, sec=sec: (0, sec * nheads + projected(t) % nheads))
             for sec in range(nsec)],
        out_specs=[out_block, out_block],
        out_shape=[out_shape, out_shape],
        scratch_shapes=[pltpu.VMEM((1, s, head_cols), BF16),
                        pltpu.VMEM((HEAD_W, s), BF16),
                        pltpu.VMEM((HEAD_W + BF16_SUBLANES, s), BF16),
                        pltpu.VMEM((2, s, HEAD_W), BF16)],
        compiler_params=pltpu.CompilerParams(dimension_semantics=("arbitrary",), vmem_limit_bytes=ATTN_VMEM_LIMIT),
        name="attention",
    )(lam_params, subln_g, cos, sin, hn3d, *([w] * nsec))


def _tail_kernel(x_ref, msb_ref, mdf_ref, p_ref, wo_sb_ref, wo_df_ref, gp_ref, wg_ref, wp_ref, gf_ref, o_ref,
                 *, final_norm):
    d = x_ref.shape[1]
    nchunks = d // TAIL_CHUNK
    cols = lambda n: slice(n * TAIL_CHUNK, (n + 1) * TAIL_CHUNK)

    def pipelined(matmul, epilogue):
        nxt = matmul(0)
        for n in range(nchunks):
            acc, nxt = nxt, (matmul(n + 1) if n + 1 < nchunks else None)
            epilogue(acc, n)

    def rms_scale(sum_sq):
        return lax.rsqrt(sum_sq * (1.0 / d) + NORM_EPS)

    h, sum_sq = [None] * nchunks, []

    def out_proj(n):
        return (jnp.dot(msb_ref[...], wo_sb_ref[:, cols(n)], preferred_element_type=F32)
                + jnp.dot(mdf_ref[...], wo_df_ref[:, cols(n)], preferred_element_type=F32))

    def residual(acc, n):
        h[n] = x_ref[:, cols(n)] + acc
        sum_sq.append(jnp.sum(h[n] * h[n], axis=-1, keepdims=True))

    pipelined(out_proj, residual)
    scale = rms_scale(sum(sum_sq))
    hn = jnp.concatenate([((h[n] * scale) * gp_ref[:, cols(n)]).astype(BF16) for n in range(nchunks)], axis=1)

    p_bf16 = p_ref[...].astype(BF16)
    sum_sq = []

    def gate_proj(n):
        return (jnp.dot(hn, wg_ref[:, cols(n)], preferred_element_type=F32),
                jnp.dot(p_bf16, wp_ref[:, cols(n)], preferred_element_type=F32))

    def gated_add(acc, n):
        gate, pp = acc
        h[n] = h[n] + _sigmoid(gate) * pp
        sum_sq.append(jnp.sum(h[n] * h[n], axis=-1, keepdims=True))

    pipelined(gate_proj, gated_add)
    if final_norm:
        scale = rms_scale(sum(sum_sq))
        for n in range(nchunks):
            o_ref[:, cols(n)] = (h[n] * scale) * gf_ref[:, cols(n)]
    else:
        for n in range(nchunks):
            o_ref[:, cols(n)] = h[n]


def _tail(x2d, msb, mdf, p2d, w_out, g_ple, w_gate, w_ple, g_final, final_norm):
    t, d = x2d.shape
    tm = TAIL_TM
    assert t % tm == 0 and d % TAIL_CHUNK == 0
    dsb, ddf, dp = msb.shape[1], mdf.shape[1], p2d.shape[1]
    wo_sb, wo_df = w_out[:dsb], w_out[dsb:]
    rows = lambda width: pl.BlockSpec((tm, width), lambda i: (i, 0))
    whole = lambda a: pl.BlockSpec(a.shape, lambda i: (0, 0), pipeline_mode=pl.Buffered(1))
    g_ple, g_final = g_ple.reshape(1, d), g_final.reshape(1, d)
    return pl.pallas_call(
        functools.partial(_tail_kernel, final_norm=final_norm),
        grid=(t // tm,),
        in_specs=[rows(d), rows(dsb), rows(ddf), rows(dp),
                  whole(wo_sb), whole(wo_df), whole(g_ple), whole(w_gate), whole(w_ple), whole(g_final)],
        out_specs=rows(d),
        out_shape=jax.ShapeDtypeStruct((t, d), F32),
        compiler_params=pltpu.CompilerParams(
            dimension_semantics=("arbitrary",), vmem_limit_bytes=VMEM_LIMIT),
        name="tail",
    )(x2d, msb, mdf, p2d, wo_sb, wo_df, g_ple, w_gate, w_ple, g_final)


def kernel(x, p, norm_mix_g, w_in, lambda_q1, lambda_k1, lambda_q2, lambda_k2, subln_g, w_out, norm_ple_g,
           w_ple_gate, w_ple_proj, norm_final_g):
    b, s, d = x.shape
    depth = w_in.shape[0]
    h = x.reshape(b * s, d)
    for i in range(depth):
        lambda_init = 0.8 - 0.6 * math.exp(-0.3 * i)
        hn = _norm(h, norm_mix_g[i]).reshape(b, s, d)
        lam_params = jnp.stack([lambda_q1[i], lambda_k1[i], lambda_q2[i], lambda_k2[i]])
        msb, mdf = _attention(hn, w_in[i].astype(BF16), lam_params, subln_g[i], lambda_init)
        msb, mdf = msb.reshape(b * s, -1), mdf.reshape(b * s, -1)
        h = _tail(h, msb, mdf, p[i].reshape(b * s, -1), w_out[i].astype(BF16), norm_ple_g[i],
                  w_ple_gate[i].astype(BF16), w_ple_proj[i].astype(BF16), norm_final_g,
                  final_norm=(i == depth - 1))
    return h.reshape(b, s, d)
```

```python
import functools
import math

import jax
import jax.numpy as jnp
from jax import lax
from jax.experimental import pallas as pl
from jax.experimental.pallas import tpu as pltpu

F32 = jnp.float32
BF16 = jnp.bfloat16

SB_HEADS = 8
DF_HEADS = 8
HEAD_W = 128
BF16_SUBLANES = 16
DF_QK_DIM = 64
ROPE_DIM = DF_QK_DIM // 4
ROPE_THETA = 500000.0
NORM_EPS = 1e-6
SUBLN_EPS = 1e-5

SEC_W = 1024
SEC_SB_Q, SEC_SB_K, SEC_SB_V, SEC_SB_G, SEC_DF_Q, SEC_DF_K, SEC_DF_V, SEC_DF_G = range(8)

NORM_TM = 1024
PROJ_ROW_SPLIT = 4
PROJ_PIECES_PER_STEP = 2
PROJ_FIRST_STEP = 2
ATTN_GW = 256
SB_CUMSUM_ROWS = 128
TAIL_TM = 512
TAIL_CHUNK = 256
VMEM_LIMIT = 56 * 1024 * 1024
ATTN_VMEM_LIMIT = 60 * 1024 * 1024

_NT = (((1,), (1,)), ((), ()))
_LOG2E = math.log2(math.e)
SB_Q_SCALE = HEAD_W ** -0.5 * _LOG2E
DF_Q_SCALE = DF_QK_DIM ** -0.5 * _LOG2E


def _sigmoid(x):
    return 1.0 / (1.0 + jnp.exp(-x))


def _norm_kernel(x_ref, g_ref, o_ref):
    x = x_ref[...]
    ms = jnp.mean(x * x, axis=-1, keepdims=True)
    o_ref[...] = ((x * lax.rsqrt(ms + NORM_EPS)) * g_ref[...]).astype(BF16)


def _norm(x2d, g):
    t, d = x2d.shape
    assert t % NORM_TM == 0
    return pl.pallas_call(
        _norm_kernel,
        grid=(t // NORM_TM,),
        in_specs=[pl.BlockSpec((NORM_TM, d), lambda i: (i, 0)), pl.BlockSpec((1, d), lambda i: (0, 0))],
        out_specs=pl.BlockSpec((NORM_TM, d), lambda i: (i, 0)),
        out_shape=jax.ShapeDtypeStruct((t, d), BF16),
        compiler_params=pltpu.CompilerParams(dimension_semantics=("arbitrary",), vmem_limit_bytes=VMEM_LIMIT),
        name="norm",
    )(x2d, g.reshape(1, d))


def _rope_tables(seq):
    half = ROPE_DIM // 2
    inv_freq = ROPE_THETA ** (-jnp.arange(0, ROPE_DIM, 2, dtype=F32) / ROPE_DIM)
    ang = jnp.arange(seq, dtype=F32)[:, None] * inv_freq[None, :]
    cos, sin = jnp.cos(ang), jnp.sin(ang)
    lane = jnp.arange(HEAD_W)
    within = lane % DF_QK_DIM
    freq = within % half
    first = within < half
    second = jnp.logical_and(within >= half, within < ROPE_DIM)
    cos_l, sin_l = cos[:, freq], sin[:, freq]
    c = jnp.where(jnp.logical_or(first, second)[None, :], cos_l, 1.0)
    sn = jnp.where(first[None, :], sin_l, jnp.where(second[None, :], -sin_l, 0.0))
    return c.astype(F32), sn.astype(F32)


def _proj_piece(hn_ref, w_refs, cos_ref, sin_ref, piece):
    nrows = hn_ref.shape[1] // PROJ_ROW_SPLIT
    r0 = (piece % PROJ_ROW_SPLIT) * nrows
    pair = piece // PROJ_ROW_SPLIT
    c0 = pair * 2 * HEAD_W
    w_pair = jnp.concatenate([w_refs[2 * pair][...], w_refs[2 * pair + 1][...]], axis=1)
    acc = jnp.dot(hn_ref[0, r0:r0 + nrows, :], w_pair, preferred_element_type=F32)
    blocks = []
    for k in range(2):
        sec = 2 * pair + k
        xh = acc[:, k * HEAD_W:(k + 1) * HEAD_W]
        if sec in (SEC_DF_Q, SEC_DF_K):
            c, s = cos_ref[r0:r0 + nrows, :], sin_ref[r0:r0 + nrows, :]
            if sec == SEC_DF_Q:
                c, s = c * DF_Q_SCALE, s * DF_Q_SCALE
            lane = lax.broadcasted_iota(jnp.int32, c.shape, 1)
            partner = jnp.where(lane % DF_QK_DIM < ROPE_DIM, lane ^ (ROPE_DIM // 2), lane)
            xh = xh * c + jnp.take_along_axis(xh * s, partner, axis=1)
        elif sec == SEC_SB_Q:
            xh = xh * SB_Q_SCALE
        blocks.append(xh.astype(BF16))
    return r0, nrows, c0, jnp.concatenate(blocks, axis=1)


def _fill_vt(v_ref, vt_ref, chunk):
    r = lax.broadcasted_iota(jnp.int32, (HEAD_W, HEAD_W), 0)
    c = lax.broadcasted_iota(jnp.int32, (HEAD_W, HEAD_W), 1)
    eye = jnp.where(r == c, 1.0, 0.0).astype(BF16)
    for lo in range(0, vt_ref.shape[1], chunk):
        vb = v_ref[0, lo:lo + chunk, :]
        vt_ref[:HEAD_W, lo:lo + chunk] = lax.dot_general(eye, vb, _NT, preferred_element_type=F32).astype(BF16)
    if vt_ref.shape[0] > HEAD_W:
        vt_ref[HEAD_W:, :] = jnp.ones((vt_ref.shape[0] - HEAD_W, vt_ref.shape[1]), BF16)


def _sb_stages(q_ref, k_ref, v_ref, g_ref, o_ref, vt_ref, *, gw, cb):
    _fill_vt(v_ref, vt_ref, gw)

    row = lax.broadcasted_iota(jnp.int32, (cb, 2 * cb), 0)
    col = lax.broadcasted_iota(jnp.int32, (cb, 2 * cb), 1) % cb
    suffix = jnp.where(col >= row, 1.0, 0.0).astype(BF16)
    key_pos = lax.broadcasted_iota(jnp.int32, (cb, gw), 0)
    qry_pos = lax.broadcasted_iota(jnp.int32, (cb, gw), 1)
    strict = [key_pos + r * cb < qry_pos for r in range(gw // cb)]

    per_group = gw // cb
    logits, splits, sums, weights = {}, {}, {}, {}

    def diag_mask(g, b):
        return strict[b - g * per_group] if b >= g * per_group else None

    def stage_logits(g):
        logits[g] = lax.dot_general(k_ref[0, :(g + 1) * gw, :], q_ref[0, g * gw:(g + 1) * gw, :], _NT,
                                    preferred_element_type=F32)

    def stage_split(g):
        y = logits[g]
        sp = jnp.maximum(y, 0.0) + jnp.log2(1.0 + jnp.exp2(-jnp.abs(y)))
        parts = []
        for b in range((g + 1) * per_group):
            sp_b = sp[b * cb:(b + 1) * cb]
            if diag_mask(g, b) is not None:
                sp_b = jnp.where(diag_mask(g, b), sp_b, 0.0)
            hi = sp_b.astype(BF16)
            lo = (sp_b - hi.astype(F32)).astype(BF16)
            parts.append(jnp.concatenate([hi, lo], axis=0))
        splits[g] = parts

    def stage_sums(g):
        sums[g] = [jnp.dot(suffix, part, preferred_element_type=F32) for part in splits.pop(g)]

    def stage_weights(g):
        y, block_sums = logits.pop(g), sums.pop(g)
        nblocks = (g + 1) * per_group
        w = [None] * nblocks
        carry = None
        for b in reversed(range(nblocks)):
            y_b = y[b * cb:(b + 1) * cb]
            w_b = jnp.exp2((y_b if carry is None else y_b - carry) - block_sums[b])
            if diag_mask(g, b) is not None:
                w_b = jnp.where(diag_mask(g, b), w_b, 0.0)
            w[b] = w_b.astype(BF16)
            total = block_sums[b][0:1, :]
            carry = total if carry is None else carry + total
        weights[g] = jnp.concatenate(w, axis=0)

    def stage_out(g):
        rows = slice(g * gw, (g + 1) * gw)
        out_t = jnp.dot(vt_ref[:, :(g + 1) * gw], weights.pop(g), preferred_element_type=F32)
        gate = g_ref[0, rows, :].astype(F32)
        o_ref[0, rows, :] = (out_t.T * (gate * _sigmoid(gate))).astype(BF16)

    return stage_logits, stage_split, stage_sums, stage_weights, stage_out


def _df_stages(lam_ref, sg_ref, q_ref, k_ref, v_ref, g_ref, o_ref, vt_ref, qs_ref, *, gw, lambda_init):
    seq = k_ref.shape[1]
    _fill_vt(v_ref, vt_ref, gw)

    q = q_ref[0]
    lane = lax.broadcasted_iota(jnp.int32, (seq, HEAD_W), 1)
    qs_ref[0] = jnp.where(lane < DF_QK_DIM, q, jnp.zeros_like(q))
    qs_ref[1] = jnp.where(lane >= DF_QK_DIM, q, jnp.zeros_like(q))

    lam_p = lam_ref[...]
    lam = (jnp.exp(jnp.sum(lam_p[0:1] * lam_p[1:2], axis=-1, keepdims=True))
           - jnp.exp(jnp.sum(lam_p[2:3] * lam_p[3:4], axis=-1, keepdims=True))
           + lambda_init)
    causal = lax.broadcasted_iota(jnp.int32, (gw, gw), 0) <= lax.broadcasted_iota(jnp.int32, (gw, gw), 1)
    scores, probs = {}, {}

    def stage_scores(g):
        keys = (g + 1) * gw
        for c in range(2):
            s = lax.dot_general(k_ref[0, :keys, :], qs_ref[c, g * gw:(g + 1) * gw, :], _NT,
                                preferred_element_type=F32)
            s_diag = jnp.where(causal, s[keys - gw:], -jnp.inf)
            s = s_diag if g == 0 else jnp.concatenate([s[:keys - gw], s_diag], axis=0)
            scores[g, c] = (s, jnp.max(s, axis=0, keepdims=True))

    def stage_probs(g):
        for c in range(2):
            s, m = scores.pop((g, c))
            probs[g, c] = jnp.exp2(s - m).astype(BF16)

    def stage_out(g):
        keys = (g + 1) * gw
        rows = slice(g * gw, (g + 1) * gw)
        maps = []
        for c in range(2):
            pv = jnp.dot(vt_ref[:, :keys], probs.pop((g, c)), preferred_element_type=F32)
            maps.append(pv[:HEAD_W] * (1.0 / pv[HEAD_W:HEAD_W + 1]))
        o = maps[0] - lam * maps[1]
        ms = jnp.mean(o * o, axis=0, keepdims=True)
        on = (o * lax.rsqrt(ms + SUBLN_EPS)).T
        gate = g_ref[0, rows, :].astype(F32)
        o_ref[0, rows, :] = (((on * sg_ref[...]) * (1.0 - lambda_init)) * (gate * _sigmoid(gate))).astype(BF16)

    return stage_scores, stage_probs, stage_out


def _attn_kernel(lam_ref, sg_ref, cos_ref, sin_ref, hn_ref, *rest, gw, cb, lambda_init):
    nsec = len(rest) - 6
    w_refs = rest[:nsec]
    sb_o, df_o, proj_ref, sb_vt, df_vt, qs_ref = rest[nsec:]

    @pl.when(pl.program_id(0) == 0)
    def _():
        proj_ref[...] = jnp.zeros_like(proj_ref)

    section = lambda sec: proj_ref.at[:, :, pl.ds(sec * HEAD_W, HEAD_W)]
    sb = _sb_stages(section(SEC_SB_Q), section(SEC_SB_K), section(SEC_SB_V), section(SEC_SB_G), sb_o, sb_vt,
                    gw=gw, cb=cb)
    df = _df_stages(lam_ref, sg_ref, section(SEC_DF_Q), section(SEC_DF_K), section(SEC_DF_V), section(SEC_DF_G),
                    df_o, df_vt, qs_ref, gw=gw, lambda_init=lambda_init)
    schedule = ((0, sb[0]), (0, df[0]), (1, sb[1]), (1, df[1]), (2, sb[2]), (2, df[2]), (3, sb[3]), (4, sb[4]))
    ngroups = proj_ref.shape[1] // gw
    npieces = PROJ_ROW_SPLIT * (proj_ref.shape[2] // (2 * HEAD_W))
    assert npieces <= PROJ_PIECES_PER_STEP * (ngroups + max(lag for lag, _ in schedule) - PROJ_FIRST_STEP)
    next_pair = []
    for step in range(ngroups + max(lag for lag, _ in schedule)):
        for lag, stage in schedule:
            if 0 <= step - lag < ngroups:
                stage(step - lag)
        for k in range(PROJ_PIECES_PER_STEP):
            piece = (step - PROJ_FIRST_STEP) * PROJ_PIECES_PER_STEP + k
            if 0 <= piece < npieces:
                next_pair.append(_proj_piece(hn_ref, w_refs, cos_ref, sin_ref, piece))
    for r0, nrows, c0, block in next_pair:
        proj_ref[0, r0:r0 + nrows, c0:c0 + 2 * HEAD_W] = block


def _attention(hn3d, w, lam_params, subln_g, lambda_init):
    b, s, d = hn3d.shape
    nheads = SEC_W // HEAD_W
    nsec = w.shape[1] // SEC_W
    head_cols = nsec * HEAD_W
    assert SB_HEADS == DF_HEADS == nheads and nsec == 8 and s % ATTN_GW == 0
    npairs = b * nheads
    cos, sin = _rope_tables(s)
    subln_g = subln_g.reshape(1, HEAD_W)
    whole = lambda a: pl.BlockSpec(a.shape, lambda t: (0,) * a.ndim, pipeline_mode=pl.Buffered(1))
    attended = lambda t: jnp.maximum(t - 1, 0)
    projected = lambda t: jnp.minimum(t, npairs - 1)
    out_block = pl.BlockSpec((1, s, HEAD_W), lambda t: (attended(t) // nheads, 0, attended(t) % nheads))
    out_shape = jax.ShapeDtypeStruct((b, s, nheads * HEAD_W), BF16)
    kern = functools.partial(_attn_kernel, gw=ATTN_GW, cb=SB_CUMSUM_ROWS, lambda_init=lambda_init)
    return pl.pallas_call(
        kern,
        grid=(npairs + 1,),
        in_specs=[
            whole(lam_params), whole(subln_g), whole(cos), whole(sin),
            pl.BlockSpec((1, s, d), lambda t: (projected(t) // nheads, 0, 0)),
        ] + [pl.BlockSpec((d, HEAD_W), lambda t, sec=sec: (0, sec * nheads + projected(t) % nheads))
             for sec in range(nsec)],
        out_specs=[out_block, out_block],
        out_shape=[out_shape, out_shape],
        scratch_shapes=[pltpu.VMEM((1, s, head_cols), BF16),
                        pltpu.VMEM((HEAD_W, s), BF16),
                        pltpu.VMEM((HEAD_W + BF16_SUBLANES, s), BF16),
                        pltpu.VMEM((2, s, HEAD_W), BF16)],
        compiler_params=pltpu.CompilerParams(dimension_semantics=("arbitrary",), vmem_limit_bytes=ATTN_VMEM_LIMIT),
        name="attention",
    )(lam_params, subln_g, cos, sin, hn3d, *([w] * nsec))


def _tail_kernel(x_ref, msb_ref, mdf_ref, p_ref, wo_sb_ref, wo_df_ref, gp_ref, wg_ref, wp_ref, gf_ref, o_ref,
                 *, final_norm):
    d = x_ref.shape[1]
    nchunks = d // TAIL_CHUNK
    cols = lambda n: slice(n * TAIL_CHUNK, (n + 1) * TAIL_CHUNK)

    def pipelined(matmul, epilogue):
        nxt = matmul(0)
        for n in range(nchunks):
            acc, nxt = nxt, (matmul(n + 1) if n + 1 < nchunks else None)
            epilogue(acc, n)

    def rms_scale(sum_sq):
        return lax.rsqrt(sum_sq * (1.0 / d) + NORM_EPS)

    h, sum_sq = [None] * nchunks, []

    def out_proj(n):
        return (jnp.dot(msb_ref[...], wo_sb_ref[:, cols(n)], preferred_element_type=F32)
                + jnp.dot(mdf_ref[...], wo_df_ref[:, cols(n)], preferred_element_type=F32))

    def residual(acc, n):
        h[n] = x_ref[:, cols(n)] + acc
        sum_sq.append(jnp.sum(h[n] * h[n], axis=-1, keepdims=True))

    pipelined(out_proj, residual)
    scale = rms_scale(sum(sum_sq))
    hn = jnp.concatenate([((h[n] * scale) * gp_ref[:, cols(n)]).astype(BF16) for n in range(nchunks)], axis=1)

    p_bf16 = p_ref[...].astype(BF16)
    sum_sq = []

    def gate_proj(n):
        return (jnp.dot(hn, wg_ref[:, cols(n)], preferred_element_type=F32),
                jnp.dot(p_bf16, wp_ref[:, cols(n)], preferred_element_type=F32))

    def gated_add(acc, n):
        gate, pp = acc
        h[n] = h[n] + _sigmoid(gate) * pp
        sum_sq.append(jnp.sum(h[n] * h[n], axis=-1, keepdims=True))

    pipelined(gate_proj, gated_add)
    if final_norm:
        scale = rms_scale(sum(sum_sq))
        for n in range(nchunks):
            o_ref[:, cols(n)] = (h[n] * scale) * gf_ref[:, cols(n)]
    else:
        for n in range(nchunks):
            o_ref[:, cols(n)] = h[n]


def _tail(x2d, msb, mdf, p2d, w_out, g_ple, w_gate, w_ple, g_final, final_norm):
    t, d = x2d.shape
    tm = TAIL_TM
    assert t % tm == 0 and d % TAIL_CHUNK == 0
    dsb, ddf, dp = msb.shape[1], mdf.shape[1], p2d.shape[1]
    wo_sb, wo_df = w_out[:dsb], w_out[dsb:]
    rows = lambda width: pl.BlockSpec((tm, width), lambda i: (i, 0))
    whole = lambda a: pl.BlockSpec(a.shape, lambda i: (0, 0), pipeline_mode=pl.Buffered(1))
    g_ple, g_final = g_ple.reshape(1, d), g_final.reshape(1, d)
    return pl.pallas_call(
        functools.partial(_tail_kernel, final_norm=final_norm),
        grid=(t // tm,),
        in_specs=[rows(d), rows(dsb), rows(ddf), rows(dp),
                  whole(wo_sb), whole(wo_df), whole(g_ple), whole(w_gate), whole(w_ple), whole(g_final)],
        out_specs=rows(d),
        out_shape=jax.ShapeDtypeStruct((t, d), F32),
        compiler_params=pltpu.CompilerParams(
            dimension_semantics=("arbitrary",), vmem_limit_bytes=VMEM_LIMIT),
        name="tail",
    )(x2d, msb, mdf, p2d, wo_sb, wo_df, g_ple, w_gate, w_ple, g_final)


def kernel(x, p, norm_mix_g, w_in, lambda_q1, lambda_k1, lambda_q2, lambda_k2, subln_g, w_out, norm_ple_g,
           w_ple_gate, w_ple_proj, norm_final_g):
    b, s, d = x.shape
    depth = w_in.shape[0]
    h = x.reshape(b * s, d)
    for i in range(depth):
        lambda_init = 0.8 - 0.6 * math.exp(-0.3 * i)
        hn = _norm(h, norm_mix_g[i]).reshape(b, s, d)
        lam_params = jnp.stack([lambda_q1[i], lambda_k1[i], lambda_q2[i], lambda_k2[i]])
        msb, mdf = _attention(hn, w_in[i].astype(BF16), lam_params, subln_g[i], lambda_init)
        msb, mdf = msb.reshape(b * s, -1), mdf.reshape(b * s, -1)
        h = _tail(h, msb, mdf, p[i].reshape(b * s, -1), w_out[i].astype(BF16), norm_ple_g[i],
                  w_ple_gate[i].astype(BF16), w_ple_proj[i].astype(BF16), norm_final_g,
                  final_norm=(i == depth - 1))
    return h.reshape(b, s, d)
```

```python
import functools
import math

import jax
import jax.numpy as jnp
from jax import lax
from jax.experimental import pallas as pl
from jax.experimental.pallas import tpu as pltpu

F32 = jnp.float32
BF16 = jnp.bfloat16

SB_HEADS = 8
DF_HEADS = 8
HEAD_W = 128
BF16_SUBLANES = 16
DF_QK_DIM = 64
ROPE_DIM = DF_QK_DIM // 4
ROPE_THETA = 500000.0
NORM_EPS = 1e-6
SUBLN_EPS = 1e-5

SEC_W = 1024
SEC_SB_Q, SEC_SB_K, SEC_SB_V, SEC_SB_G, SEC_DF_Q, SEC_DF_K, SEC_DF_V, SEC_DF_G = range(8)

NORM_TM = 1024
PROJ_ROW_SPLIT = 2
PROJ_FIRST_STEP = 2
PROJ_PAIR_ORDER = (2, 0, 1, 3)
ATTN_GW = 256
SB_CUMSUM_ROWS = 128
TAIL_TM = 512
TAIL_CHUNK = 256
VMEM_LIMIT = 56 * 1024 * 1024
ATTN_VMEM_LIMIT = 60 * 1024 * 1024

_NT = (((1,), (1,)), ((), ()))
_LOG2E = math.log2(math.e)
SB_Q_SCALE = HEAD_W ** -0.5 * _LOG2E
DF_Q_SCALE = DF_QK_DIM ** -0.5 * _LOG2E


def _sigmoid(x):
    return 1.0 / (1.0 + jnp.exp(-x))


def _norm_kernel(x_ref, g_ref, o_ref):
    x = x_ref[...]
    ms = jnp.mean(x * x, axis=-1, keepdims=True)
    o_ref[...] = ((x * lax.rsqrt(ms + NORM_EPS)) * g_ref[...]).astype(BF16)


def _norm(x2d, g):
    t, d = x2d.shape
    assert t % NORM_TM == 0
    return pl.pallas_call(
        _norm_kernel,
        grid=(t // NORM_TM,),
        in_specs=[pl.BlockSpec((NORM_TM, d), lambda i: (i, 0)), pl.BlockSpec((1, d), lambda i: (0, 0))],
        out_specs=pl.BlockSpec((NORM_TM, d), lambda i: (i, 0)),
        out_shape=jax.ShapeDtypeStruct((t, d), BF16),
        compiler_params=pltpu.CompilerParams(dimension_semantics=("arbitrary",), vmem_limit_bytes=VMEM_LIMIT),
        name="norm",
    )(x2d, g.reshape(1, d))


def _rope_tables(seq):
    half = ROPE_DIM // 2
    inv_freq = ROPE_THETA ** (-jnp.arange(0, ROPE_DIM, 2, dtype=F32) / ROPE_DIM)
    ang = jnp.arange(seq, dtype=F32)[:, None] * inv_freq[None, :]
    cos, sin = jnp.cos(ang), jnp.sin(ang)
    lane = jnp.arange(HEAD_W)
    within = lane % DF_QK_DIM
    freq = within % half
    first = within < half
    second = jnp.logical_and(within >= half, within < ROPE_DIM)
    cos_l, sin_l = cos[:, freq], sin[:, freq]
    c = jnp.where(jnp.logical_or(first, second)[None, :], cos_l, 1.0)
    sn = jnp.where(first[None, :], sin_l, jnp.where(second[None, :], -sin_l, 0.0))
    return c.astype(F32), sn.astype(F32)


def _proj_piece(hn_ref, w_refs, cos_ref, sin_ref, piece):
    nrows = hn_ref.shape[1] // PROJ_ROW_SPLIT
    r0 = (piece % PROJ_ROW_SPLIT) * nrows
    pair = PROJ_PAIR_ORDER[piece // PROJ_ROW_SPLIT]
    c0 = pair * 2 * HEAD_W
    w_pair = jnp.concatenate([w_refs[2 * pair][...], w_refs[2 * pair + 1][...]], axis=1)
    acc = jnp.dot(hn_ref[0, r0:r0 + nrows, :], w_pair, preferred_element_type=F32)
    blocks = []
    for k in range(2):
        sec = 2 * pair + k
        xh = acc[:, k * HEAD_W:(k + 1) * HEAD_W]
        if sec in (SEC_DF_Q, SEC_DF_K):
            c, s = cos_ref[r0:r0 + nrows, :], sin_ref[r0:r0 + nrows, :]
            if sec == SEC_DF_Q:
                c, s = c * DF_Q_SCALE, s * DF_Q_SCALE
            lane = lax.broadcasted_iota(jnp.int32, c.shape, 1)
            partner = jnp.where(lane % DF_QK_DIM < ROPE_DIM, lane ^ (ROPE_DIM // 2), lane)
            xh = xh * c + jnp.take_along_axis(xh * s, partner, axis=1)
        elif sec == SEC_SB_Q:
            xh = xh * SB_Q_SCALE
        blocks.append(xh.astype(BF16))
    return r0, nrows, c0, jnp.concatenate(blocks, axis=1)


def _fill_vt(v_ref, vt_ref, chunk):
    r = lax.broadcasted_iota(jnp.int32, (HEAD_W, HEAD_W), 0)
    c = lax.broadcasted_iota(jnp.int32, (HEAD_W, HEAD_W), 1)
    eye = jnp.where(r == c, 1.0, 0.0).astype(BF16)
    for lo in range(0, vt_ref.shape[1], chunk):
        vb = v_ref[0, lo:lo + chunk, :]
        vt_ref[:HEAD_W, lo:lo + chunk] = lax.dot_general(eye, vb, _NT, preferred_element_type=F32).astype(BF16)
    if vt_ref.shape[0] > HEAD_W:
        vt_ref[HEAD_W:, :] = jnp.ones((vt_ref.shape[0] - HEAD_W, vt_ref.shape[1]), BF16)


def _sb_stages(q_ref, k_ref, v_ref, g_ref, o_ref, vt_ref, *, gw, cb):
    _fill_vt(v_ref, vt_ref, gw)

    row = lax.broadcasted_iota(jnp.int32, (cb, 2 * cb), 0)
    col = lax.broadcasted_iota(jnp.int32, (cb, 2 * cb), 1) % cb
    suffix = jnp.where(col >= row, 1.0, 0.0).astype(BF16)
    key_pos = lax.broadcasted_iota(jnp.int32, (cb, gw), 0)
    qry_pos = lax.broadcasted_iota(jnp.int32, (cb, gw), 1)
    strict = [key_pos + r * cb < qry_pos for r in range(gw // cb)]

    per_group = gw // cb
    logits, splits, sums, weights = {}, {}, {}, {}

    def diag_mask(g, b):
        return strict[b - g * per_group] if b >= g * per_group else None

    def stage_logits(g):
        logits[g] = lax.dot_general(k_ref[0, :(g + 1) * gw, :], q_ref[0, g * gw:(g + 1) * gw, :], _NT,
                                    preferred_element_type=F32)

    def stage_split(g):
        y = logits[g]
        sp = jnp.maximum(y, 0.0) + jnp.log2(1.0 + jnp.exp2(-jnp.abs(y)))
        parts = []
        for b in range((g + 1) * per_group):
            sp_b = sp[b * cb:(b + 1) * cb]
            if diag_mask(g, b) is not None:
                sp_b = jnp.where(diag_mask(g, b), sp_b, 0.0)
            hi = sp_b.astype(BF16)
            lo = (sp_b - hi.astype(F32)).astype(BF16)
            parts.append(jnp.concatenate([hi, lo], axis=0))
        splits[g] = parts

    def stage_sums(g):
        sums[g] = [jnp.dot(suffix, part, preferred_element_type=F32) for part in splits.pop(g)]

    def stage_weights(g):
        y, block_sums = logits.pop(g), sums.pop(g)
        nblocks = (g + 1) * per_group
        w = [None] * nblocks
        carry = None
        for b in reversed(range(nblocks)):
            y_b = y[b * cb:(b + 1) * cb]
            w_b = jnp.exp2((y_b if carry is None else y_b - carry) - block_sums[b])
            if diag_mask(g, b) is not None:
                w_b = jnp.where(diag_mask(g, b), w_b, 0.0)
            w[b] = w_b.astype(BF16)
            total = block_sums[b][0:1, :]
            carry = total if carry is None else carry + total
        weights[g] = jnp.concatenate(w, axis=0)

    def stage_out(g):
        rows = slice(g * gw, (g + 1) * gw)
        out_t = jnp.dot(vt_ref[:, :(g + 1) * gw], weights.pop(g), preferred_element_type=F32)
        gate = g_ref[0, rows, :].astype(F32)
        o_ref[0, rows, :] = (out_t.T * (gate * _sigmoid(gate))).astype(BF16)

    return stage_logits, stage_split, stage_sums, stage_weights, stage_out


def _df_stages(lam_ref, sg_ref, q_ref, k_ref, v_ref, g_ref, o_ref, vt_ref, qs_ref, *, gw, lambda_init):
    seq = k_ref.shape[1]
    _fill_vt(v_ref, vt_ref, gw)

    q = q_ref[0]
    lane = lax.broadcasted_iota(jnp.int32, (seq, HEAD_W), 1)
    qs_ref[0] = jnp.where(lane < DF_QK_DIM, q, jnp.zeros_like(q))
    qs_ref[1] = jnp.where(lane >= DF_QK_DIM, q, jnp.zeros_like(q))

    lam_p = lam_ref[...]
    lam = (jnp.exp(jnp.sum(lam_p[0:1] * lam_p[1:2], axis=-1, keepdims=True))
           - jnp.exp(jnp.sum(lam_p[2:3] * lam_p[3:4], axis=-1, keepdims=True))
           + lambda_init)
    causal = lax.broadcasted_iota(jnp.int32, (gw, gw), 0) <= lax.broadcasted_iota(jnp.int32, (gw, gw), 1)
    scores, probs = {}, {}

    def stage_scores(g):
        keys = (g + 1) * gw
        for c in range(2):
            s = lax.dot_general(k_ref[0, :keys, :], qs_ref[c, g * gw:(g + 1) * gw, :], _NT,
                                preferred_element_type=F32)
            s_diag = jnp.where(causal, s[keys - gw:], -jnp.inf)
            s = s_diag if g == 0 else jnp.concatenate([s[:keys - gw], s_diag], axis=0)
            scores[g, c] = (s, jnp.max(s, axis=0, keepdims=True))

    def stage_probs(g):
        for c in range(2):
            s, m = scores.pop((g, c))
            probs[g, c] = jnp.exp2(s - m).astype(BF16)

    def stage_out(g):
        keys = (g + 1) * gw
        rows = slice(g * gw, (g + 1) * gw)
        maps = []
        for c in range(2):
            pv = jnp.dot(vt_ref[:, :keys], probs.pop((g, c)), preferred_element_type=F32)
            maps.append(pv[:HEAD_W] * (1.0 / pv[HEAD_W:HEAD_W + 1]))
        o = maps[0] - lam * maps[1]
        ms = jnp.mean(o * o, axis=0, keepdims=True)
        on = (o * lax.rsqrt(ms + SUBLN_EPS)).T
        gate = g_ref[0, rows, :].astype(F32)
        o_ref[0, rows, :] = (((on * sg_ref[...]) * (1.0 - lambda_init)) * (gate * _sigmoid(gate))).astype(BF16)

    return stage_scores, stage_probs, stage_out


def _attn_kernel(lam_ref, sg_ref, cos_ref, sin_ref, hn_ref, *rest, gw, cb, lambda_init):
    nsec = len(rest) - 6
    w_refs = rest[:nsec]
    sb_o, df_o, proj_ref, sb_vt, df_vt, qs_ref = rest[nsec:]

    @pl.when(pl.program_id(0) == 0)
    def _():
        proj_ref[...] = jnp.zeros_like(proj_ref)

    section = lambda sec: proj_ref.at[:, :, pl.ds(sec * HEAD_W, HEAD_W)]
    sb = _sb_stages(section(SEC_SB_Q), section(SEC_SB_K), section(SEC_SB_V), section(SEC_SB_G), sb_o, sb_vt,
                    gw=gw, cb=cb)
    df = _df_stages(lam_ref, sg_ref, section(SEC_DF_Q), section(SEC_DF_K), section(SEC_DF_V), section(SEC_DF_G),
                    df_o, df_vt, qs_ref, gw=gw, lambda_init=lambda_init)
    schedule = ((0, sb[0]), (0, df[0]), (1, sb[1]), (1, df[1]), (2, sb[2]), (2, df[2]), (3, sb[3]), (4, sb[4]))
    ngroups = proj_ref.shape[1] // gw
    npieces = PROJ_ROW_SPLIT * (proj_ref.shape[2] // (2 * HEAD_W))
    assert npieces + PROJ_FIRST_STEP <= ngroups + max(lag for lag, _ in schedule)
    next_pair = []
    for step in range(ngroups + max(lag for lag, _ in schedule)):
        for lag, stage in schedule:
            if 0 <= step - lag < ngroups:
                stage(step - lag)
        if PROJ_FIRST_STEP <= step < npieces + PROJ_FIRST_STEP:
            next_pair.append(_proj_piece(hn_ref, w_refs, cos_ref, sin_ref, step - PROJ_FIRST_STEP))
    for r0, nrows, c0, block in next_pair:
        proj_ref[0, r0:r0 + nrows, c0:c0 + 2 * HEAD_W] = block


def _attention(hn3d, w, lam_params, subln_g, lambda_init):
    b, s, d = hn3d.shape
    nheads = SEC_W // HEAD_W
    nsec = w.shape[1] // SEC_W
    head_cols = nsec * HEAD_W
    assert SB_HEADS == DF_HEADS == nheads and nsec == 8 and s % ATTN_GW == 0
    npairs = b * nheads
    cos, sin = _rope_tables(s)
    subln_g = subln_g.reshape(1, HEAD_W)
    whole = lambda a: pl.BlockSpec(a.shape, lambda t: (0,) * a.ndim, pipeline_mode=pl.Buffered(1))
    attended = lambda t: jnp.maximum(t - 1, 0)
    projected = lambda t: jnp.minimum(t, npairs - 1)
    out_block = pl.BlockSpec((1, s, HEAD_W), lambda t: (attended(t) // nheads, 0, attended(t) % nheads))
    out_shape = jax.ShapeDtypeStruct((b, s, nheads * HEAD_W), BF16)
    kern = functools.partial(_attn_kernel, gw=ATTN_GW, cb=SB_CUMSUM_ROWS, lambda_init=lambda_init)
    return pl.pallas_call(
        kern,
        grid=(npairs + 1,),
        in_specs=[
            whole(lam_params), whole(subln_g), whole(cos), whole(sin),
            pl.BlockSpec((1, s, d), lambda t: (projected(t) // nheads, 0, 0)),
        ] + [pl.BlockSpec((d, HEAD_W), lambda t, sec=sec: (0, sec * nheads + projected(t) % nheads))
             for sec in range(nsec)],
        out_specs=[out_block, out_block],
        out_shape=[out_shape, out_shape],
        scratch_shapes=[pltpu.VMEM((1, s, head_cols), BF16),
                        pltpu.VMEM((HEAD_W, s), BF16),
                        pltpu.VMEM((HEAD_W + BF16_SUBLANES, s), BF16),
                        pltpu.VMEM((2, s, HEAD_W), BF16)],
        compiler_params=pltpu.CompilerParams(dimension_semantics=("arbitrary",), vmem_limit_bytes=ATTN_VMEM_LIMIT),
        name="attention",
    )(lam_params, subln_g, cos, sin, hn3d, *([w] * nsec))


def _tail_kernel(x_ref, msb_ref, mdf_ref, p_ref, wo_sb_ref, wo_df_ref, gp_ref, wg_ref, wp_ref, gf_ref, o_ref,
                 *, final_norm):
    d = x_ref.shape[1]
    nchunks = d // TAIL_CHUNK
    cols = lambda n: slice(n * TAIL_CHUNK, (n + 1) * TAIL_CHUNK)

    def pipelined(matmul, epilogue):
        nxt = matmul(0)
        for n in range(nchunks):
            acc, nxt = nxt, (matmul(n + 1) if n + 1 < nchunks else None)
            epilogue(acc, n)

    def rms_scale(sum_sq):
        return lax.rsqrt(sum_sq * (1.0 / d) + NORM_EPS)

    h, sum_sq = [None] * nchunks, []

    def out_proj(n):
        return (jnp.dot(msb_ref[...], wo_sb_ref[:, cols(n)], preferred_element_type=F32)
                + jnp.dot(mdf_ref[...], wo_df_ref[:, cols(n)], preferred_element_type=F32))

    def residual(acc, n):
        h[n] = x_ref[:, cols(n)] + acc
        sum_sq.append(jnp.sum(h[n] * h[n], axis=-1, keepdims=True))

    pipelined(out_proj, residual)
    scale = rms_scale(sum(sum_sq))
    hn = jnp.concatenate([((h[n] * scale) * gp_ref[:, cols(n)]).astype(BF16) for n in range(nchunks)], axis=1)

    p_bf16 = p_ref[...].astype(BF16)
    sum_sq = []

    def gate_proj(n):
        return (jnp.dot(hn, wg_ref[:, cols(n)], preferred_element_type=F32),
                jnp.dot(p_bf16, wp_ref[:, cols(n)], preferred_element_type=F32))

    def gated_add(acc, n):
        gate, pp = acc
        h[n] = h[n] + _sigmoid(gate) * pp
        sum_sq.append(jnp.sum(h[n] * h[n], axis=-1, keepdims=True))

    pipelined(gate_proj, gated_add)
    if final_norm:
        scale = rms_scale(sum(sum_sq))
        for n in range(nchunks):
            o_ref[:, cols(n)] = (h[n] * scale) * gf_ref[:, cols(n)]
    else:
        for n in range(nchunks):
            o_ref[:, cols(n)] = h[n]


def _tail(x2d, msb, mdf, p2d, w_out, g_ple, w_gate, w_ple, g_final, final_norm):
    t, d = x2d.shape
    tm = TAIL_TM
    assert t % tm == 0 and d % TAIL_CHUNK == 0
    dsb, ddf, dp = msb.shape[1], mdf.shape[1], p2d.shape[1]
    wo_sb, wo_df = w_out[:dsb], w_out[dsb:]
    rows = lambda width: pl.BlockSpec((tm, width), lambda i: (i, 0))
    whole = lambda a: pl.BlockSpec(a.shape, lambda i: (0, 0), pipeline_mode=pl.Buffered(1))
    g_ple, g_final = g_ple.reshape(1, d), g_final.reshape(1, d)
    return pl.pallas_call(
        functools.partial(_tail_kernel, final_norm=final_norm),
        grid=(t // tm,),
        in_specs=[rows(d), rows(dsb), rows(ddf), rows(dp),
                  whole(wo_sb), whole(wo_df), whole(g_ple), whole(w_gate), whole(w_ple), whole(g_final)],
        out_specs=rows(d),
        out_shape=jax.ShapeDtypeStruct((t, d), F32),
        compiler_params=pltpu.CompilerParams(
            dimension_semantics=("arbitrary",), vmem_limit_bytes=VMEM_LIMIT),
        name="tail",
    )(x2d, msb, mdf, p2d, wo_sb, wo_df, g_ple, w_gate, w_ple, g_final)


def kernel(x, p, norm_mix_g, w_in, lambda_q1, lambda_k1, lambda_q2, lambda_k2, subln_g, w_out, norm_ple_g,
           w_ple_gate, w_ple_proj, norm_final_g):
    b, s, d = x.shape
    depth = w_in.shape[0]
    h = x.reshape(b * s, d)
    for i in range(depth):
        lambda_init = 0.8 - 0.6 * math.exp(-0.3 * i)
        hn = _norm(h, norm_mix_g[i]).reshape(b, s, d)
        lam_params = jnp.stack([lambda_q1[i], lambda_k1[i], lambda_q2[i], lambda_k2[i]])
        msb, mdf = _attention(hn, w_in[i].astype(BF16), lam_params, subln_g[i], lambda_init)
        msb, mdf = msb.reshape(b * s, -1), mdf.reshape(b * s, -1)
        h = _tail(h, msb, mdf, p[i].reshape(b * s, -1), w_out[i].astype(BF16), norm_ple_g[i],
                  w_ple_gate[i].astype(BF16), w_ple_proj[i].astype(BF16), norm_final_g,
                  final_norm=(i == depth - 1))
    return h.reshape(b, s, d)
```

```python
import functools
import math

import jax
import jax.numpy as jnp
from jax import lax
from jax.experimental import pallas as pl
from jax.experimental.pallas import tpu as pltpu

F32 = jnp.float32
BF16 = jnp.bfloat16

SB_HEADS = 8
DF_HEADS = 8
HEAD_W = 128
BF16_SUBLANES = 16
DF_QK_DIM = 64
ROPE_DIM = DF_QK_DIM // 4
ROPE_THETA = 500000.0
NORM_EPS = 1e-6
SUBLN_EPS = 1e-5

SEC_W = 1024
SEC_SB_Q, SEC_SB_K, SEC_SB_V, SEC_SB_G, SEC_DF_Q, SEC_DF_K, SEC_DF_V, SEC_DF_G = range(8)

NORM_TM = 1024
PROJ_ROW_SPLIT = 2
PROJ_FIRST_STEP = 2
ATTN_GW = 256
SB_CUMSUM_ROWS = 128
TAIL_TM = 512
TAIL_CHUNK = 256
VMEM_LIMIT = 56 * 1024 * 1024
ATTN_VMEM_LIMIT = 60 * 1024 * 1024

_NT = (((1,), (1,)), ((), ()))
_LOG2E = math.log2(math.e)
SB_Q_SCALE = HEAD_W ** -0.5 * _LOG2E
DF_Q_SCALE = DF_QK_DIM ** -0.5 * _LOG2E


def _sigmoid(x):
    return 1.0 / (1.0 + jnp.exp(-x))


def _norm_kernel(x_ref, g_ref, o_ref):
    x = x_ref[...]
    ms = jnp.mean(x * x, axis=-1, keepdims=True)
    o_ref[...] = ((x * lax.rsqrt(ms + NORM_EPS)) * g_ref[...]).astype(BF16)


def _norm(x2d, g):
    t, d = x2d.shape
    assert t % NORM_TM == 0
    return pl.pallas_call(
        _norm_kernel,
        grid=(t // NORM_TM,),
        in_specs=[pl.BlockSpec((NORM_TM, d), lambda i: (i, 0)), pl.BlockSpec((1, d), lambda i: (0, 0))],
        out_specs=pl.BlockSpec((NORM_TM, d), lambda i: (i, 0)),
        out_shape=jax.ShapeDtypeStruct((t, d), BF16),
        compiler_params=pltpu.CompilerParams(dimension_semantics=("arbitrary",), vmem_limit_bytes=VMEM_LIMIT),
        name="norm",
    )(x2d, g.reshape(1, d))


def _rope_tables(seq):
    half = ROPE_DIM // 2
    inv_freq = ROPE_THETA ** (-jnp.arange(0, ROPE_DIM, 2, dtype=F32) / ROPE_DIM)
    ang = jnp.arange(seq, dtype=F32)[:, None] * inv_freq[None, :]
    cos, sin = jnp.cos(ang), jnp.sin(ang)
    lane = jnp.arange(HEAD_W)
    within = lane % DF_QK_DIM
    freq = within % half
    first = within < half
    second = jnp.logical_and(within >= half, within < ROPE_DIM)
    cos_l, sin_l = cos[:, freq], sin[:, freq]
    c = jnp.where(jnp.logical_or(first, second)[None, :], cos_l, 1.0)
    sn = jnp.where(first[None, :], sin_l, jnp.where(second[None, :], -sin_l, 0.0))
    return c.astype(F32), sn.astype(F32)


def _proj_piece(hn_ref, w_refs, cos_ref, sin_ref, piece):
    nrows = hn_ref.shape[1] // PROJ_ROW_SPLIT
    r0 = (piece % PROJ_ROW_SPLIT) * nrows
    pair = piece // PROJ_ROW_SPLIT
    c0 = pair * 2 * HEAD_W
    w_pair = jnp.concatenate([w_refs[2 * pair][...], w_refs[2 * pair + 1][...]], axis=1)
    acc = jnp.dot(hn_ref[0, r0:r0 + nrows, :], w_pair, preferred_element_type=F32)
    blocks = []
    for k in range(2):
        sec = 2 * pair + k
        xh = acc[:, k * HEAD_W:(k + 1) * HEAD_W]
        if sec in (SEC_DF_Q, SEC_DF_K):
            c, s = cos_ref[r0:r0 + nrows, :], sin_ref[r0:r0 + nrows, :]
            if sec == SEC_DF_Q:
                c, s = c * DF_Q_SCALE, s * DF_Q_SCALE
            lane = lax.broadcasted_iota(jnp.int32, c.shape, 1)
            partner = jnp.where(lane % DF_QK_DIM < ROPE_DIM, lane ^ (ROPE_DIM // 2), lane)
            xh = xh * c + jnp.take_along_axis(xh * s, partner, axis=1)
        elif sec == SEC_SB_Q:
            xh = xh * SB_Q_SCALE
        blocks.append(xh.astype(BF16))
    return r0, nrows, c0, jnp.concatenate(blocks, axis=1)


def _fill_vt(v_ref, vt_ref, chunk):
    for lo in range(0, vt_ref.shape[1], chunk):
        vt_ref[:HEAD_W, lo:lo + chunk] = v_ref[0, lo:lo + chunk, :].astype(F32).T.astype(BF16)
    if vt_ref.shape[0] > HEAD_W:
        vt_ref[HEAD_W:, :] = jnp.ones((vt_ref.shape[0] - HEAD_W, vt_ref.shape[1]), BF16)


def _sb_stages(q_ref, k_ref, v_ref, g_ref, o_ref, vt_ref, *, gw, cb):
    _fill_vt(v_ref, vt_ref, gw)

    row = lax.broadcasted_iota(jnp.int32, (cb, 2 * cb), 0)
    col = lax.broadcasted_iota(jnp.int32, (cb, 2 * cb), 1) % cb
    suffix = jnp.where(col >= row, 1.0, 0.0).astype(BF16)
    key_pos = lax.broadcasted_iota(jnp.int32, (cb, gw), 0)
    qry_pos = lax.broadcasted_iota(jnp.int32, (cb, gw), 1)
    strict = [key_pos + r * cb < qry_pos for r in range(gw // cb)]

    per_group = gw // cb
    logits, splits, sums, weights = {}, {}, {}, {}

    def diag_mask(g, b):
        return strict[b - g * per_group] if b >= g * per_group else None

    def stage_logits(g):
        logits[g] = lax.dot_general(k_ref[0, :(g + 1) * gw, :], q_ref[0, g * gw:(g + 1) * gw, :], _NT,
                                    preferred_element_type=F32)

    def stage_split(g):
        y = logits[g]
        sp = jnp.maximum(y, 0.0) + jnp.log2(1.0 + jnp.exp2(-jnp.abs(y)))
        parts = []
        for b in range((g + 1) * per_group):
            sp_b = sp[b * cb:(b + 1) * cb]
            if diag_mask(g, b) is not None:
                sp_b = jnp.where(diag_mask(g, b), sp_b, 0.0)
            hi = sp_b.astype(BF16)
            lo = (sp_b - hi.astype(F32)).astype(BF16)
            parts.append(jnp.concatenate([hi, lo], axis=0))
        splits[g] = parts

    def stage_sums(g):
        sums[g] = [jnp.dot(suffix, part, preferred_element_type=F32) for part in splits.pop(g)]

    def stage_weights(g):
        y, block_sums = logits.pop(g), sums.pop(g)
        nblocks = (g + 1) * per_group
        w = [None] * nblocks
        carry = None
        for b in reversed(range(nblocks)):
            y_b = y[b * cb:(b + 1) * cb]
            w_b = jnp.exp2((y_b if carry is None else y_b - carry) - block_sums[b])
            if diag_mask(g, b) is not None:
                w_b = jnp.where(diag_mask(g, b), w_b, 0.0)
            w[b] = w_b.astype(BF16)
            total = block_sums[b][0:1, :]
            carry = total if carry is None else carry + total
        weights[g] = jnp.concatenate(w, axis=0)

    def stage_out(g):
        rows = slice(g * gw, (g + 1) * gw)
        out_t = jnp.dot(vt_ref[:, :(g + 1) * gw], weights.pop(g), preferred_element_type=F32)
        gate = g_ref[0, rows, :].astype(F32)
        o_ref[0, rows, :] = (out_t.T * (gate * _sigmoid(gate))).astype(BF16)

    return stage_logits, stage_split, stage_sums, stage_weights, stage_out


def _df_stages(lam_ref, sg_ref, q_ref, k_ref, v_ref, g_ref, o_ref, vt_ref, qs_ref, *, gw, lambda_init):
    seq = k_ref.shape[1]
    _fill_vt(v_ref, vt_ref, gw)

    q = q_ref[0]
    lane = lax.broadcasted_iota(jnp.int32, (seq, HEAD_W), 1)
    qs_ref[0] = jnp.where(lane < DF_QK_DIM, q, jnp.zeros_like(q))
    qs_ref[1] = jnp.where(lane >= DF_QK_DIM, q, jnp.zeros_like(q))

    lam_p = lam_ref[...]
    lam = (jnp.exp(jnp.sum(lam_p[0:1] * lam_p[1:2], axis=-1, keepdims=True))
           - jnp.exp(jnp.sum(lam_p[2:3] * lam_p[3:4], axis=-1, keepdims=True))
           + lambda_init)
    causal = lax.broadcasted_iota(jnp.int32, (gw, gw), 0) <= lax.broadcasted_iota(jnp.int32, (gw, gw), 1)
    scores, probs = {}, {}

    def stage_scores(g):
        keys = (g + 1) * gw
        for c in range(2):
            s = lax.dot_general(k_ref[0, :keys, :], qs_ref[c, g * gw:(g + 1) * gw, :], _NT,
                                preferred_element_type=F32)
            s_diag = jnp.where(causal, s[keys - gw:], -jnp.inf)
            s = s_diag if g == 0 else jnp.concatenate([s[:keys - gw], s_diag], axis=0)
            scores[g, c] = (s, jnp.max(s, axis=0, keepdims=True))

    def stage_probs(g):
        for c in range(2):
            s, m = scores.pop((g, c))
            probs[g, c] = jnp.exp2(s - m).astype(BF16)

    def stage_out(g):
        keys = (g + 1) * gw
        rows = slice(g * gw, (g + 1) * gw)
        maps = []
        for c in range(2):
            pv = jnp.dot(vt_ref[:, :keys], probs.pop((g, c)), preferred_element_type=F32)
            maps.append(pv[:HEAD_W] * (1.0 / pv[HEAD_W:HEAD_W + 1]))
        o = maps[0] - lam * maps[1]
        ms = jnp.mean(o * o, axis=0, keepdims=True)
        on = (o * lax.rsqrt(ms + SUBLN_EPS)).T
        gate = g_ref[0, rows, :].astype(F32)
        o_ref[0, rows, :] = (((on * sg_ref[...]) * (1.0 - lambda_init)) * (gate * _sigmoid(gate))).astype(BF16)

    return stage_scores, stage_probs, stage_out


def _attn_kernel(lam_ref, sg_ref, cos_ref, sin_ref, hn_ref, *rest, gw, cb, lambda_init):
    nsec = len(rest) - 6
    w_refs = rest[:nsec]
    sb_o, df_o, proj_ref, sb_vt, df_vt, qs_ref = rest[nsec:]

    @pl.when(pl.program_id(0) == 0)
    def _():
        proj_ref[...] = jnp.zeros_like(proj_ref)

    section = lambda sec: proj_ref.at[:, :, pl.ds(sec * HEAD_W, HEAD_W)]
    sb = _sb_stages(section(SEC_SB_Q), section(SEC_SB_K), section(SEC_SB_V), section(SEC_SB_G), sb_o, sb_vt,
                    gw=gw, cb=cb)
    df = _df_stages(lam_ref, sg_ref, section(SEC_DF_Q), section(SEC_DF_K), section(SEC_DF_V), section(SEC_DF_G),
                    df_o, df_vt, qs_ref, gw=gw, lambda_init=lambda_init)
    schedule = ((0, sb[0]), (0, df[0]), (1, sb[1]), (1, df[1]), (2, sb[2]), (2, df[2]), (3, sb[3]), (4, sb[4]))
    ngroups = proj_ref.shape[1] // gw
    npieces = PROJ_ROW_SPLIT * (proj_ref.shape[2] // (2 * HEAD_W))
    assert npieces + PROJ_FIRST_STEP <= ngroups + max(lag for lag, _ in schedule)
    next_pair = []
    for step in range(ngroups + max(lag for lag, _ in schedule)):
        for lag, stage in schedule:
            if 0 <= step - lag < ngroups:
                stage(step - lag)
        if PROJ_FIRST_STEP <= step < npieces + PROJ_FIRST_STEP:
            next_pair.append(_proj_piece(hn_ref, w_refs, cos_ref, sin_ref, step - PROJ_FIRST_STEP))
    for r0, nrows, c0, block in next_pair:
        proj_ref[0, r0:r0 + nrows, c0:c0 + 2 * HEAD_W] = block


def _attention(hn3d, w, lam_params, subln_g, lambda_init):
    b, s, d = hn3d.shape
    nheads = SEC_W // HEAD_W
    nsec = w.shape[1] // SEC_W
    head_cols = nsec * HEAD_W
    assert SB_HEADS == DF_HEADS == nheads and nsec == 8 and s % ATTN_GW == 0
    npairs = b * nheads
    cos, sin = _rope_tables(s)
    subln_g = subln_g.reshape(1, HEAD_W)
    whole = lambda a: pl.BlockSpec(a.shape, lambda t: (0,) * a.ndim, pipeline_mode=pl.Buffered(1))
    attended = lambda t: jnp.maximum(t - 1, 0)
    projected = lambda t: jnp.minimum(t, npairs - 1)
    out_block = pl.BlockSpec((1, s, HEAD_W), lambda t: (attended(t) // nheads, 0, attended(t) % nheads))
    out_shape = jax.ShapeDtypeStruct((b, s, nheads * HEAD_W), BF16)
    kern = functools.partial(_attn_kernel, gw=ATTN_GW, cb=SB_CUMSUM_ROWS, lambda_init=lambda_init)
    return pl.pallas_call(
        kern,
        grid=(npairs + 1,),
        in_specs=[
            whole(lam_params), whole(subln_g), whole(cos), whole(sin),
            pl.BlockSpec((1, s, d), lambda t: (projected(t) // nheads, 0, 0)),
        ] + [pl.BlockSpec((d, HEAD_W), lambda t, sec=sec: (0, sec * nheads + projected(t) % nheads))
             for sec in range(nsec)],
        out_specs=[out_block, out_block],
        out_shape=[out_shape, out_shape],
        scratch_shapes=[pltpu.VMEM((1, s, head_cols), BF16),
                        pltpu.VMEM((HEAD_W, s), BF16),
                        pltpu.VMEM((HEAD_W + BF16_SUBLANES, s), BF16),
                        pltpu.VMEM((2, s, HEAD_W), BF16)],
        compiler_params=pltpu.CompilerParams(dimension_semantics=("arbitrary",), vmem_limit_bytes=ATTN_VMEM_LIMIT),
        name="attention",
    )(lam_params, subln_g, cos, sin, hn3d, *([w] * nsec))


def _tail_kernel(x_ref, msb_ref, mdf_ref, p_ref, wo_sb_ref, wo_df_ref, gp_ref, wg_ref, wp_ref, gf_ref, o_ref,
                 *, final_norm):
    d = x_ref.shape[1]
    nchunks = d // TAIL_CHUNK
    cols = lambda n: slice(n * TAIL_CHUNK, (n + 1) * TAIL_CHUNK)

    def pipelined(matmul, epilogue):
        nxt = matmul(0)
        for n in range(nchunks):
            acc, nxt = nxt, (matmul(n + 1) if n + 1 < nchunks else None)
            epilogue(acc, n)

    def rms_scale(sum_sq):
        return lax.rsqrt(sum_sq * (1.0 / d) + NORM_EPS)

    h, sum_sq = [None] * nchunks, []

    def out_proj(n):
        return (jnp.dot(msb_ref[...], wo_sb_ref[:, cols(n)], preferred_element_type=F32)
                + jnp.dot(mdf_ref[...], wo_df_ref[:, cols(n)], preferred_element_type=F32))

    def residual(acc, n):
        h[n] = x_ref[:, cols(n)] + acc
        sum_sq.append(jnp.sum(h[n] * h[n], axis=-1, keepdims=True))

    pipelined(out_proj, residual)
    scale = rms_scale(sum(sum_sq))
    hn = jnp.concatenate([((h[n] * scale) * gp_ref[:, cols(n)]).astype(BF16) for n in range(nchunks)], axis=1)

    p_bf16 = p_ref[...].astype(BF16)
    sum_sq = []

    def gate_proj(n):
        return (jnp.dot(hn, wg_ref[:, cols(n)], preferred_element_type=F32),
                jnp.dot(p_bf16, wp_ref[:, cols(n)], preferred_element_type=F32))

    def gated_add(acc, n):
        gate, pp = acc
        h[n] = h[n] + _sigmoid(gate) * pp
        sum_sq.append(jnp.sum(h[n] * h[n], axis=-1, keepdims=True))

    pipelined(gate_proj, gated_add)
    if final_norm:
        scale = rms_scale(sum(sum_sq))
        for n in range(nchunks):
            o_ref[:, cols(n)] = (h[n] * scale) * gf_ref[:, cols(n)]
    else:
        for n in range(nchunks):
            o_ref[:, cols(n)] = h[n]


def _tail(x2d, msb, mdf, p2d, w_out, g_ple, w_gate, w_ple, g_final, final_norm):
    t, d = x2d.shape
    tm = TAIL_TM
    assert t % tm == 0 and d % TAIL_CHUNK == 0
    dsb, ddf, dp = msb.shape[1], mdf.shape[1], p2d.shape[1]
    wo_sb, wo_df = w_out[:dsb], w_out[dsb:]
    rows = lambda width: pl.BlockSpec((tm, width), lambda i: (i, 0))
    whole = lambda a: pl.BlockSpec(a.shape, lambda i: (0, 0), pipeline_mode=pl.Buffered(1))
    g_ple, g_final = g_ple.reshape(1, d), g_final.reshape(1, d)
    return pl.pallas_call(
        functools.partial(_tail_kernel, final_norm=final_norm),
        grid=(t // tm,),
        in_specs=[rows(d), rows(dsb), rows(ddf), rows(dp),
                  whole(wo_sb), whole(wo_df), whole(g_ple), whole(w_gate), whole(w_ple), whole(g_final)],
        out_specs=rows(d),
        out_shape=jax.ShapeDtypeStruct((t, d), F32),
        compiler_params=pltpu.CompilerParams(
            dimension_semantics=("arbitrary",), vmem_limit_bytes=VMEM_LIMIT),
        name="tail",
    )(x2d, msb, mdf, p2d, wo_sb, wo_df, g_ple, w_gate, w_ple, g_final)


def kernel(x, p, norm_mix_g, w_in, lambda_q1, lambda_k1, lambda_q2, lambda_k2, subln_g, w_out, norm_ple_g,
           w_ple_gate, w_ple_proj, norm_final_g):
    b, s, d = x.shape
    depth = w_in.shape[0]
    h = x.reshape(b * s, d)
    for i in range(depth):
        lambda_init = 0.8 - 0.6 * math.exp(-0.3 * i)
        hn = _norm(h, norm_mix_g[i]).reshape(b, s, d)
        lam_params = jnp.stack([lambda_q1[i], lambda_k1[i], lambda_q2[i], lambda_k2[i]])
        msb, mdf = _attention(hn, w_in[i].astype(BF16), lam_params, subln_g[i], lambda_init)
        msb, mdf = msb.reshape(b * s, -1), mdf.reshape(b * s, -1)
        h = _tail(h, msb, mdf, p[i].reshape(b * s, -1), w_out[i].astype(BF16), norm_ple_g[i],
                  w_ple_gate[i].astype(BF16), w_ple_proj[i].astype(BF16), norm_final_g,
                  final_norm=(i == depth - 1))
    return h.reshape(b, s, d)
```

```python
import functools
import math

import jax
import jax.numpy as jnp
from jax import lax
from jax.experimental import pallas as pl
from jax.experimental.pallas import tpu as pltpu

F32 = jnp.float32
BF16 = jnp.bfloat16

SB_HEADS = 8
DF_HEADS = 8
HEAD_W = 128
BF16_SUBLANES = 16
DF_QK_DIM = 64
ROPE_DIM = DF_QK_DIM // 4
ROPE_THETA = 500000.0
NORM_EPS = 1e-6
SUBLN_EPS = 1e-5

SEC_W = 1024
SEC_SB_Q, SEC_SB_K, SEC_SB_V, SEC_SB_G, SEC_DF_Q, SEC_DF_K, SEC_DF_V, SEC_DF_G = range(8)

NORM_TM = 1024
PROJ_ROW_SPLIT = 2
PROJ_FIRST_STEP = 2
ATTN_GW = 256
SB_CUMSUM_ROWS = 128
TAIL_TM = 512
TAIL_CHUNK = 256
VMEM_LIMIT = 56 * 1024 * 1024
ATTN_VMEM_LIMIT = 60 * 1024 * 1024

_NT = (((1,), (1,)), ((), ()))
_LOG2E = math.log2(math.e)
SB_Q_SCALE = HEAD_W ** -0.5 * _LOG2E
DF_Q_SCALE = DF_QK_DIM ** -0.5 * _LOG2E


def _sigmoid(x):
    return 1.0 / (1.0 + jnp.exp(-x))


def _norm_kernel(x_ref, g_ref, o_ref):
    x = x_ref[...]
    ms = jnp.mean(x * x, axis=-1, keepdims=True)
    o_ref[...] = ((x * lax.rsqrt(ms + NORM_EPS)) * g_ref[...]).astype(BF16)


def _norm(x2d, g):
    t, d = x2d.shape
    assert t % NORM_TM == 0
    return pl.pallas_call(
        _norm_kernel,
        grid=(t // NORM_TM,),
        in_specs=[pl.BlockSpec((NORM_TM, d), lambda i: (i, 0)), pl.BlockSpec((1, d), lambda i: (0, 0))],
        out_specs=pl.BlockSpec((NORM_TM, d), lambda i: (i, 0)),
        out_shape=jax.ShapeDtypeStruct((t, d), BF16),
        compiler_params=pltpu.CompilerParams(dimension_semantics=("arbitrary",), vmem_limit_bytes=VMEM_LIMIT),
        name="norm",
    )(x2d, g.reshape(1, d))


def _rope_tables(seq):
    half = ROPE_DIM // 2
    inv_freq = ROPE_THETA ** (-jnp.arange(0, ROPE_DIM, 2, dtype=F32) / ROPE_DIM)
    ang = jnp.arange(seq, dtype=F32)[:, None] * inv_freq[None, :]
    cos, sin = jnp.cos(ang), jnp.sin(ang)
    lane = jnp.arange(HEAD_W)
    within = lane % DF_QK_DIM
    freq = within % half
    first = within < half
    second = jnp.logical_and(within >= half, within < ROPE_DIM)
    cos_l, sin_l = cos[:, freq], sin[:, freq]
    c = jnp.where(jnp.logical_or(first, second)[None, :], cos_l, 1.0)
    sn = jnp.where(first[None, :], sin_l, jnp.where(second[None, :], -sin_l, 0.0))
    return c.astype(F32), sn.astype(F32)


def _proj_piece(hn_ref, w_refs, cos_ref, sin_ref, piece):
    nrows = hn_ref.shape[1] // PROJ_ROW_SPLIT
    r0 = (piece % PROJ_ROW_SPLIT) * nrows
    pair = piece // PROJ_ROW_SPLIT
    c0 = pair * 2 * HEAD_W
    w_pair = jnp.concatenate([w_refs[2 * pair][...], w_refs[2 * pair + 1][...]], axis=1)
    acc = jnp.dot(hn_ref[0, r0:r0 + nrows, :], w_pair, preferred_element_type=F32)
    blocks = []
    for k in range(2):
        sec = 2 * pair + k
        xh = acc[:, k * HEAD_W:(k + 1) * HEAD_W]
        if sec in (SEC_DF_Q, SEC_DF_K):
            c, s = cos_ref[r0:r0 + nrows, :], sin_ref[r0:r0 + nrows, :]
            if sec == SEC_DF_Q:
                c, s = c * DF_Q_SCALE, s * DF_Q_SCALE
            lane = lax.broadcasted_iota(jnp.int32, c.shape, 1)
            partner = jnp.where(lane % DF_QK_DIM < ROPE_DIM, lane ^ (ROPE_DIM // 2), lane)
            xh = xh * c + jnp.take_along_axis(xh * s, partner, axis=1)
        elif sec == SEC_SB_Q:
            xh = xh * SB_Q_SCALE
        blocks.append(xh.astype(BF16))
    return r0, nrows, c0, jnp.concatenate(blocks, axis=1)


def _fill_vt(v_ref, vt_ref, chunk):
    for lo in range(0, vt_ref.shape[1], chunk):
        vt_ref[:HEAD_W, lo:lo + chunk] = v_ref[0, lo:lo + chunk, :].astype(F32).T.astype(BF16)
    if vt_ref.shape[0] > HEAD_W:
        vt_ref[HEAD_W:, :] = jnp.ones((vt_ref.shape[0] - HEAD_W, vt_ref.shape[1]), BF16)


def _sb_stages(q_ref, k_ref, v_ref, g_ref, o_ref, vt_ref, *, gw, cb):
    _fill_vt(v_ref, vt_ref, gw)

    row = lax.broadcasted_iota(jnp.int32, (cb, 2 * cb), 0)
    col = lax.broadcasted_iota(jnp.int32, (cb, 2 * cb), 1) % cb
    suffix = jnp.where(col >= row, 1.0, 0.0).astype(BF16)
    key_pos = lax.broadcasted_iota(jnp.int32, (cb, gw), 0)
    qry_pos = lax.broadcasted_iota(jnp.int32, (cb, gw), 1)
    strict = [key_pos + r * cb < qry_pos for r in range(gw // cb)]

    per_group = gw // cb
    logits, splits, sums, weights = {}, {}, {}, {}

    def diag_mask(g, b):
        return strict[b - g * per_group] if b >= g * per_group else None

    def stage_logits(g):
        logits[g] = lax.dot_general(k_ref[0, :(g + 1) * gw, :], q_ref[0, g * gw:(g + 1) * gw, :], _NT,
                                    preferred_element_type=F32)

    def stage_split(g):
        y = logits[g]
        sp = jnp.maximum(y, 0.0) + jnp.log2(1.0 + jnp.exp2(-jnp.abs(y)))
        parts = []
        for b in range((g + 1) * per_group):
            sp_b = sp[b * cb:(b + 1) * cb]
            if diag_mask(g, b) is not None:
                sp_b = jnp.where(diag_mask(g, b), sp_b, 0.0)
            hi = sp_b.astype(BF16)
            lo = (sp_b - hi.astype(F32)).astype(BF16)
            parts.append(jnp.concatenate([hi, lo], axis=0))
        splits[g] = parts

    def stage_sums(g):
        sums[g] = [jnp.dot(suffix, part, preferred_element_type=F32) for part in splits.pop(g)]

    def stage_weights(g):
        y, block_sums = logits.pop(g), sums.pop(g)
        nblocks = (g + 1) * per_group
        w = [None] * nblocks
        carry = None
        for b in reversed(range(nblocks)):
            y_b = y[b * cb:(b + 1) * cb]
            w_b = jnp.exp2((y_b if carry is None else y_b - carry) - block_sums[b])
            if diag_mask(g, b) is not None:
                w_b = jnp.where(diag_mask(g, b), w_b, 0.0)
            w[b] = w_b.astype(BF16)
            total = block_sums[b][0:1, :]
            carry = total if carry is None else carry + total
        weights[g] = jnp.concatenate(w, axis=0)

    def stage_out(g):
        rows = slice(g * gw, (g + 1) * gw)
        out_t = jnp.dot(vt_ref[:, :(g + 1) * gw], weights.pop(g), preferred_element_type=F32)
        gate = g_ref[0, rows, :].astype(F32)
        o_ref[0, rows, :] = (out_t.T * (gate * _sigmoid(gate))).astype(BF16)

    return stage_logits, stage_split, stage_sums, stage_weights, stage_out


def _df_stages(lam_ref, sg_ref, q_ref, k_ref, v_ref, g_ref, o_ref, vt_ref, qs_ref, *, gw, lambda_init):
    seq = k_ref.shape[1]
    _fill_vt(v_ref, vt_ref, gw)

    q = q_ref[0]
    lane = lax.broadcasted_iota(jnp.int32, (seq, HEAD_W), 1)
    qs_ref[0] = jnp.where(lane < DF_QK_DIM, q, jnp.zeros_like(q))
    qs_ref[1] = jnp.where(lane >= DF_QK_DIM, q, jnp.zeros_like(q))

    lam_p = lam_ref[...]
    lam = (jnp.exp(jnp.sum(lam_p[0:1] * lam_p[1:2], axis=-1, keepdims=True))
           - jnp.exp(jnp.sum(lam_p[2:3] * lam_p[3:4], axis=-1, keepdims=True))
           + lambda_init)
    causal = lax.broadcasted_iota(jnp.int32, (gw, gw), 0) <= lax.broadcasted_iota(jnp.int32, (gw, gw), 1)
    scores, probs = {}, {}

    def stage_scores(g):
        keys = (g + 1) * gw
        for c in range(2):
            s = lax.dot_general(k_ref[0, :keys, :], qs_ref[c, g * gw:(g + 1) * gw, :], _NT,
                                preferred_element_type=F32)
            s_diag = jnp.where(causal, s[keys - gw:], -jnp.inf)
            s = s_diag if g == 0 else jnp.concatenate([s[:keys - gw], s_diag], axis=0)
            scores[g, c] = (s, jnp.max(s, axis=0, keepdims=True))

    def stage_probs(g):
        for c in range(2):
            s, m = scores.pop((g, c))
            probs[g, c] = jnp.exp2(s - m).astype(BF16)

    def stage_out(g):
        keys = (g + 1) * gw
        rows = slice(g * gw, (g + 1) * gw)
        maps = []
        for c in range(2):
            pv = jnp.dot(vt_ref[:, :keys], probs.pop((g, c)), preferred_element_type=F32)
            maps.append(pv[:HEAD_W] * (1.0 / pv[HEAD_W:HEAD_W + 1]))
        o = maps[0] - lam * maps[1]
        ms = jnp.mean(o * o, axis=0, keepdims=True)
        on = (o * lax.rsqrt(ms + SUBLN_EPS)).T
        gate = g_ref[0, rows, :].astype(F32)
        o_ref[0, rows, :] = (((on * sg_ref[...]) * (1.0 - lambda_init)) * (gate * _sigmoid(gate))).astype(BF16)

    return stage_scores, stage_probs, stage_out


def _attn_kernel(lam_ref, sg_ref, cos_ref, sin_ref, hn_ref, *rest, gw, cb, lambda_init):
    nsec = len(rest) - 6
    w_refs = rest[:nsec]
    sb_o, df_o, proj_ref, sb_vt, df_vt, qs_ref = rest[nsec:]

    @pl.when(pl.program_id(0) == 0)
    def _():
        proj_ref[...] = jnp.zeros_like(proj_ref)

    section = lambda sec: proj_ref.at[:, :, pl.ds(sec * HEAD_W, HEAD_W)]
    sb = _sb_stages(section(SEC_SB_Q), section(SEC_SB_K), section(SEC_SB_V), section(SEC_SB_G), sb_o, sb_vt,
                    gw=gw, cb=cb)
    df = _df_stages(lam_ref, sg_ref, section(SEC_DF_Q), section(SEC_DF_K), section(SEC_DF_V), section(SEC_DF_G),
                    df_o, df_vt, qs_ref, gw=gw, lambda_init=lambda_init)
    schedule = ((0, sb[0]), (0, df[0]), (1, sb[1]), (1, df[1]), (1, sb[2]), (2, df[2]), (2, sb[3]), (3, sb[4]))
    ngroups = proj_ref.shape[1] // gw
    npieces = PROJ_ROW_SPLIT * (proj_ref.shape[2] // (2 * HEAD_W))
    assert npieces + PROJ_FIRST_STEP <= ngroups + max(lag for lag, _ in schedule)
    next_pair = []
    for step in range(ngroups + max(lag for lag, _ in schedule)):
        for lag, stage in schedule:
            if 0 <= step - lag < ngroups:
                stage(step - lag)
        if PROJ_FIRST_STEP <= step < npieces + PROJ_FIRST_STEP:
            next_pair.append(_proj_piece(hn_ref, w_refs, cos_ref, sin_ref, step - PROJ_FIRST_STEP))
    for r0, nrows, c0, block in next_pair:
        proj_ref[0, r0:r0 + nrows, c0:c0 + 2 * HEAD_W] = block


def _attention(hn3d, w, lam_params, subln_g, lambda_init):
    b, s, d = hn3d.shape
    nheads = SEC_W // HEAD_W
    nsec = w.shape[1] // SEC_W
    head_cols = nsec * HEAD_W
    assert SB_HEADS == DF_HEADS == nheads and nsec == 8 and s % ATTN_GW == 0
    npairs = b * nheads
    cos, sin = _rope_tables(s)
    subln_g = subln_g.reshape(1, HEAD_W)
    whole = lambda a: pl.BlockSpec(a.shape, lambda t: (0,) * a.ndim, pipeline_mode=pl.Buffered(1))
    attended = lambda t: jnp.maximum(t - 1, 0)
    projected = lambda t: jnp.minimum(t, npairs - 1)
    out_block = pl.BlockSpec((1, s, HEAD_W), lambda t: (attended(t) // nheads, 0, attended(t) % nheads))
    out_shape = jax.ShapeDtypeStruct((b, s, nheads * HEAD_W), BF16)
    kern = functools.partial(_attn_kernel, gw=ATTN_GW, cb=SB_CUMSUM_ROWS, lambda_init=lambda_init)
    return pl.pallas_call(
        kern,
        grid=(npairs + 1,),
        in_specs=[
            whole(lam_params), whole(subln_g), whole(cos), whole(sin),
            pl.BlockSpec((1, s, d), lambda t: (projected(t) // nheads, 0, 0)),
        ] + [pl.BlockSpec((d, HEAD_W), lambda t, sec=sec: (0, sec * nheads + projected(t) % nheads))
             for sec in range(nsec)],
        out_specs=[out_block, out_block],
        out_shape=[out_shape, out_shape],
        scratch_shapes=[pltpu.VMEM((1, s, head_cols), BF16),
                        pltpu.VMEM((HEAD_W, s), BF16),
                        pltpu.VMEM((HEAD_W + BF16_SUBLANES, s), BF16),
                        pltpu.VMEM((2, s, HEAD_W), BF16)],
        compiler_params=pltpu.CompilerParams(dimension_semantics=("arbitrary",), vmem_limit_bytes=ATTN_VMEM_LIMIT),
        name="attention",
    )(lam_params, subln_g, cos, sin, hn3d, *([w] * nsec))


def _tail_kernel(x_ref, msb_ref, mdf_ref, p_ref, wo_sb_ref, wo_df_ref, gp_ref, wg_ref, wp_ref, gf_ref, o_ref,
                 *, final_norm):
    d = x_ref.shape[1]
    nchunks = d // TAIL_CHUNK
    cols = lambda n: slice(n * TAIL_CHUNK, (n + 1) * TAIL_CHUNK)

    def pipelined(matmul, epilogue):
        nxt = matmul(0)
        for n in range(nchunks):
            acc, nxt = nxt, (matmul(n + 1) if n + 1 < nchunks else None)
            epilogue(acc, n)

    def rms_scale(sum_sq):
        return lax.rsqrt(sum_sq * (1.0 / d) + NORM_EPS)

    h, sum_sq = [None] * nchunks, []

    def out_proj(n):
        return (jnp.dot(msb_ref[...], wo_sb_ref[:, cols(n)], preferred_element_type=F32)
                + jnp.dot(mdf_ref[...], wo_df_ref[:, cols(n)], preferred_element_type=F32))

    def residual(acc, n):
        h[n] = x_ref[:, cols(n)] + acc
        sum_sq.append(jnp.sum(h[n] * h[n], axis=-1, keepdims=True))

    pipelined(out_proj, residual)
    scale = rms_scale(sum(sum_sq))
    hn = jnp.concatenate([((h[n] * scale) * gp_ref[:, cols(n)]).astype(BF16) for n in range(nchunks)], axis=1)

    p_bf16 = p_ref[...].astype(BF16)
    sum_sq = []

    def gate_proj(n):
        return (jnp.dot(hn, wg_ref[:, cols(n)], preferred_element_type=F32),
                jnp.dot(p_bf16, wp_ref[:, cols(n)], preferred_element_type=F32))

    def gated_add(acc, n):
        gate, pp = acc
        h[n] = h[n] + _sigmoid(gate) * pp
        sum_sq.append(jnp.sum(h[n] * h[n], axis=-1, keepdims=True))

    pipelined(gate_proj, gated_add)
    if final_norm:
        scale = rms_scale(sum(sum_sq))
        for n in range(nchunks):
            o_ref[:, cols(n)] = (h[n] * scale) * gf_ref[:, cols(n)]
    else:
        for n in range(nchunks):
            o_ref[:, cols(n)] = h[n]


def _tail(x2d, msb, mdf, p2d, w_out, g_ple, w_gate, w_ple, g_final, final_norm):
    t, d = x2d.shape
    tm = TAIL_TM
    assert t % tm == 0 and d % TAIL_CHUNK == 0
    dsb, ddf, dp = msb.shape[1], mdf.shape[1], p2d.shape[1]
    wo_sb, wo_df = w_out[:dsb], w_out[dsb:]
    rows = lambda width: pl.BlockSpec((tm, width), lambda i: (i, 0))
    whole = lambda a: pl.BlockSpec(a.shape, lambda i: (0, 0), pipeline_mode=pl.Buffered(1))
    g_ple, g_final = g_ple.reshape(1, d), g_final.reshape(1, d)
    return pl.pallas_call(
        functools.partial(_tail_kernel, final_norm=final_norm),
        grid=(t // tm,),
        in_specs=[rows(d), rows(dsb), rows(ddf), rows(dp),
                  whole(wo_sb), whole(wo_df), whole(g_ple), whole(w_gate), whole(w_ple), whole(g_final)],
        out_specs=rows(d),
        out_shape=jax.ShapeDtypeStruct((t, d), F32),
        compiler_params=pltpu.CompilerParams(
            dimension_semantics=("arbitrary",), vmem_limit_bytes=VMEM_LIMIT),
        name="tail",
    )(x2d, msb, mdf, p2d, wo_sb, wo_df, g_ple, w_gate, w_ple, g_final)


def kernel(x, p, norm_mix_g, w_in, lambda_q1, lambda_k1, lambda_q2, lambda_k2, subln_g, w_out, norm_ple_g,
           w_ple_gate, w_ple_proj, norm_final_g):
    b, s, d = x.shape
    depth = w_in.shape[0]
    h = x.reshape(b * s, d)
    for i in range(depth):
        lambda_init = 0.8 - 0.6 * math.exp(-0.3 * i)
        hn = _norm(h, norm_mix_g[i]).reshape(b, s, d)
        lam_params = jnp.stack([lambda_q1[i], lambda_k1[i], lambda_q2[i], lambda_k2[i]])
        msb, mdf = _attention(hn, w_in[i].astype(BF16), lam_params, subln_g[i], lambda_init)
        msb, mdf = msb.reshape(b * s, -1), mdf.reshape(b * s, -1)
        h = _tail(h, msb, mdf, p[i].reshape(b * s, -1), w_out[i].astype(BF16), norm_ple_g[i],
                  w_ple_gate[i].astype(BF16), w_ple_proj[i].astype(BF16), norm_final_g,
                  final_norm=(i == depth - 1))
    return h.reshape(b, s, d)
```

```python
import functools
import math

import jax
import jax.numpy as jnp
from jax import lax
from jax.experimental import pallas as pl
from jax.experimental.pallas import tpu as pltpu

F32 = jnp.float32
BF16 = jnp.bfloat16

SB_HEADS = 8
DF_HEADS = 8
HEAD_W = 128
BF16_SUBLANES = 16
DF_QK_DIM = 64
ROPE_DIM = DF_QK_DIM // 4
ROPE_THETA = 500000.0
NORM_EPS = 1e-6
SUBLN_EPS = 1e-5

SEC_W = 1024
SEC_SB_Q, SEC_SB_K, SEC_SB_V, SEC_SB_G, SEC_DF_Q, SEC_DF_K, SEC_DF_V, SEC_DF_G = range(8)

NORM_TM = 1024
PROJ_ROW_SPLIT = 2
PROJ_FIRST_STEP = 2
ATTN_GW = 256
SB_CUMSUM_ROWS = 128
TAIL_TM = 512
TAIL_CHUNK = 256
VMEM_LIMIT = 56 * 1024 * 1024
ATTN_VMEM_LIMIT = 60 * 1024 * 1024

_NT = (((1,), (1,)), ((), ()))
_LOG2E = math.log2(math.e)
SB_Q_SCALE = HEAD_W ** -0.5 * _LOG2E
DF_Q_SCALE = DF_QK_DIM ** -0.5 * _LOG2E


def _sigmoid(x):
    return 1.0 / (1.0 + jnp.exp(-x))


def _norm_kernel(x_ref, g_ref, o_ref):
    x = x_ref[...]
    ms = jnp.mean(x * x, axis=-1, keepdims=True)
    o_ref[...] = ((x * lax.rsqrt(ms + NORM_EPS)) * g_ref[...]).astype(BF16)


def _norm(x2d, g):
    t, d = x2d.shape
    assert t % NORM_TM == 0
    return pl.pallas_call(
        _norm_kernel,
        grid=(t // NORM_TM,),
        in_specs=[pl.BlockSpec((NORM_TM, d), lambda i: (i, 0)), pl.BlockSpec((1, d), lambda i: (0, 0))],
        out_specs=pl.BlockSpec((NORM_TM, d), lambda i: (i, 0)),
        out_shape=jax.ShapeDtypeStruct((t, d), BF16),
        compiler_params=pltpu.CompilerParams(dimension_semantics=("arbitrary",), vmem_limit_bytes=VMEM_LIMIT),
        name="norm",
    )(x2d, g.reshape(1, d))


def _rope_tables(seq):
    half = ROPE_DIM // 2
    inv_freq = ROPE_THETA ** (-jnp.arange(0, ROPE_DIM, 2, dtype=F32) / ROPE_DIM)
    ang = jnp.arange(seq, dtype=F32)[:, None] * inv_freq[None, :]
    cos, sin = jnp.cos(ang), jnp.sin(ang)
    lane = jnp.arange(HEAD_W)
    within = lane % DF_QK_DIM
    freq = within % half
    first = within < half
    second = jnp.logical_and(within >= half, within < ROPE_DIM)
    cos_l, sin_l = cos[:, freq], sin[:, freq]
    c = jnp.where(jnp.logical_or(first, second)[None, :], cos_l, 1.0)
    sn = jnp.where(first[None, :], sin_l, jnp.where(second[None, :], -sin_l, 0.0))
    return c.astype(F32), sn.astype(F32)


def _proj_piece(hn_ref, w_refs, cos_ref, sin_ref, piece):
    nrows = hn_ref.shape[1] // PROJ_ROW_SPLIT
    npairs = len(w_refs) // 2
    r0 = (piece // npairs) * nrows
    pair = piece % npairs
    c0 = pair * 2 * HEAD_W
    w_pair = jnp.concatenate([w_refs[2 * pair][...], w_refs[2 * pair + 1][...]], axis=1)
    acc = jnp.dot(hn_ref[0, r0:r0 + nrows, :], w_pair, preferred_element_type=F32)
    blocks = []
    for k in range(2):
        sec = 2 * pair + k
        xh = acc[:, k * HEAD_W:(k + 1) * HEAD_W]
        if sec in (SEC_DF_Q, SEC_DF_K):
            c, s = cos_ref[r0:r0 + nrows, :], sin_ref[r0:r0 + nrows, :]
            if sec == SEC_DF_Q:
                c, s = c * DF_Q_SCALE, s * DF_Q_SCALE
            lane = lax.broadcasted_iota(jnp.int32, c.shape, 1)
            partner = jnp.where(lane % DF_QK_DIM < ROPE_DIM, lane ^ (ROPE_DIM // 2), lane)
            xh = xh * c + jnp.take_along_axis(xh * s, partner, axis=1)
        elif sec == SEC_SB_Q:
            xh = xh * SB_Q_SCALE
        blocks.append(xh.astype(BF16))
    return r0, nrows, c0, jnp.concatenate(blocks, axis=1)


def _fill_vt(v_ref, vt_ref, chunk):
    for lo in range(0, vt_ref.shape[1], chunk):
        vt_ref[:HEAD_W, lo:lo + chunk] = v_ref[0, lo:lo + chunk, :].astype(F32).T.astype(BF16)
    if vt_ref.shape[0] > HEAD_W:
        vt_ref[HEAD_W:, :] = jnp.ones((vt_ref.shape[0] - HEAD_W, vt_ref.shape[1]), BF16)


def _sb_stages(q_ref, k_ref, v_ref, g_ref, o_ref, vt_ref, *, gw, cb):
    _fill_vt(v_ref, vt_ref, gw)

    row = lax.broadcasted_iota(jnp.int32, (cb, 2 * cb), 0)
    col = lax.broadcasted_iota(jnp.int32, (cb, 2 * cb), 1) % cb
    suffix = jnp.where(col >= row, 1.0, 0.0).astype(BF16)
    key_pos = lax.broadcasted_iota(jnp.int32, (cb, gw), 0)
    qry_pos = lax.broadcasted_iota(jnp.int32, (cb, gw), 1)
    strict = [key_pos + r * cb < qry_pos for r in range(gw // cb)]

    per_group = gw // cb
    logits, splits, sums, weights = {}, {}, {}, {}

    def diag_mask(g, b):
        return strict[b - g * per_group] if b >= g * per_group else None

    def stage_logits(g):
        logits[g] = lax.dot_general(k_ref[0, :(g + 1) * gw, :], q_ref[0, g * gw:(g + 1) * gw, :], _NT,
                                    preferred_element_type=F32)

    def stage_split(g):
        y = logits[g]
        sp = jnp.maximum(y, 0.0) + jnp.log2(1.0 + jnp.exp2(-jnp.abs(y)))
        parts = []
        for b in range((g + 1) * per_group):
            sp_b = sp[b * cb:(b + 1) * cb]
            if diag_mask(g, b) is not None:
                sp_b = jnp.where(diag_mask(g, b), sp_b, 0.0)
            hi = sp_b.astype(BF16)
            lo = (sp_b - hi.astype(F32)).astype(BF16)
            parts.append(jnp.concatenate([hi, lo], axis=0))
        splits[g] = parts

    def stage_sums(g):
        sums[g] = [jnp.dot(suffix, part, preferred_element_type=F32) for part in splits.pop(g)]

    def stage_weights(g):
        y, block_sums = logits.pop(g), sums.pop(g)
        nblocks = (g + 1) * per_group
        w = [None] * nblocks
        carry = None
        for b in reversed(range(nblocks)):
            y_b = y[b * cb:(b + 1) * cb]
            w_b = jnp.exp2((y_b if carry is None else y_b - carry) - block_sums[b])
            if diag_mask(g, b) is not None:
                w_b = jnp.where(diag_mask(g, b), w_b, 0.0)
            w[b] = w_b.astype(BF16)
            total = block_sums[b][0:1, :]
            carry = total if carry is None else carry + total
        weights[g] = jnp.concatenate(w, axis=0)

    def stage_out(g):
        rows = slice(g * gw, (g + 1) * gw)
        out_t = jnp.dot(vt_ref[:, :(g + 1) * gw], weights.pop(g), preferred_element_type=F32)
        gate = g_ref[0, rows, :].astype(F32)
        o_ref[0, rows, :] = (out_t.T * (gate * _sigmoid(gate))).astype(BF16)

    return stage_logits, stage_split, stage_sums, stage_weights, stage_out


def _df_stages(lam_ref, sg_ref, q_ref, k_ref, v_ref, g_ref, o_ref, vt_ref, qs_ref, *, gw, lambda_init):
    seq = k_ref.shape[1]
    _fill_vt(v_ref, vt_ref, gw)

    q = q_ref[0]
    lane = lax.broadcasted_iota(jnp.int32, (seq, HEAD_W), 1)
    qs_ref[0] = jnp.where(lane < DF_QK_DIM, q, jnp.zeros_like(q))
    qs_ref[1] = jnp.where(lane >= DF_QK_DIM, q, jnp.zeros_like(q))

    lam_p = lam_ref[...]
    lam = (jnp.exp(jnp.sum(lam_p[0:1] * lam_p[1:2], axis=-1, keepdims=True))
           - jnp.exp(jnp.sum(lam_p[2:3] * lam_p[3:4], axis=-1, keepdims=True))
           + lambda_init)
    causal = lax.broadcasted_iota(jnp.int32, (gw, gw), 0) <= lax.broadcasted_iota(jnp.int32, (gw, gw), 1)
    scores, probs = {}, {}

    def stage_scores(g):
        keys = (g + 1) * gw
        for c in range(2):
            s = lax.dot_general(k_ref[0, :keys, :], qs_ref[c, g * gw:(g + 1) * gw, :], _NT,
                                preferred_element_type=F32)
            s_diag = jnp.where(causal, s[keys - gw:], -jnp.inf)
            s = s_diag if g == 0 else jnp.concatenate([s[:keys - gw], s_diag], axis=0)
            scores[g, c] = (s, jnp.max(s, axis=0, keepdims=True))

    def stage_probs(g):
        for c in range(2):
            s, m = scores.pop((g, c))
            probs[g, c] = jnp.exp2(s - m).astype(BF16)

    def stage_out(g):
        keys = (g + 1) * gw
        rows = slice(g * gw, (g + 1) * gw)
        maps = []
        for c in range(2):
            pv = jnp.dot(vt_ref[:, :keys], probs.pop((g, c)), preferred_element_type=F32)
            maps.append(pv[:HEAD_W] * (1.0 / pv[HEAD_W:HEAD_W + 1]))
        o = maps[0] - lam * maps[1]
        ms = jnp.mean(o * o, axis=0, keepdims=True)
        on = (o * lax.rsqrt(ms + SUBLN_EPS)).T
        gate = g_ref[0, rows, :].astype(F32)
        o_ref[0, rows, :] = (((on * sg_ref[...]) * (1.0 - lambda_init)) * (gate * _sigmoid(gate))).astype(BF16)

    return stage_scores, stage_probs, stage_out


def _attn_kernel(lam_ref, sg_ref, cos_ref, sin_ref, hn_ref, *rest, gw, cb, lambda_init):
    nsec = len(rest) - 6
    w_refs = rest[:nsec]
    sb_o, df_o, proj_ref, sb_vt, df_vt, qs_ref = rest[nsec:]

    @pl.when(pl.program_id(0) == 0)
    def _():
        proj_ref[...] = jnp.zeros_like(proj_ref)

    section = lambda sec: proj_ref.at[:, :, pl.ds(sec * HEAD_W, HEAD_W)]
    sb = _sb_stages(section(SEC_SB_Q), section(SEC_SB_K), section(SEC_SB_V), section(SEC_SB_G), sb_o, sb_vt,
                    gw=gw, cb=cb)
    df = _df_stages(lam_ref, sg_ref, section(SEC_DF_Q), section(SEC_DF_K), section(SEC_DF_V), section(SEC_DF_G),
                    df_o, df_vt, qs_ref, gw=gw, lambda_init=lambda_init)
    schedule = ((0, sb[0]), (0, df[0]), (1, sb[1]), (1, df[1]), (2, sb[2]), (2, df[2]), (3, sb[3]), (4, sb[4]))
    ngroups = proj_ref.shape[1] // gw
    npieces = PROJ_ROW_SPLIT * (proj_ref.shape[2] // (2 * HEAD_W))
    assert npieces + PROJ_FIRST_STEP <= ngroups + max(lag for lag, _ in schedule)
    next_pair = []
    for step in range(ngroups + max(lag for lag, _ in schedule)):
        for lag, stage in schedule:
            if 0 <= step - lag < ngroups:
                stage(step - lag)
        if PROJ_FIRST_STEP <= step < npieces + PROJ_FIRST_STEP:
            next_pair.append(_proj_piece(hn_ref, w_refs, cos_ref, sin_ref, step - PROJ_FIRST_STEP))
    for r0, nrows, c0, block in next_pair:
        proj_ref[0, r0:r0 + nrows, c0:c0 + 2 * HEAD_W] = block


def _attention(hn3d, w, lam_params, subln_g, lambda_init):
    b, s, d = hn3d.shape
    nheads = SEC_W // HEAD_W
    nsec = w.shape[1] // SEC_W
    head_cols = nsec * HEAD_W
    assert SB_HEADS == DF_HEADS == nheads and nsec == 8 and s % ATTN_GW == 0
    npairs = b * nheads
    cos, sin = _rope_tables(s)
    subln_g = subln_g.reshape(1, HEAD_W)
    whole = lambda a: pl.BlockSpec(a.shape, lambda t: (0,) * a.ndim, pipeline_mode=pl.Buffered(1))
    attended = lambda t: jnp.maximum(t - 1, 0)
    projected = lambda t: jnp.minimum(t, npairs - 1)
    out_block = pl.BlockSpec((1, s, HEAD_W), lambda t: (attended(t) // nheads, 0, attended(t) % nheads))
    out_shape = jax.ShapeDtypeStruct((b, s, nheads * HEAD_W), BF16)
    kern = functools.partial(_attn_kernel, gw=ATTN_GW, cb=SB_CUMSUM_ROWS, lambda_init=lambda_init)
    return pl.pallas_call(
        kern,
        grid=(npairs + 1,),
        in_specs=[
            whole(lam_params), whole(subln_g), whole(cos), whole(sin),
            pl.BlockSpec((1, s, d), lambda t: (projected(t) // nheads, 0, 0)),
        ] + [pl.BlockSpec((d, HEAD_W), lambda t, sec=sec: (0, sec * nheads + projected(t) % nheads))
             for sec in range(nsec)],
        out_specs=[out_block, out_block],
        out_shape=[out_shape, out_shape],
        scratch_shapes=[pltpu.VMEM((1, s, head_cols), BF16),
                        pltpu.VMEM((HEAD_W, s), BF16),
                        pltpu.VMEM((HEAD_W + BF16_SUBLANES, s), BF16),
                        pltpu.VMEM((2, s, HEAD_W), BF16)],
        compiler_params=pltpu.CompilerParams(dimension_semantics=("arbitrary",), vmem_limit_bytes=ATTN_VMEM_LIMIT),
        name="attention",
    )(lam_params, subln_g, cos, sin, hn3d, *([w] * nsec))


def _tail_kernel(x_ref, msb_ref, mdf_ref, p_ref, wo_sb_ref, wo_df_ref, gp_ref, wg_ref, wp_ref, gf_ref, o_ref,
                 *, final_norm):
    d = x_ref.shape[1]
    nchunks = d // TAIL_CHUNK
    cols = lambda n: slice(n * TAIL_CHUNK, (n + 1) * TAIL_CHUNK)

    def pipelined(matmul, epilogue):
        nxt = matmul(0)
        for n in range(nchunks):
            acc, nxt = nxt, (matmul(n + 1) if n + 1 < nchunks else None)
            epilogue(acc, n)

    def rms_scale(sum_sq):
        return lax.rsqrt(sum_sq * (1.0 / d) + NORM_EPS)

    h, sum_sq = [None] * nchunks, []

    def out_proj(n):
        return (jnp.dot(msb_ref[...], wo_sb_ref[:, cols(n)], preferred_element_type=F32)
                + jnp.dot(mdf_ref[...], wo_df_ref[:, cols(n)], preferred_element_type=F32))

    def residual(acc, n):
        h[n] = x_ref[:, cols(n)] + acc
        sum_sq.append(jnp.sum(h[n] * h[n], axis=-1, keepdims=True))

    pipelined(out_proj, residual)
    scale = rms_scale(sum(sum_sq))
    hn = jnp.concatenate([((h[n] * scale) * gp_ref[:, cols(n)]).astype(BF16) for n in range(nchunks)], axis=1)

    p_bf16 = p_ref[...].astype(BF16)
    sum_sq = []

    def gate_proj(n):
        return (jnp.dot(hn, wg_ref[:, cols(n)], preferred_element_type=F32),
                jnp.dot(p_bf16, wp_ref[:, cols(n)], preferred_element_type=F32))

    def gated_add(acc, n):
        gate, pp = acc
        h[n] = h[n] + _sigmoid(gate) * pp
        sum_sq.append(jnp.sum(h[n] * h[n], axis=-1, keepdims=True))

    pipelined(gate_proj, gated_add)
    if final_norm:
        scale = rms_scale(sum(sum_sq))
        for n in range(nchunks):
            o_ref[:, cols(n)] = (h[n] * scale) * gf_ref[:, cols(n)]
    else:
        for n in range(nchunks):
            o_ref[:, cols(n)] = h[n]


def _tail(x2d, msb, mdf, p2d, w_out, g_ple, w_gate, w_ple, g_final, final_norm):
    t, d = x2d.shape
    tm = TAIL_TM
    assert t % tm == 0 and d % TAIL_CHUNK == 0
    dsb, ddf, dp = msb.shape[1], mdf.shape[1], p2d.shape[1]
    wo_sb, wo_df = w_out[:dsb], w_out[dsb:]
    rows = lambda width: pl.BlockSpec((tm, width), lambda i: (i, 0))
    whole = lambda a: pl.BlockSpec(a.shape, lambda i: (0, 0), pipeline_mode=pl.Buffered(1))
    g_ple, g_final = g_ple.reshape(1, d), g_final.reshape(1, d)
    return pl.pallas_call(
        functools.partial(_tail_kernel, final_norm=final_norm),
        grid=(t // tm,),
        in_specs=[rows(d), rows(dsb), rows(ddf), rows(dp),
                  whole(wo_sb), whole(wo_df), whole(g_ple), whole(w_gate), whole(w_ple), whole(g_final)],
        out_specs=rows(d),
        out_shape=jax.ShapeDtypeStruct((t, d), F32),
        compiler_params=pltpu.CompilerParams(
            dimension_semantics=("arbitrary",), vmem_limit_bytes=VMEM_LIMIT),
        name="tail",
    )(x2d, msb, mdf, p2d, wo_sb, wo_df, g_ple, w_gate, w_ple, g_final)


def kernel(x, p, norm_mix_g, w_in, lambda_q1, lambda_k1, lambda_q2, lambda_k2, subln_g, w_out, norm_ple_g,
           w_ple_gate, w_ple_proj, norm_final_g):
    b, s, d = x.shape
    depth = w_in.shape[0]
    h = x.reshape(b * s, d)
    for i in range(depth):
        lambda_init = 0.8 - 0.6 * math.exp(-0.3 * i)
        hn = _norm(h, norm_mix_g[i]).reshape(b, s, d)
        lam_params = jnp.stack([lambda_q1[i], lambda_k1[i], lambda_q2[i], lambda_k2[i]])
        msb, mdf = _attention(hn, w_in[i].astype(BF16), lam_params, subln_g[i], lambda_init)
        msb, mdf = msb.reshape(b * s, -1), mdf.reshape(b * s, -1)
        h = _tail(h, msb, mdf, p[i].reshape(b * s, -1), w_out[i].astype(BF16), norm_ple_g[i],
                  w_ple_gate[i].astype(BF16), w_ple_proj[i].astype(BF16), norm_final_g,
                  final_norm=(i == depth - 1))
    return h.reshape(b, s, d)
```

```python
import functools
import math

import jax
import jax.numpy as jnp
from jax import lax
from jax.experimental import pallas as pl
from jax.experimental.pallas import tpu as pltpu

F32 = jnp.float32
BF16 = jnp.bfloat16

SB_HEADS = 8
DF_HEADS = 8
HEAD_W = 128
BF16_SUBLANES = 16
DF_QK_DIM = 64
ROPE_DIM = DF_QK_DIM // 4
ROPE_THETA = 500000.0
NORM_EPS = 1e-6
SUBLN_EPS = 1e-5

SEC_W = 1024
SEC_SB_Q, SEC_SB_K, SEC_SB_V, SEC_SB_G, SEC_DF_Q, SEC_DF_K, SEC_DF_V, SEC_DF_G = range(8)

NORM_TM = 1024
PROJ_ROW_SPLIT = 2
PROJ_STEPS = (2, 3, 4, 5, 6, 7, 10, 11)
ATTN_GW = 256
SB_CUMSUM_ROWS = 128
TAIL_TM = 512
TAIL_CHUNK = 256
VMEM_LIMIT = 56 * 1024 * 1024
ATTN_VMEM_LIMIT = 60 * 1024 * 1024

_NT = (((1,), (1,)), ((), ()))
_LOG2E = math.log2(math.e)
SB_Q_SCALE = HEAD_W ** -0.5 * _LOG2E
DF_Q_SCALE = DF_QK_DIM ** -0.5 * _LOG2E


def _sigmoid(x):
    return 1.0 / (1.0 + jnp.exp(-x))


def _norm_kernel(x_ref, g_ref, o_ref):
    x = x_ref[...]
    ms = jnp.mean(x * x, axis=-1, keepdims=True)
    o_ref[...] = ((x * lax.rsqrt(ms + NORM_EPS)) * g_ref[...]).astype(BF16)


def _norm(x2d, g):
    t, d = x2d.shape
    assert t % NORM_TM == 0
    return pl.pallas_call(
        _norm_kernel,
        grid=(t // NORM_TM,),
        in_specs=[pl.BlockSpec((NORM_TM, d), lambda i: (i, 0)), pl.BlockSpec((1, d), lambda i: (0, 0))],
        out_specs=pl.BlockSpec((NORM_TM, d), lambda i: (i, 0)),
        out_shape=jax.ShapeDtypeStruct((t, d), BF16),
        compiler_params=pltpu.CompilerParams(dimension_semantics=("arbitrary",), vmem_limit_bytes=VMEM_LIMIT),
        name="norm",
    )(x2d, g.reshape(1, d))


def _rope_tables(seq):
    half = ROPE_DIM // 2
    inv_freq = ROPE_THETA ** (-jnp.arange(0, ROPE_DIM, 2, dtype=F32) / ROPE_DIM)
    ang = jnp.arange(seq, dtype=F32)[:, None] * inv_freq[None, :]
    cos, sin = jnp.cos(ang), jnp.sin(ang)
    lane = jnp.arange(HEAD_W)
    within = lane % DF_QK_DIM
    freq = within % half
    first = within < half
    second = jnp.logical_and(within >= half, within < ROPE_DIM)
    cos_l, sin_l = cos[:, freq], sin[:, freq]
    c = jnp.where(jnp.logical_or(first, second)[None, :], cos_l, 1.0)
    sn = jnp.where(first[None, :], sin_l, jnp.where(second[None, :], -sin_l, 0.0))
    return c.astype(F32), sn.astype(F32)


def _proj_piece(hn_ref, w_refs, cos_ref, sin_ref, piece):
    nrows = hn_ref.shape[1] // PROJ_ROW_SPLIT
    r0 = (piece % PROJ_ROW_SPLIT) * nrows
    pair = piece // PROJ_ROW_SPLIT
    c0 = pair * 2 * HEAD_W
    w_pair = jnp.concatenate([w_refs[2 * pair][...], w_refs[2 * pair + 1][...]], axis=1)
    acc = jnp.dot(hn_ref[0, r0:r0 + nrows, :], w_pair, preferred_element_type=F32)
    blocks = []
    for k in range(2):
        sec = 2 * pair + k
        xh = acc[:, k * HEAD_W:(k + 1) * HEAD_W]
        if sec in (SEC_DF_Q, SEC_DF_K):
            c, s = cos_ref[r0:r0 + nrows, :], sin_ref[r0:r0 + nrows, :]
            if sec == SEC_DF_Q:
                c, s = c * DF_Q_SCALE, s * DF_Q_SCALE
            lane = lax.broadcasted_iota(jnp.int32, c.shape, 1)
            partner = jnp.where(lane % DF_QK_DIM < ROPE_DIM, lane ^ (ROPE_DIM // 2), lane)
            xh = xh * c + jnp.take_along_axis(xh * s, partner, axis=1)
        elif sec == SEC_SB_Q:
            xh = xh * SB_Q_SCALE
        blocks.append(xh.astype(BF16))
    return r0, nrows, c0, jnp.concatenate(blocks, axis=1)


def _fill_vt(v_ref, vt_ref, chunk):
    for lo in range(0, vt_ref.shape[1], chunk):
        vt_ref[:HEAD_W, lo:lo + chunk] = v_ref[0, lo:lo + chunk, :].astype(F32).T.astype(BF16)
    if vt_ref.shape[0] > HEAD_W:
        vt_ref[HEAD_W:, :] = jnp.ones((vt_ref.shape[0] - HEAD_W, vt_ref.shape[1]), BF16)


def _sb_stages(q_ref, k_ref, v_ref, g_ref, o_ref, vt_ref, *, gw, cb):
    _fill_vt(v_ref, vt_ref, gw)

    row = lax.broadcasted_iota(jnp.int32, (cb, 2 * cb), 0)
    col = lax.broadcasted_iota(jnp.int32, (cb, 2 * cb), 1) % cb
    suffix = jnp.where(col >= row, 1.0, 0.0).astype(BF16)
    key_pos = lax.broadcasted_iota(jnp.int32, (cb, gw), 0)
    qry_pos = lax.broadcasted_iota(jnp.int32, (cb, gw), 1)
    strict = [key_pos + r * cb < qry_pos for r in range(gw // cb)]

    per_group = gw // cb
    logits, splits, sums, weights = {}, {}, {}, {}

    def diag_mask(g, b):
        return strict[b - g * per_group] if b >= g * per_group else None

    def stage_logits(g):
        logits[g] = lax.dot_general(k_ref[0, :(g + 1) * gw, :], q_ref[0, g * gw:(g + 1) * gw, :], _NT,
                                    preferred_element_type=F32)

    def stage_split(g):
        y = logits[g]
        sp = jnp.maximum(y, 0.0) + jnp.log2(1.0 + jnp.exp2(-jnp.abs(y)))
        parts = []
        for b in range((g + 1) * per_group):
            sp_b = sp[b * cb:(b + 1) * cb]
            if diag_mask(g, b) is not None:
                sp_b = jnp.where(diag_mask(g, b), sp_b, 0.0)
            hi = sp_b.astype(BF16)
            lo = (sp_b - hi.astype(F32)).astype(BF16)
            parts.append(jnp.concatenate([hi, lo], axis=0))
        splits[g] = parts

    def stage_sums(g):
        sums[g] = [jnp.dot(suffix, part, preferred_element_type=F32) for part in splits.pop(g)]

    def stage_weights(g):
        y, block_sums = logits.pop(g), sums.pop(g)
        nblocks = (g + 1) * per_group
        w = [None] * nblocks
        carry = None
        for b in reversed(range(nblocks)):
            y_b = y[b * cb:(b + 1) * cb]
            w_b = jnp.exp2((y_b if carry is None else y_b - carry) - block_sums[b])
            if diag_mask(g, b) is not None:
                w_b = jnp.where(diag_mask(g, b), w_b, 0.0)
            w[b] = w_b.astype(BF16)
            total = block_sums[b][0:1, :]
            carry = total if carry is None else carry + total
        weights[g] = jnp.concatenate(w, axis=0)

    def stage_out(g):
        rows = slice(g * gw, (g + 1) * gw)
        out_t = jnp.dot(vt_ref[:, :(g + 1) * gw], weights.pop(g), preferred_element_type=F32)
        gate = g_ref[0, rows, :].astype(F32)
        o_ref[0, rows, :] = (out_t.T * (gate * _sigmoid(gate))).astype(BF16)

    return stage_logits, stage_split, stage_sums, stage_weights, stage_out


def _df_stages(lam_ref, sg_ref, q_ref, k_ref, v_ref, g_ref, o_ref, vt_ref, qs_ref, *, gw, lambda_init):
    seq = k_ref.shape[1]
    _fill_vt(v_ref, vt_ref, gw)

    q = q_ref[0]
    lane = lax.broadcasted_iota(jnp.int32, (seq, HEAD_W), 1)
    qs_ref[0] = jnp.where(lane < DF_QK_DIM, q, jnp.zeros_like(q))
    qs_ref[1] = jnp.where(lane >= DF_QK_DIM, q, jnp.zeros_like(q))

    lam_p = lam_ref[...]
    lam = (jnp.exp(jnp.sum(lam_p[0:1] * lam_p[1:2], axis=-1, keepdims=True))
           - jnp.exp(jnp.sum(lam_p[2:3] * lam_p[3:4], axis=-1, keepdims=True))
           + lambda_init)
    causal = lax.broadcasted_iota(jnp.int32, (gw, gw), 0) <= lax.broadcasted_iota(jnp.int32, (gw, gw), 1)
    scores, probs = {}, {}

    def stage_scores(g):
        keys = (g + 1) * gw
        for c in range(2):
            s = lax.dot_general(k_ref[0, :keys, :], qs_ref[c, g * gw:(g + 1) * gw, :], _NT,
                                preferred_element_type=F32)
            s_diag = jnp.where(causal, s[keys - gw:], -jnp.inf)
            s = s_diag if g == 0 else jnp.concatenate([s[:keys - gw], s_diag], axis=0)
            scores[g, c] = (s, jnp.max(s, axis=0, keepdims=True))

    def stage_probs(g):
        for c in range(2):
            s, m = scores.pop((g, c))
            probs[g, c] = jnp.exp2(s - m).astype(BF16)

    def stage_out(g):
        keys = (g + 1) * gw
        rows = slice(g * gw, (g + 1) * gw)
        maps = []
        for c in range(2):
            pv = jnp.dot(vt_ref[:, :keys], probs.pop((g, c)), preferred_element_type=F32)
            maps.append(pv[:HEAD_W] * (1.0 / pv[HEAD_W:HEAD_W + 1]))
        o = maps[0] - lam * maps[1]
        ms = jnp.mean(o * o, axis=0, keepdims=True)
        on = (o * lax.rsqrt(ms + SUBLN_EPS)).T
        gate = g_ref[0, rows, :].astype(F32)
        o_ref[0, rows, :] = (((on * sg_ref[...]) * (1.0 - lambda_init)) * (gate * _sigmoid(gate))).astype(BF16)

    return stage_scores, stage_probs, stage_out


def _attn_kernel(lam_ref, sg_ref, cos_ref, sin_ref, hn_ref, *rest, gw, cb, lambda_init):
    nsec = len(rest) - 6
    w_refs = rest[:nsec]
    sb_o, df_o, proj_ref, sb_vt, df_vt, qs_ref = rest[nsec:]

    @pl.when(pl.program_id(0) == 0)
    def _():
        proj_ref[...] = jnp.zeros_like(proj_ref)

    section = lambda sec: proj_ref.at[:, :, pl.ds(sec * HEAD_W, HEAD_W)]
    sb = _sb_stages(section(SEC_SB_Q), section(SEC_SB_K), section(SEC_SB_V), section(SEC_SB_G), sb_o, sb_vt,
                    gw=gw, cb=cb)
    df = _df_stages(lam_ref, sg_ref, section(SEC_DF_Q), section(SEC_DF_K), section(SEC_DF_V), section(SEC_DF_G),
                    df_o, df_vt, qs_ref, gw=gw, lambda_init=lambda_init)
    schedule = ((0, sb[0]), (0, df[0]), (1, sb[1]), (1, df[1]), (2, sb[2]), (2, df[2]), (3, sb[3]), (4, sb[4]))
    ngroups = proj_ref.shape[1] // gw
    npieces = PROJ_ROW_SPLIT * (proj_ref.shape[2] // (2 * HEAD_W))
    assert npieces == len(PROJ_STEPS) and max(PROJ_STEPS) < ngroups + max(lag for lag, _ in schedule)
    next_pair = []
    for step in range(ngroups + max(lag for lag, _ in schedule)):
        for lag, stage in schedule:
            if 0 <= step - lag < ngroups:
                stage(step - lag)
        if step in PROJ_STEPS:
            next_pair.append(_proj_piece(hn_ref, w_refs, cos_ref, sin_ref, PROJ_STEPS.index(step)))
    for r0, nrows, c0, block in next_pair:
        proj_ref[0, r0:r0 + nrows, c0:c0 + 2 * HEAD_W] = block


def _attention(hn3d, w, lam_params, subln_g, lambda_init):
    b, s, d = hn3d.shape
    nheads = SEC_W // HEAD_W
    nsec = w.shape[1] // SEC_W
    head_cols = nsec * HEAD_W
    assert SB_HEADS == DF_HEADS == nheads and nsec == 8 and s % ATTN_GW == 0
    npairs = b * nheads
    cos, sin = _rope_tables(s)
    subln_g = subln_g.reshape(1, HEAD_W)
    whole = lambda a: pl.BlockSpec(a.shape, lambda t: (0,) * a.ndim, pipeline_mode=pl.Buffered(1))
    attended = lambda t: jnp.maximum(t - 1, 0)
    projected = lambda t: jnp.minimum(t, npairs - 1)
    out_block = pl.BlockSpec((1, s, HEAD_W), lambda t: (attended(t) // nheads, 0, attended(t) % nheads))
    out_shape = jax.ShapeDtypeStruct((b, s, nheads * HEAD_W), BF16)
    kern = functools.partial(_attn_kernel, gw=ATTN_GW, cb=SB_CUMSUM_ROWS, lambda_init=lambda_init)
    return pl.pallas_call(
        kern,
        grid=(npairs + 1,),
        in_specs=[
            whole(lam_params), whole(subln_g), whole(cos), whole(sin),
            pl.BlockSpec((1, s, d), lambda t: (projected(t) // nheads, 0, 0)),
        ] + [pl.BlockSpec((d, HEAD_W), lambda t, sec=sec: (0, sec * nheads + projected(t) % nheads))
             for sec in range(nsec)],
        out_specs=[out_block, out_block],
        out_shape=[out_shape, out_shape],
        scratch_shapes=[pltpu.VMEM((1, s, head_cols), BF16),
                        pltpu.VMEM((HEAD_W, s), BF16),
                        pltpu.VMEM((HEAD_W + BF16_SUBLANES, s), BF16),
                        pltpu.VMEM((2, s, HEAD_W), BF16)],
        compiler_params=pltpu.CompilerParams(dimension_semantics=("arbitrary",), vmem_limit_bytes=ATTN_VMEM_LIMIT),
        name="attention",
    )(lam_params, subln_g, cos, sin, hn3d, *([w] * nsec))


def _tail_kernel(x_ref, msb_ref, mdf_ref, p_ref, wo_sb_ref, wo_df_ref, gp_ref, wg_ref, wp_ref, gf_ref, o_ref,
                 *, final_norm):
    d = x_ref.shape[1]
    nchunks = d // TAIL_CHUNK
    cols = lambda n: slice(n * TAIL_CHUNK, (n + 1) * TAIL_CHUNK)

    def pipelined(matmul, epilogue):
        nxt = matmul(0)
        for n in range(nchunks):
            acc, nxt = nxt, (matmul(n + 1) if n + 1 < nchunks else None)
            epilogue(acc, n)

    def rms_scale(sum_sq):
        return lax.rsqrt(sum_sq * (1.0 / d) + NORM_EPS)

    h, sum_sq = [None] * nchunks, []

    def out_proj(n):
        return (jnp.dot(msb_ref[...], wo_sb_ref[:, cols(n)], preferred_element_type=F32)
                + jnp.dot(mdf_ref[...], wo_df_ref[:, cols(n)], preferred_element_type=F32))

    def residual(acc, n):
        h[n] = x_ref[:, cols(n)] + acc
        sum_sq.append(jnp.sum(h[n] * h[n], axis=-1, keepdims=True))

    pipelined(out_proj, residual)
    scale = rms_scale(sum(sum_sq))
    hn = jnp.concatenate([((h[n] * scale) * gp_ref[:, cols(n)]).astype(BF16) for n in range(nchunks)], axis=1)

    p_bf16 = p_ref[...].astype(BF16)
    sum_sq = []

    def gate_proj(n):
        return (jnp.dot(hn, wg_ref[:, cols(n)], preferred_element_type=F32),
                jnp.dot(p_bf16, wp_ref[:, cols(n)], preferred_element_type=F32))

    def gated_add(acc, n):
        gate, pp = acc
        h[n] = h[n] + _sigmoid(gate) * pp
        sum_sq.append(jnp.sum(h[n] * h[n], axis=-1, keepdims=True))

    pipelined(gate_proj, gated_add)
    if final_norm:
        scale = rms_scale(sum(sum_sq))
        for n in range(nchunks):
            o_ref[:, cols(n)] = (h[n] * scale) * gf_ref[:, cols(n)]
    else:
        for n in range(nchunks):
            o_ref[:, cols(n)] = h[n]


def _tail(x2d, msb, mdf, p2d, w_out, g_ple, w_gate, w_ple, g_final, final_norm):
    t, d = x2d.shape
    tm = TAIL_TM
    assert t % tm == 0 and d % TAIL_CHUNK == 0
    dsb, ddf, dp = msb.shape[1], mdf.shape[1], p2d.shape[1]
    wo_sb, wo_df = w_out[:dsb], w_out[dsb:]
    rows = lambda width: pl.BlockSpec((tm, width), lambda i: (i, 0))
    whole = lambda a: pl.BlockSpec(a.shape, lambda i: (0, 0), pipeline_mode=pl.Buffered(1))
    g_ple, g_final = g_ple.reshape(1, d), g_final.reshape(1, d)
    return pl.pallas_call(
        functools.partial(_tail_kernel, final_norm=final_norm),
        grid=(t // tm,),
        in_specs=[rows(d), rows(dsb), rows(ddf), rows(dp),
                  whole(wo_sb), whole(wo_df), whole(g_ple), whole(w_gate), whole(w_ple), whole(g_final)],
        out_specs=rows(d),
        out_shape=jax.ShapeDtypeStruct((t, d), F32),
        compiler_params=pltpu.CompilerParams(
            dimension_semantics=("arbitrary",), vmem_limit_bytes=VMEM_LIMIT),
        name="tail",
    )(x2d, msb, mdf, p2d, wo_sb, wo_df, g_ple, w_gate, w_ple, g_final)


def kernel(x, p, norm_mix_g, w_in, lambda_q1, lambda_k1, lambda_q2, lambda_k2, subln_g, w_out, norm_ple_g,
           w_ple_gate, w_ple_proj, norm_final_g):
    b, s, d = x.shape
    depth = w_in.shape[0]
    h = x.reshape(b * s, d)
    for i in range(depth):
        lambda_init = 0.8 - 0.6 * math.exp(-0.3 * i)
        hn = _norm(h, norm_mix_g[i]).reshape(b, s, d)
        lam_params = jnp.stack([lambda_q1[i], lambda_k1[i], lambda_q2[i], lambda_k2[i]])
        msb, mdf = _attention(hn, w_in[i].astype(BF16), lam_params, subln_g[i], lambda_init)
        msb, mdf = msb.reshape(b * s, -1), mdf.reshape(b * s, -1)
        h = _tail(h, msb, mdf, p[i].reshape(b * s, -1), w_out[i].astype(BF16), norm_ple_g[i],
                  w_ple_gate[i].astype(BF16), w_ple_proj[i].astype(BF16), norm_final_g,
                  final_norm=(i == depth - 1))
    return h.reshape(b, s, d)
```

```python
import functools
import math

import jax
import jax.numpy as jnp
from jax import lax
from jax.experimental import pallas as pl
from jax.experimental.pallas import tpu as pltpu

F32 = jnp.float32
BF16 = jnp.bfloat16

SB_HEADS = 8
DF_HEADS = 8
HEAD_W = 128
BF16_SUBLANES = 16
DF_QK_DIM = 64
ROPE_DIM = DF_QK_DIM // 4
ROPE_THETA = 500000.0
NORM_EPS = 1e-6
SUBLN_EPS = 1e-5

SEC_W = 1024
SEC_SB_Q, SEC_SB_K, SEC_SB_V, SEC_SB_G, SEC_DF_Q, SEC_DF_K, SEC_DF_V, SEC_DF_G = range(8)

NORM_TM = 1024
PROJ_ROW_SPLIT = 2
PROJ_FIRST_STEP = 2
ATTN_GW = 256
SB_CUMSUM_ROWS = 128
TAIL_TM = 512
TAIL_CHUNK = 256
VMEM_LIMIT = 56 * 1024 * 1024
ATTN_VMEM_LIMIT = 60 * 1024 * 1024

_NT = (((1,), (1,)), ((), ()))
_LOG2E = math.log2(math.e)
SB_Q_SCALE = HEAD_W ** -0.5 * _LOG2E
DF_Q_SCALE = DF_QK_DIM ** -0.5 * _LOG2E


def _sigmoid(x):
    return 1.0 / (1.0 + jnp.exp(-x))


def _norm_kernel(x_ref, g_ref, o_ref):
    x = x_ref[...]
    ms = jnp.mean(x * x, axis=-1, keepdims=True)
    o_ref[...] = ((x * lax.rsqrt(ms + NORM_EPS)) * g_ref[...]).astype(BF16)


def _norm(x2d, g):
    t, d = x2d.shape
    assert t % NORM_TM == 0
    return pl.pallas_call(
        _norm_kernel,
        grid=(t // NORM_TM,),
        in_specs=[pl.BlockSpec((NORM_TM, d), lambda i: (i, 0)), pl.BlockSpec((1, d), lambda i: (0, 0))],
        out_specs=pl.BlockSpec((NORM_TM, d), lambda i: (i, 0)),
        out_shape=jax.ShapeDtypeStruct((t, d), BF16),
        compiler_params=pltpu.CompilerParams(dimension_semantics=("arbitrary",), vmem_limit_bytes=VMEM_LIMIT),
        name="norm",
    )(x2d, g.reshape(1, d))


def _rope_tables(seq):
    half = ROPE_DIM // 2
    inv_freq = ROPE_THETA ** (-jnp.arange(0, ROPE_DIM, 2, dtype=F32) / ROPE_DIM)
    ang = jnp.arange(seq, dtype=F32)[:, None] * inv_freq[None, :]
    cos, sin = jnp.cos(ang), jnp.sin(ang)
    lane = jnp.arange(HEAD_W)
    within = lane % DF_QK_DIM
    freq = within % half
    first = within < half
    second = jnp.logical_and(within >= half, within < ROPE_DIM)
    cos_l, sin_l = cos[:, freq], sin[:, freq]
    c = jnp.where(jnp.logical_or(first, second)[None, :], cos_l, 1.0)
    sn = jnp.where(first[None, :], sin_l, jnp.where(second[None, :], -sin_l, 0.0))
    return c.astype(F32), sn.astype(F32)


def _proj_piece(hn_ref, w_refs, cos_ref, sin_ref, piece):
    nrows = hn_ref.shape[1] // PROJ_ROW_SPLIT
    r0 = (piece % PROJ_ROW_SPLIT) * nrows
    pair = piece // PROJ_ROW_SPLIT
    c0 = pair * 2 * HEAD_W
    w_pair = jnp.concatenate([w_refs[2 * pair][...], w_refs[2 * pair + 1][...]], axis=1).astype(BF16)
    acc = jnp.dot(hn_ref[0, r0:r0 + nrows, :], w_pair, preferred_element_type=F32)
    blocks = []
    for k in range(2):
        sec = 2 * pair + k
        xh = acc[:, k * HEAD_W:(k + 1) * HEAD_W]
        if sec in (SEC_DF_Q, SEC_DF_K):
            c, s = cos_ref[r0:r0 + nrows, :], sin_ref[r0:r0 + nrows, :]
            if sec == SEC_DF_Q:
                c, s = c * DF_Q_SCALE, s * DF_Q_SCALE
            lane = lax.broadcasted_iota(jnp.int32, c.shape, 1)
            partner = jnp.where(lane % DF_QK_DIM < ROPE_DIM, lane ^ (ROPE_DIM // 2), lane)
            xh = xh * c + jnp.take_along_axis(xh * s, partner, axis=1)
        elif sec == SEC_SB_Q:
            xh = xh * SB_Q_SCALE
        blocks.append(xh.astype(BF16))
    return r0, nrows, c0, jnp.concatenate(blocks, axis=1)


def _fill_vt(v_ref, vt_ref, chunk):
    for lo in range(0, vt_ref.shape[1], chunk):
        vt_ref[:HEAD_W, lo:lo + chunk] = v_ref[0, lo:lo + chunk, :].astype(F32).T.astype(BF16)
    if vt_ref.shape[0] > HEAD_W:
        vt_ref[HEAD_W:, :] = jnp.ones((vt_ref.shape[0] - HEAD_W, vt_ref.shape[1]), BF16)


def _sb_stages(q_ref, k_ref, v_ref, g_ref, o_ref, vt_ref, *, gw, cb):
    _fill_vt(v_ref, vt_ref, gw)

    row = lax.broadcasted_iota(jnp.int32, (cb, 2 * cb), 0)
    col = lax.broadcasted_iota(jnp.int32, (cb, 2 * cb), 1) % cb
    suffix = jnp.where(col >= row, 1.0, 0.0).astype(BF16)
    key_pos = lax.broadcasted_iota(jnp.int32, (cb, gw), 0)
    qry_pos = lax.broadcasted_iota(jnp.int32, (cb, gw), 1)
    strict = [key_pos + r * cb < qry_pos for r in range(gw // cb)]

    per_group = gw // cb
    logits, splits, sums, weights = {}, {}, {}, {}

    def diag_mask(g, b):
        return strict[b - g * per_group] if b >= g * per_group else None

    def stage_logits(g):
        logits[g] = lax.dot_general(k_ref[0, :(g + 1) * gw, :], q_ref[0, g * gw:(g + 1) * gw, :], _NT,
                                    preferred_element_type=F32)

    def stage_split(g):
        y = logits[g]
        sp = jnp.maximum(y, 0.0) + jnp.log2(1.0 + jnp.exp2(-jnp.abs(y)))
        parts = []
        for b in range((g + 1) * per_group):
            sp_b = sp[b * cb:(b + 1) * cb]
            if diag_mask(g, b) is not None:
                sp_b = jnp.where(diag_mask(g, b), sp_b, 0.0)
            hi = sp_b.astype(BF16)
            lo = (sp_b - hi.astype(F32)).astype(BF16)
            parts.append(jnp.concatenate([hi, lo], axis=0))
        splits[g] = parts

    def stage_sums(g):
        sums[g] = [jnp.dot(suffix, part, preferred_element_type=F32) for part in splits.pop(g)]

    def stage_weights(g):
        y, block_sums = logits.pop(g), sums.pop(g)
        nblocks = (g + 1) * per_group
        w = [None] * nblocks
        carry = None
        for b in reversed(range(nblocks)):
            y_b = y[b * cb:(b + 1) * cb]
            w_b = jnp.exp2((y_b if carry is None else y_b - carry) - block_sums[b])
            if diag_mask(g, b) is not None:
                w_b = jnp.where(diag_mask(g, b), w_b, 0.0)
            w[b] = w_b.astype(BF16)
            total = block_sums[b][0:1, :]
            carry = total if carry is None else carry + total
        weights[g] = jnp.concatenate(w, axis=0)

    def stage_out(g):
        rows = slice(g * gw, (g + 1) * gw)
        out_t = jnp.dot(vt_ref[:, :(g + 1) * gw], weights.pop(g), preferred_element_type=F32)
        gate = g_ref[0, rows, :].astype(F32)
        o_ref[0, rows, :] = (out_t.T * (gate * _sigmoid(gate))).astype(BF16)

    return stage_logits, stage_split, stage_sums, stage_weights, stage_out


def _df_stages(lam_ref, sg_ref, q_ref, k_ref, v_ref, g_ref, o_ref, vt_ref, qs_ref, *, gw, lambda_init):
    seq = k_ref.shape[1]
    _fill_vt(v_ref, vt_ref, gw)

    q = q_ref[0]
    lane = lax.broadcasted_iota(jnp.int32, (seq, HEAD_W), 1)
    qs_ref[0] = jnp.where(lane < DF_QK_DIM, q, jnp.zeros_like(q))
    qs_ref[1] = jnp.where(lane >= DF_QK_DIM, q, jnp.zeros_like(q))

    lam_p = lam_ref[...]
    lam = (jnp.exp(jnp.sum(lam_p[0:1] * lam_p[1:2], axis=-1, keepdims=True))
           - jnp.exp(jnp.sum(lam_p[2:3] * lam_p[3:4], axis=-1, keepdims=True))
           + lambda_init)
    causal = lax.broadcasted_iota(jnp.int32, (gw, gw), 0) <= lax.broadcasted_iota(jnp.int32, (gw, gw), 1)
    scores, probs = {}, {}

    def stage_scores(g):
        keys = (g + 1) * gw
        for c in range(2):
            s = lax.dot_general(k_ref[0, :keys, :], qs_ref[c, g * gw:(g + 1) * gw, :], _NT,
                                preferred_element_type=F32)
            s_diag = jnp.where(causal, s[keys - gw:], -jnp.inf)
            s = s_diag if g == 0 else jnp.concatenate([s[:keys - gw], s_diag], axis=0)
            scores[g, c] = (s, jnp.max(s, axis=0, keepdims=True))

    def stage_probs(g):
        for c in range(2):
            s, m = scores.pop((g, c))
            probs[g, c] = jnp.exp2(s - m).astype(BF16)

    def stage_out(g):
        keys = (g + 1) * gw
        rows = slice(g * gw, (g + 1) * gw)
        maps = []
        for c in range(2):
            pv = jnp.dot(vt_ref[:, :keys], probs.pop((g, c)), preferred_element_type=F32)
            maps.append(pv[:HEAD_W] * (1.0 / pv[HEAD_W:HEAD_W + 1]))
        o = maps[0] - lam * maps[1]
        ms = jnp.mean(o * o, axis=0, keepdims=True)
        on = (o * lax.rsqrt(ms + SUBLN_EPS)).T
        gate = g_ref[0, rows, :].astype(F32)
        o_ref[0, rows, :] = (((on * sg_ref[...]) * (1.0 - lambda_init)) * (gate * _sigmoid(gate))).astype(BF16)

    return stage_scores, stage_probs, stage_out


def _attn_kernel(lam_ref, sg_ref, cos_ref, sin_ref, hn_ref, *rest, gw, cb, lambda_init):
    nsec = len(rest) - 6
    w_refs = rest[:nsec]
    sb_o, df_o, proj_ref, sb_vt, df_vt, qs_ref = rest[nsec:]

    @pl.when(pl.program_id(0) == 0)
    def _():
        proj_ref[...] = jnp.zeros_like(proj_ref)

    section = lambda sec: proj_ref.at[:, :, pl.ds(sec * HEAD_W, HEAD_W)]
    sb = _sb_stages(section(SEC_SB_Q), section(SEC_SB_K), section(SEC_SB_V), section(SEC_SB_G), sb_o, sb_vt,
                    gw=gw, cb=cb)
    df = _df_stages(lam_ref, sg_ref, section(SEC_DF_Q), section(SEC_DF_K), section(SEC_DF_V), section(SEC_DF_G),
                    df_o, df_vt, qs_ref, gw=gw, lambda_init=lambda_init)
    schedule = ((0, sb[0]), (0, df[0]), (1, sb[1]), (1, df[1]), (2, sb[2]), (2, df[2]), (3, sb[3]), (4, sb[4]))
    ngroups = proj_ref.shape[1] // gw
    npieces = PROJ_ROW_SPLIT * (proj_ref.shape[2] // (2 * HEAD_W))
    assert npieces + PROJ_FIRST_STEP <= ngroups + max(lag for lag, _ in schedule)
    next_pair = []
    for step in range(ngroups + max(lag for lag, _ in schedule)):
        for lag, stage in schedule:
            if 0 <= step - lag < ngroups:
                stage(step - lag)
        if PROJ_FIRST_STEP <= step < npieces + PROJ_FIRST_STEP:
            next_pair.append(_proj_piece(hn_ref, w_refs, cos_ref, sin_ref, step - PROJ_FIRST_STEP))
    for r0, nrows, c0, block in next_pair:
        proj_ref[0, r0:r0 + nrows, c0:c0 + 2 * HEAD_W] = block


def _attention(hn3d, w, lam_params, subln_g, lambda_init):
    b, s, d = hn3d.shape
    nheads = SEC_W // HEAD_W
    nsec = w.shape[1] // SEC_W
    head_cols = nsec * HEAD_W
    assert SB_HEADS == DF_HEADS == nheads and nsec == 8 and s % ATTN_GW == 0
    npairs = b * nheads
    cos, sin = _rope_tables(s)
    subln_g = subln_g.reshape(1, HEAD_W)
    whole = lambda a: pl.BlockSpec(a.shape, lambda t: (0,) * a.ndim, pipeline_mode=pl.Buffered(1))
    attended = lambda t: jnp.maximum(t - 1, 0)
    projected = lambda t: jnp.minimum(t, npairs - 1)
    out_block = pl.BlockSpec((1, s, HEAD_W), lambda t: (attended(t) % b, 0, attended(t) // b))
    out_shape = jax.ShapeDtypeStruct((b, s, nheads * HEAD_W), BF16)
    kern = functools.partial(_attn_kernel, gw=ATTN_GW, cb=SB_CUMSUM_ROWS, lambda_init=lambda_init)
    return pl.pallas_call(
        kern,
        grid=(npairs + 1,),
        in_specs=[
            whole(lam_params), whole(subln_g), whole(cos), whole(sin),
            pl.BlockSpec((1, s, d), lambda t: (projected(t) % b, 0, 0)),
        ] + [pl.BlockSpec((d, HEAD_W), lambda t, sec=sec: (0, sec * nheads + projected(t) // b),
                          pipeline_mode=pl.Buffered(1)) for sec in range(nsec)],
        out_specs=[out_block, out_block],
        out_shape=[out_shape, out_shape],
        scratch_shapes=[pltpu.VMEM((1, s, head_cols), BF16),
                        pltpu.VMEM((HEAD_W, s), BF16),
                        pltpu.VMEM((HEAD_W + BF16_SUBLANES, s), BF16),
                        pltpu.VMEM((2, s, HEAD_W), BF16)],
        compiler_params=pltpu.CompilerParams(dimension_semantics=("arbitrary",), vmem_limit_bytes=ATTN_VMEM_LIMIT),
        name="attention",
    )(lam_params, subln_g, cos, sin, hn3d, *([w] * nsec))


def _tail_kernel(x_ref, msb_ref, mdf_ref, p_ref, wo_sb_ref, wo_df_ref, gp_ref, wg_ref, wp_ref, gf_ref, o_ref,
                 *, final_norm):
    d = x_ref.shape[1]
    nchunks = d // TAIL_CHUNK
    cols = lambda n: slice(n * TAIL_CHUNK, (n + 1) * TAIL_CHUNK)

    def pipelined(matmul, epilogue):
        nxt = matmul(0)
        for n in range(nchunks):
            acc, nxt = nxt, (matmul(n + 1) if n + 1 < nchunks else None)
            epilogue(acc, n)

    def rms_scale(sum_sq):
        return lax.rsqrt(sum_sq * (1.0 / d) + NORM_EPS)

    h, sum_sq = [None] * nchunks, []

    def out_proj(n):
        return (jnp.dot(msb_ref[...], wo_sb_ref[:, cols(n)], preferred_element_type=F32)
                + jnp.dot(mdf_ref[...], wo_df_ref[:, cols(n)], preferred_element_type=F32))

    def residual(acc, n):
        h[n] = x_ref[:, cols(n)] + acc
        sum_sq.append(jnp.sum(h[n] * h[n], axis=-1, keepdims=True))

    pipelined(out_proj, residual)
    scale = rms_scale(sum(sum_sq))
    hn = jnp.concatenate([((h[n] * scale) * gp_ref[:, cols(n)]).astype(BF16) for n in range(nchunks)], axis=1)

    p_bf16 = p_ref[...].astype(BF16)
    sum_sq = []

    def gate_proj(n):
        return (jnp.dot(hn, wg_ref[:, cols(n)], preferred_element_type=F32),
                jnp.dot(p_bf16, wp_ref[:, cols(n)], preferred_element_type=F32))

    def gated_add(acc, n):
        gate, pp = acc
        h[n] = h[n] + _sigmoid(gate) * pp
        sum_sq.append(jnp.sum(h[n] * h[n], axis=-1, keepdims=True))

    pipelined(gate_proj, gated_add)
    if final_norm:
        scale = rms_scale(sum(sum_sq))
        for n in range(nchunks):
            o_ref[:, cols(n)] = (h[n] * scale) * gf_ref[:, cols(n)]
    else:
        for n in range(nchunks):
            o_ref[:, cols(n)] = h[n]


def _tail(x2d, msb, mdf, p2d, w_out, g_ple, w_gate, w_ple, g_final, final_norm):
    t, d = x2d.shape
    tm = TAIL_TM
    assert t % tm == 0 and d % TAIL_CHUNK == 0
    dsb, ddf, dp = msb.shape[1], mdf.shape[1], p2d.shape[1]
    wo_sb, wo_df = w_out[:dsb], w_out[dsb:]
    rows = lambda width: pl.BlockSpec((tm, width), lambda i: (i, 0))
    whole = lambda a: pl.BlockSpec(a.shape, lambda i: (0, 0), pipeline_mode=pl.Buffered(1))
    g_ple, g_final = g_ple.reshape(1, d), g_final.reshape(1, d)
    return pl.pallas_call(
        functools.partial(_tail_kernel, final_norm=final_norm),
        grid=(t // tm,),
        in_specs=[rows(d), rows(dsb), rows(ddf), rows(dp),
                  whole(wo_sb), whole(wo_df), whole(g_ple), whole(w_gate), whole(w_ple), whole(g_final)],
        out_specs=rows(d),
        out_shape=jax.ShapeDtypeStruct((t, d), F32),
        compiler_params=pltpu.CompilerParams(
            dimension_semantics=("arbitrary",), vmem_limit_bytes=VMEM_LIMIT),
        name="tail",
    )(x2d, msb, mdf, p2d, wo_sb, wo_df, g_ple, w_gate, w_ple, g_final)


def kernel(x, p, norm_mix_g, w_in, lambda_q1, lambda_k1, lambda_q2, lambda_k2, subln_g, w_out, norm_ple_g,
           w_ple_gate, w_ple_proj, norm_final_g):
    b, s, d = x.shape
    depth = w_in.shape[0]
    h = x.reshape(b * s, d)
    for i in range(depth):
        lambda_init = 0.8 - 0.6 * math.exp(-0.3 * i)
        hn = _norm(h, norm_mix_g[i]).reshape(b, s, d)
        lam_params = jnp.stack([lambda_q1[i], lambda_k1[i], lambda_q2[i], lambda_k2[i]])
        msb, mdf = _attention(hn, w_in[i], lam_params, subln_g[i], lambda_init)
        msb, mdf = msb.reshape(b * s, -1), mdf.reshape(b * s, -1)
        h = _tail(h, msb, mdf, p[i].reshape(b * s, -1), w_out[i].astype(BF16), norm_ple_g[i],
                  w_ple_gate[i].astype(BF16), w_ple_proj[i].astype(BF16), norm_final_g,
                  final_norm=(i == depth - 1))
    return h.reshape(b, s, d)
```

```python
import functools
import math

import jax
import jax.numpy as jnp
from jax import lax
from jax.experimental import pallas as pl
from jax.experimental.pallas import tpu as pltpu

F32 = jnp.float32
BF16 = jnp.bfloat16

SB_HEADS = 8
DF_HEADS = 8
HEAD_W = 128
BF16_SUBLANES = 16
DF_QK_DIM = 64
ROPE_DIM = DF_QK_DIM // 4
ROPE_THETA = 500000.0
NORM_EPS = 1e-6
SUBLN_EPS = 1e-5

SEC_W = 1024
SEC_SB_Q, SEC_SB_K, SEC_SB_V, SEC_SB_G, SEC_DF_Q, SEC_DF_K, SEC_DF_V, SEC_DF_G = range(8)

NORM_TM = 1024
PROJ_ROW_SPLIT = 2
PROJ_FIRST_STEP = 2
ATTN_GW = 256
SB_CUMSUM_ROWS = 128
TAIL_TM = 512
TAIL_CHUNK = 256
VMEM_LIMIT = 56 * 1024 * 1024
ATTN_VMEM_LIMIT = 60 * 1024 * 1024

_NT = (((1,), (1,)), ((), ()))
_LOG2E = math.log2(math.e)
SB_Q_SCALE = HEAD_W ** -0.5 * _LOG2E
DF_Q_SCALE = DF_QK_DIM ** -0.5 * _LOG2E


def _sigmoid(x):
    return 1.0 / (1.0 + jnp.exp(-x))


def _norm_kernel(x_ref, g_ref, o_ref):
    x = x_ref[...]
    ms = jnp.mean(x * x, axis=-1, keepdims=True)
    o_ref[...] = ((x * lax.rsqrt(ms + NORM_EPS)) * g_ref[...]).astype(BF16)


def _norm(x2d, g):
    t, d = x2d.shape
    assert t % NORM_TM == 0
    return pl.pallas_call(
        _norm_kernel,
        grid=(t // NORM_TM,),
        in_specs=[pl.BlockSpec((NORM_TM, d), lambda i: (i, 0)), pl.BlockSpec((1, d), lambda i: (0, 0))],
        out_specs=pl.BlockSpec((NORM_TM, d), lambda i: (i, 0)),
        out_shape=jax.ShapeDtypeStruct((t, d), BF16),
        compiler_params=pltpu.CompilerParams(dimension_semantics=("arbitrary",), vmem_limit_bytes=VMEM_LIMIT),
        name="norm",
    )(x2d, g.reshape(1, d))


def _rope_tables(seq):
    half = ROPE_DIM // 2
    inv_freq = ROPE_THETA ** (-jnp.arange(0, ROPE_DIM, 2, dtype=F32) / ROPE_DIM)
    ang = jnp.arange(seq, dtype=F32)[:, None] * inv_freq[None, :]
    cos, sin = jnp.cos(ang), jnp.sin(ang)
    lane = jnp.arange(HEAD_W)
    within = lane % DF_QK_DIM
    freq = within % half
    first = within < half
    second = jnp.logical_and(within >= half, within < ROPE_DIM)
    cos_l, sin_l = cos[:, freq], sin[:, freq]
    c = jnp.where(jnp.logical_or(first, second)[None, :], cos_l, 1.0)
    sn = jnp.where(first[None, :], sin_l, jnp.where(second[None, :], -sin_l, 0.0))
    return c.astype(F32), sn.astype(F32)


def _proj_piece(hn_ref, w_refs, cos_ref, sin_ref, piece):
    nrows = hn_ref.shape[1] // PROJ_ROW_SPLIT
    r0 = (piece % PROJ_ROW_SPLIT) * nrows
    pair = piece // PROJ_ROW_SPLIT
    c0 = pair * 2 * HEAD_W
    w_pair = jnp.concatenate([w_refs[2 * pair][...], w_refs[2 * pair + 1][...]], axis=1).astype(BF16)
    acc = jnp.dot(hn_ref[0, r0:r0 + nrows, :], w_pair, preferred_element_type=F32)
    blocks = []
    for k in range(2):
        sec = 2 * pair + k
        xh = acc[:, k * HEAD_W:(k + 1) * HEAD_W]
        if sec in (SEC_DF_Q, SEC_DF_K):
            c, s = cos_ref[r0:r0 + nrows, :], sin_ref[r0:r0 + nrows, :]
            if sec == SEC_DF_Q:
                c, s = c * DF_Q_SCALE, s * DF_Q_SCALE
            lane = lax.broadcasted_iota(jnp.int32, c.shape, 1)
            partner = jnp.where(lane % DF_QK_DIM < ROPE_DIM, lane ^ (ROPE_DIM // 2), lane)
            xh = xh * c + jnp.take_along_axis(xh * s, partner, axis=1)
        elif sec == SEC_SB_Q:
            xh = xh * SB_Q_SCALE
        blocks.append(xh.astype(BF16))
    return r0, nrows, c0, jnp.concatenate(blocks, axis=1)


def _fill_vt(v_ref, vt_ref, chunk):
    for lo in range(0, vt_ref.shape[1], chunk):
        vt_ref[:HEAD_W, lo:lo + chunk] = v_ref[0, lo:lo + chunk, :].astype(F32).T.astype(BF16)
    if vt_ref.shape[0] > HEAD_W:
        vt_ref[HEAD_W:, :] = jnp.ones((vt_ref.shape[0] - HEAD_W, vt_ref.shape[1]), BF16)


def _sb_stages(q_ref, k_ref, v_ref, g_ref, o_ref, vt_ref, *, gw, cb):
    _fill_vt(v_ref, vt_ref, gw)

    row = lax.broadcasted_iota(jnp.int32, (cb, 2 * cb), 0)
    col = lax.broadcasted_iota(jnp.int32, (cb, 2 * cb), 1) % cb
    suffix = jnp.where(col >= row, 1.0, 0.0).astype(BF16)
    key_pos = lax.broadcasted_iota(jnp.int32, (cb, gw), 0)
    qry_pos = lax.broadcasted_iota(jnp.int32, (cb, gw), 1)
    strict = [key_pos + r * cb < qry_pos for r in range(gw // cb)]

    per_group = gw // cb
    logits, splits, sums, weights = {}, {}, {}, {}

    def diag_mask(g, b):
        return strict[b - g * per_group] if b >= g * per_group else None

    def stage_logits(g):
        logits[g] = lax.dot_general(k_ref[0, :(g + 1) * gw, :], q_ref[0, g * gw:(g + 1) * gw, :], _NT,
                                    preferred_element_type=F32)

    def stage_split(g):
        y = logits[g]
        sp = jnp.maximum(y, 0.0) + jnp.log2(1.0 + jnp.exp2(-jnp.abs(y)))
        parts = []
        for b in range((g + 1) * per_group):
            sp_b = sp[b * cb:(b + 1) * cb]
            if diag_mask(g, b) is not None:
                sp_b = jnp.where(diag_mask(g, b), sp_b, 0.0)
            hi = sp_b.astype(BF16)
            lo = (sp_b - hi.astype(F32)).astype(BF16)
            parts.append(jnp.concatenate([hi, lo], axis=0))
        splits[g] = parts

    def stage_sums(g):
        sums[g] = [jnp.dot(suffix, part, preferred_element_type=F32) for part in splits.pop(g)]

    def stage_weights(g):
        y, block_sums = logits.pop(g), sums.pop(g)
        nblocks = (g + 1) * per_group
        w = [None] * nblocks
        carry = None
        for b in reversed(range(nblocks)):
            y_b = y[b * cb:(b + 1) * cb]
            w_b = jnp.exp2((y_b if carry is None else y_b - carry) - block_sums[b])
            if diag_mask(g, b) is not None:
                w_b = jnp.where(diag_mask(g, b), w_b, 0.0)
            w[b] = w_b.astype(BF16)
            total = block_sums[b][0:1, :]
            carry = total if carry is None else carry + total
        weights[g] = jnp.concatenate(w, axis=0)

    def stage_out(g):
        rows = slice(g * gw, (g + 1) * gw)
        out_t = jnp.dot(vt_ref[:, :(g + 1) * gw], weights.pop(g), preferred_element_type=F32)
        gate = g_ref[0, rows, :].astype(F32)
        o_ref[0, rows, :] = (out_t.T * (gate * _sigmoid(gate))).astype(BF16)

    return stage_logits, stage_split, stage_sums, stage_weights, stage_out


def _df_stages(lam_ref, sg_ref, q_ref, k_ref, v_ref, g_ref, o_ref, vt_ref, qs_ref, *, gw, lambda_init):
    seq = k_ref.shape[1]
    _fill_vt(v_ref, vt_ref, gw)

    q = q_ref[0]
    lane = lax.broadcasted_iota(jnp.int32, (seq, HEAD_W), 1)
    qs_ref[0] = jnp.where(lane < DF_QK_DIM, q, jnp.zeros_like(q))
    qs_ref[1] = jnp.where(lane >= DF_QK_DIM, q, jnp.zeros_like(q))

    lam_p = lam_ref[...]
    lam = (jnp.exp(jnp.sum(lam_p[0:1] * lam_p[1:2], axis=-1, keepdims=True))
           - jnp.exp(jnp.sum(lam_p[2:3] * lam_p[3:4], axis=-1, keepdims=True))
           + lambda_init)
    causal = lax.broadcasted_iota(jnp.int32, (gw, gw), 0) <= lax.broadcasted_iota(jnp.int32, (gw, gw), 1)
    scores, probs = {}, {}

    def stage_scores(g):
        keys = (g + 1) * gw
        for c in range(2):
            s = lax.dot_general(k_ref[0, :keys, :], qs_ref[c, g * gw:(g + 1) * gw, :], _NT,
                                preferred_element_type=F32)
            s_diag = jnp.where(causal, s[keys - gw:], -jnp.inf)
            s = s_diag if g == 0 else jnp.concatenate([s[:keys - gw], s_diag], axis=0)
            scores[g, c] = (s, jnp.max(s, axis=0, keepdims=True))

    def stage_probs(g):
        for c in range(2):
            s, m = scores.pop((g, c))
            probs[g, c] = jnp.exp2(s - m).astype(BF16)

    def stage_out(g):
        keys = (g + 1) * gw
        rows = slice(g * gw, (g + 1) * gw)
        maps = []
        for c in range(2):
            pv = jnp.dot(vt_ref[:, :keys], probs.pop((g, c)), preferred_element_type=F32)
            maps.append(pv[:HEAD_W] * (1.0 / pv[HEAD_W:HEAD_W + 1]))
        o = maps[0] - lam * maps[1]
        ms = jnp.mean(o * o, axis=0, keepdims=True)
        on = (o * lax.rsqrt(ms + SUBLN_EPS)).T
        gate = g_ref[0, rows, :].astype(F32)
        o_ref[0, rows, :] = (((on * sg_ref[...]) * (1.0 - lambda_init)) * (gate * _sigmoid(gate))).astype(BF16)

    return stage_scores, stage_probs, stage_out


def _attn_kernel(lam_ref, sg_ref, cos_ref, sin_ref, hn_ref, *rest, gw, cb, lambda_init):
    nsec = len(rest) - 6
    w_refs = rest[:nsec]
    sb_o, df_o, proj_ref, sb_vt, df_vt, qs_ref = rest[nsec:]

    @pl.when(pl.program_id(0) == 0)
    def _():
        proj_ref[...] = jnp.zeros_like(proj_ref)

    section = lambda sec: proj_ref.at[:, :, pl.ds(sec * HEAD_W, HEAD_W)]
    sb = _sb_stages(section(SEC_SB_Q), section(SEC_SB_K), section(SEC_SB_V), section(SEC_SB_G), sb_o, sb_vt,
                    gw=gw, cb=cb)
    df = _df_stages(lam_ref, sg_ref, section(SEC_DF_Q), section(SEC_DF_K), section(SEC_DF_V), section(SEC_DF_G),
                    df_o, df_vt, qs_ref, gw=gw, lambda_init=lambda_init)
    schedule = ((0, sb[0]), (0, df[0]), (1, sb[1]), (1, df[1]), (2, sb[2]), (2, df[2]), (3, sb[3]), (3, sb[4]))
    ngroups = proj_ref.shape[1] // gw
    npieces = PROJ_ROW_SPLIT * (proj_ref.shape[2] // (2 * HEAD_W))
    assert npieces + PROJ_FIRST_STEP <= ngroups + max(lag for lag, _ in schedule)
    next_pair = []
    for step in range(ngroups + max(lag for lag, _ in schedule)):
        for lag, stage in schedule:
            if 0 <= step - lag < ngroups:
                stage(step - lag)
        if PROJ_FIRST_STEP <= step < npieces + PROJ_FIRST_STEP:
            next_pair.append(_proj_piece(hn_ref, w_refs, cos_ref, sin_ref, step - PROJ_FIRST_STEP))
    for r0, nrows, c0, block in next_pair:
        proj_ref[0, r0:r0 + nrows, c0:c0 + 2 * HEAD_W] = block


def _attention(hn3d, w, lam_params, subln_g, lambda_init):
    b, s, d = hn3d.shape
    nheads = SEC_W // HEAD_W
    nsec = w.shape[1] // SEC_W
    head_cols = nsec * HEAD_W
    assert SB_HEADS == DF_HEADS == nheads and nsec == 8 and s % ATTN_GW == 0
    npairs = b * nheads
    cos, sin = _rope_tables(s)
    subln_g = subln_g.reshape(1, HEAD_W)
    whole = lambda a: pl.BlockSpec(a.shape, lambda t: (0,) * a.ndim, pipeline_mode=pl.Buffered(1))
    attended = lambda t: jnp.maximum(t - 1, 0)
    projected = lambda t: jnp.minimum(t, npairs - 1)
    out_block = pl.BlockSpec((1, s, HEAD_W), lambda t: (attended(t) % b, 0, attended(t) // b))
    out_shape = jax.ShapeDtypeStruct((b, s, nheads * HEAD_W), BF16)
    kern = functools.partial(_attn_kernel, gw=ATTN_GW, cb=SB_CUMSUM_ROWS, lambda_init=lambda_init)
    return pl.pallas_call(
        kern,
        grid=(npairs + 1,),
        in_specs=[
            whole(lam_params), whole(subln_g), whole(cos), whole(sin),
            pl.BlockSpec((1, s, d), lambda t: (projected(t) % b, 0, 0)),
        ] + [pl.BlockSpec((d, HEAD_W), lambda t, sec=sec: (0, sec * nheads + projected(t) // b),
                          pipeline_mode=pl.Buffered(1)) for sec in range(nsec)],
        out_specs=[out_block, out_block],
        out_shape=[out_shape, out_shape],
        scratch_shapes=[pltpu.VMEM((1, s, head_cols), BF16),
                        pltpu.VMEM((HEAD_W, s), BF16),
                        pltpu.VMEM((HEAD_W + BF16_SUBLANES, s), BF16),
                        pltpu.VMEM((2, s, HEAD_W), BF16)],
        compiler_params=pltpu.CompilerParams(dimension_semantics=("arbitrary",), vmem_limit_bytes=ATTN_VMEM_LIMIT),
        name="attention",
    )(lam_params, subln_g, cos, sin, hn3d, *([w] * nsec))


def _tail_kernel(x_ref, msb_ref, mdf_ref, p_ref, wo_sb_ref, wo_df_ref, gp_ref, wg_ref, wp_ref, gf_ref, o_ref,
                 *, final_norm):
    d = x_ref.shape[1]
    nchunks = d // TAIL_CHUNK
    cols = lambda n: slice(n * TAIL_CHUNK, (n + 1) * TAIL_CHUNK)

    def pipelined(matmul, epilogue):
        nxt = matmul(0)
        for n in range(nchunks):
            acc, nxt = nxt, (matmul(n + 1) if n + 1 < nchunks else None)
            epilogue(acc, n)

    def rms_scale(sum_sq):
        return lax.rsqrt(sum_sq * (1.0 / d) + NORM_EPS)

    h, sum_sq = [None] * nchunks, []

    def out_proj(n):
        return (jnp.dot(msb_ref[...], wo_sb_ref[:, cols(n)], preferred_element_type=F32)
                + jnp.dot(mdf_ref[...], wo_df_ref[:, cols(n)], preferred_element_type=F32))

    def residual(acc, n):
        h[n] = x_ref[:, cols(n)] + acc
        sum_sq.append(jnp.sum(h[n] * h[n], axis=-1, keepdims=True))

    pipelined(out_proj, residual)
    scale = rms_scale(sum(sum_sq))
    hn = jnp.concatenate([((h[n] * scale) * gp_ref[:, cols(n)]).astype(BF16) for n in range(nchunks)], axis=1)

    p_bf16 = p_ref[...].astype(BF16)
    sum_sq = []

    def gate_proj(n):
        return (jnp.dot(hn, wg_ref[:, cols(n)], preferred_element_type=F32),
                jnp.dot(p_bf16, wp_ref[:, cols(n)], preferred_element_type=F32))

    def gated_add(acc, n):
        gate, pp = acc
        h[n] = h[n] + _sigmoid(gate) * pp
        sum_sq.append(jnp.sum(h[n] * h[n], axis=-1, keepdims=True))

    pipelined(gate_proj, gated_add)
    if final_norm:
        scale = rms_scale(sum(sum_sq))
        for n in range(nchunks):
            o_ref[:, cols(n)] = (h[n] * scale) * gf_ref[:, cols(n)]
    else:
        for n in range(nchunks):
            o_ref[:, cols(n)] = h[n]


def _tail(x2d, msb, mdf, p2d, w_out, g_ple, w_gate, w_ple, g_final, final_norm):
    t, d = x2d.shape
    tm = TAIL_TM
    assert t % tm == 0 and d % TAIL_CHUNK == 0
    dsb, ddf, dp = msb.shape[1], mdf.shape[1], p2d.shape[1]
    wo_sb, wo_df = w_out[:dsb], w_out[dsb:]
    rows = lambda width: pl.BlockSpec((tm, width), lambda i: (i, 0))
    whole = lambda a: pl.BlockSpec(a.shape, lambda i: (0, 0), pipeline_mode=pl.Buffered(1))
    g_ple, g_final = g_ple.reshape(1, d), g_final.reshape(1, d)
    return pl.pallas_call(
        functools.partial(_tail_kernel, final_norm=final_norm),
        grid=(t // tm,),
        in_specs=[rows(d), rows(dsb), rows(ddf), rows(dp),
                  whole(wo_sb), whole(wo_df), whole(g_ple), whole(w_gate), whole(w_ple), whole(g_final)],
        out_specs=rows(d),
        out_shape=jax.ShapeDtypeStruct((t, d), F32),
        compiler_params=pltpu.CompilerParams(
            dimension_semantics=("arbitrary",), vmem_limit_bytes=VMEM_LIMIT),
        name="tail",
    )(x2d, msb, mdf, p2d, wo_sb, wo_df, g_ple, w_gate, w_ple, g_final)


def kernel(x, p, norm_mix_g, w_in, lambda_q1, lambda_k1, lambda_q2, lambda_k2, subln_g, w_out, norm_ple_g,
           w_ple_gate, w_ple_proj, norm_final_g):
    b, s, d = x.shape
    depth = w_in.shape[0]
    h = x.reshape(b * s, d)
    for i in range(depth):
        lambda_init = 0.8 - 0.6 * math.exp(-0.3 * i)
        hn = _norm(h, norm_mix_g[i]).reshape(b, s, d)
        lam_params = jnp.stack([lambda_q1[i], lambda_k1[i], lambda_q2[i], lambda_k2[i]])
        msb, mdf = _attention(hn, w_in[i], lam_params, subln_g[i], lambda_init)
        msb, mdf = msb.reshape(b * s, -1), mdf.reshape(b * s, -1)
        h = _tail(h, msb, mdf, p[i].reshape(b * s, -1), w_out[i].astype(BF16), norm_ple_g[i],
                  w_ple_gate[i].astype(BF16), w_ple_proj[i].astype(BF16), norm_final_g,
                  final_norm=(i == depth - 1))
    return h.reshape(b, s, d)
```

```python
import functools
import math

import jax
import jax.numpy as jnp
from jax import lax
from jax.experimental import pallas as pl
from jax.experimental.pallas import tpu as pltpu

F32 = jnp.float32
BF16 = jnp.bfloat16

SB_HEADS = 8
DF_HEADS = 8
HEAD_W = 128
BF16_SUBLANES = 16
DF_QK_DIM = 64
ROPE_DIM = DF_QK_DIM // 4
ROPE_THETA = 500000.0
NORM_EPS = 1e-6
SUBLN_EPS = 1e-5

SEC_W = 1024
SEC_SB_Q, SEC_SB_K, SEC_SB_V, SEC_SB_G, SEC_DF_Q, SEC_DF_K, SEC_DF_V, SEC_DF_G = range(8)

NORM_TM = 1024
PROJ_ROW_SPLIT = 2
PROJ_FIRST_STEP = 2
ATTN_GW = 256
SB_CUMSUM_ROWS = 128
TAIL_TM = 512
TAIL_CHUNK = 256
VMEM_LIMIT = 56 * 1024 * 1024
ATTN_VMEM_LIMIT = 60 * 1024 * 1024

_NT = (((1,), (1,)), ((), ()))
_LOG2E = math.log2(math.e)
SB_Q_SCALE = HEAD_W ** -0.5 * _LOG2E
DF_Q_SCALE = DF_QK_DIM ** -0.5 * _LOG2E


def _sigmoid(x):
    return 1.0 / (1.0 + jnp.exp(-x))


def _norm_kernel(x_ref, g_ref, o_ref):
    x = x_ref[...]
    ms = jnp.mean(x * x, axis=-1, keepdims=True)
    o_ref[...] = ((x * lax.rsqrt(ms + NORM_EPS)) * g_ref[...]).astype(BF16)


def _norm(x2d, g):
    t, d = x2d.shape
    assert t % NORM_TM == 0
    return pl.pallas_call(
        _norm_kernel,
        grid=(t // NORM_TM,),
        in_specs=[pl.BlockSpec((NORM_TM, d), lambda i: (i, 0)), pl.BlockSpec((1, d), lambda i: (0, 0))],
        out_specs=pl.BlockSpec((NORM_TM, d), lambda i: (i, 0)),
        out_shape=jax.ShapeDtypeStruct((t, d), BF16),
        compiler_params=pltpu.CompilerParams(dimension_semantics=("arbitrary",), vmem_limit_bytes=VMEM_LIMIT),
        name="norm",
    )(x2d, g.reshape(1, d))


def _rope_tables(seq):
    half = ROPE_DIM // 2
    inv_freq = ROPE_THETA ** (-jnp.arange(0, ROPE_DIM, 2, dtype=F32) / ROPE_DIM)
    ang = jnp.arange(seq, dtype=F32)[:, None] * inv_freq[None, :]
    cos, sin = jnp.cos(ang), jnp.sin(ang)
    lane = jnp.arange(HEAD_W)
    within = lane % DF_QK_DIM
    freq = within % half
    first = within < half
    second = jnp.logical_and(within >= half, within < ROPE_DIM)
    cos_l, sin_l = cos[:, freq], sin[:, freq]
    c = jnp.where(jnp.logical_or(first, second)[None, :], cos_l, 1.0)
    sn = jnp.where(first[None, :], sin_l, jnp.where(second[None, :], -sin_l, 0.0))
    return c.astype(F32), sn.astype(F32)


def _proj_piece(hn_ref, w_refs, cos_ref, sin_ref, piece):
    nrows = hn_ref.shape[1] // PROJ_ROW_SPLIT
    r0 = (piece % PROJ_ROW_SPLIT) * nrows
    pair = piece // PROJ_ROW_SPLIT
    c0 = pair * 2 * HEAD_W
    w_pair = jnp.concatenate([w_refs[2 * pair][...], w_refs[2 * pair + 1][...]], axis=1).astype(BF16)
    acc = jnp.dot(hn_ref[0, r0:r0 + nrows, :], w_pair, preferred_element_type=F32)
    blocks = []
    for k in range(2):
        sec = 2 * pair + k
        xh = acc[:, k * HEAD_W:(k + 1) * HEAD_W]
        if sec in (SEC_DF_Q, SEC_DF_K):
            c, s = cos_ref[r0:r0 + nrows, :], sin_ref[r0:r0 + nrows, :]
            if sec == SEC_DF_Q:
                c, s = c * DF_Q_SCALE, s * DF_Q_SCALE
            lane = lax.broadcasted_iota(jnp.int32, c.shape, 1)
            partner = jnp.where(lane % DF_QK_DIM < ROPE_DIM, lane ^ (ROPE_DIM // 2), lane)
            xh = xh * c + jnp.take_along_axis(xh * s, partner, axis=1)
        elif sec == SEC_SB_Q:
            xh = xh * SB_Q_SCALE
        blocks.append(xh.astype(BF16))
    return r0, nrows, c0, jnp.concatenate(blocks, axis=1)


def _fill_vt(v_ref, vt_ref, chunk):
    for lo in range(0, vt_ref.shape[1], chunk):
        vt_ref[:HEAD_W, lo:lo + chunk] = v_ref[0, lo:lo + chunk, :].astype(F32).T.astype(BF16)
    if vt_ref.shape[0] > HEAD_W:
        vt_ref[HEAD_W:, :] = jnp.ones((vt_ref.shape[0] - HEAD_W, vt_ref.shape[1]), BF16)


def _sb_stages(q_ref, k_ref, v_ref, g_ref, o_ref, vt_ref, *, gw, cb):
    _fill_vt(v_ref, vt_ref, gw)

    row = lax.broadcasted_iota(jnp.int32, (cb, 2 * cb), 0)
    col = lax.broadcasted_iota(jnp.int32, (cb, 2 * cb), 1) % cb
    suffix = jnp.where(col >= row, 1.0, 0.0).astype(BF16)
    key_pos = lax.broadcasted_iota(jnp.int32, (cb, gw), 0)
    qry_pos = lax.broadcasted_iota(jnp.int32, (cb, gw), 1)
    strict = [key_pos + r * cb < qry_pos for r in range(gw // cb)]

    per_group = gw // cb
    logits, splits, sums, weights = {}, {}, {}, {}

    def diag_mask(g, b):
        return strict[b - g * per_group] if b >= g * per_group else None

    def stage_logits(g):
        logits[g] = lax.dot_general(k_ref[0, :(g + 1) * gw, :], q_ref[0, g * gw:(g + 1) * gw, :], _NT,
                                    preferred_element_type=F32)

    def stage_split(g):
        y = logits[g]
        sp = jnp.maximum(y, 0.0) + jnp.log2(1.0 + jnp.exp2(-jnp.abs(y)))
        parts = []
        for b in range((g + 1) * per_group):
            sp_b = sp[b * cb:(b + 1) * cb]
            if diag_mask(g, b) is not None:
                sp_b = jnp.where(diag_mask(g, b), sp_b, 0.0)
            hi = sp_b.astype(BF16)
            lo = (sp_b - hi.astype(F32)).astype(BF16)
            parts.append(jnp.concatenate([hi, lo], axis=0))
        splits[g] = parts

    def stage_sums(g):
        sums[g] = [jnp.dot(suffix, part, preferred_element_type=F32) for part in splits.pop(g)]

    def stage_weights(g):
        y, block_sums = logits.pop(g), sums.pop(g)
        nblocks = (g + 1) * per_group
        w = [None] * nblocks
        carry = None
        for b in reversed(range(nblocks)):
            y_b = y[b * cb:(b + 1) * cb]
            w_b = jnp.exp2((y_b if carry is None else y_b - carry) - block_sums[b])
            if diag_mask(g, b) is not None:
                w_b = jnp.where(diag_mask(g, b), w_b, 0.0)
            w[b] = w_b.astype(BF16)
            total = block_sums[b][0:1, :]
            carry = total if carry is None else carry + total
        weights[g] = jnp.concatenate(w, axis=0)

    def stage_out(g):
        rows = slice(g * gw, (g + 1) * gw)
        out_t = jnp.dot(vt_ref[:, :(g + 1) * gw], weights.pop(g), preferred_element_type=F32)
        gate = g_ref[0, rows, :].astype(F32)
        o_ref[0, rows, :] = (out_t.T * (gate * _sigmoid(gate))).astype(BF16)

    return stage_logits, stage_split, stage_sums, stage_weights, stage_out


def _df_stages(lam_ref, sg_ref, q_ref, k_ref, v_ref, g_ref, o_ref, vt_ref, qs_ref, *, gw, lambda_init):
    seq = k_ref.shape[1]
    _fill_vt(v_ref, vt_ref, gw)

    q = q_ref[0]
    lane = lax.broadcasted_iota(jnp.int32, (seq, HEAD_W), 1)
    qs_ref[0] = jnp.where(lane < DF_QK_DIM, q, jnp.zeros_like(q))
    qs_ref[1] = jnp.where(lane >= DF_QK_DIM, q, jnp.zeros_like(q))

    lam_p = lam_ref[...]
    lam = (jnp.exp(jnp.sum(lam_p[0:1] * lam_p[1:2], axis=-1, keepdims=True))
           - jnp.exp(jnp.sum(lam_p[2:3] * lam_p[3:4], axis=-1, keepdims=True))
           + lambda_init)
    causal = lax.broadcasted_iota(jnp.int32, (gw, gw), 0) <= lax.broadcasted_iota(jnp.int32, (gw, gw), 1)
    scores, probs = {}, {}

    def stage_scores(g):
        keys = (g + 1) * gw
        for c in range(2):
            s = lax.dot_general(k_ref[0, :keys, :], qs_ref[c, g * gw:(g + 1) * gw, :], _NT,
                                preferred_element_type=F32)
            s_diag = jnp.where(causal, s[keys - gw:], -jnp.inf)
            s = s_diag if g == 0 else jnp.concatenate([s[:keys - gw], s_diag], axis=0)
            scores[g, c] = (s, jnp.max(s, axis=0, keepdims=True))

    def stage_probs(g):
        for c in range(2):
            s, m = scores.pop((g, c))
            probs[g, c] = jnp.exp2(s - m).astype(BF16)

    def stage_out(g):
        keys = (g + 1) * gw
        rows = slice(g * gw, (g + 1) * gw)
        maps = []
        for c in range(2):
            pv = jnp.dot(vt_ref[:, :keys], probs.pop((g, c)), preferred_element_type=F32)
            maps.append(pv[:HEAD_W] * (1.0 / pv[HEAD_W:HEAD_W + 1]))
        o = maps[0] - lam * maps[1]
        ms = jnp.mean(o * o, axis=0, keepdims=True)
        on = (o * lax.rsqrt(ms + SUBLN_EPS)).T
        gate = g_ref[0, rows, :].astype(F32)
        o_ref[0, rows, :] = (((on * sg_ref[...]) * (1.0 - lambda_init)) * (gate * _sigmoid(gate))).astype(BF16)

    return stage_scores, stage_probs, stage_out


def _attn_kernel(lam_ref, sg_ref, cos_ref, sin_ref, hn_ref, *rest, gw, cb, lambda_init):
    nsec = len(rest) - 6
    w_refs = rest[:nsec]
    sb_o, df_o, proj_ref, sb_vt, df_vt, qs_ref = rest[nsec:]

    @pl.when(pl.program_id(0) == 0)
    def _():
        proj_ref[...] = jnp.zeros_like(proj_ref)

    section = lambda sec: proj_ref.at[:, :, pl.ds(sec * HEAD_W, HEAD_W)]
    sb = _sb_stages(section(SEC_SB_Q), section(SEC_SB_K), section(SEC_SB_V), section(SEC_SB_G), sb_o, sb_vt,
                    gw=gw, cb=cb)
    df = _df_stages(lam_ref, sg_ref, section(SEC_DF_Q), section(SEC_DF_K), section(SEC_DF_V), section(SEC_DF_G),
                    df_o, df_vt, qs_ref, gw=gw, lambda_init=lambda_init)
    schedule = ((0, sb[0]), (0, df[0]), (1, sb[1]), (1, df[1]), (2, sb[2]), (2, df[2]), (3, sb[3]), (4, sb[4]))
    ngroups = proj_ref.shape[1] // gw
    npieces = PROJ_ROW_SPLIT * (proj_ref.shape[2] // (2 * HEAD_W))
    assert npieces + PROJ_FIRST_STEP <= ngroups + max(lag for lag, _ in schedule)
    next_pair = []
    for step in range(ngroups + max(lag for lag, _ in schedule)):
        for lag, stage in schedule:
            if 0 <= step - lag < ngroups:
                stage(step - lag)
        if PROJ_FIRST_STEP <= step < npieces + PROJ_FIRST_STEP:
            next_pair.append(_proj_piece(hn_ref, w_refs, cos_ref, sin_ref, step - PROJ_FIRST_STEP))
    for r0, nrows, c0, block in next_pair:
        proj_ref[0, r0:r0 + nrows, c0:c0 + 2 * HEAD_W] = block


def _attention(hn3d, w, lam_params, subln_g, lambda_init):
    b, s, d = hn3d.shape
    nheads = SEC_W // HEAD_W
    nsec = w.shape[1] // SEC_W
    head_cols = nsec * HEAD_W
    assert SB_HEADS == DF_HEADS == nheads and nsec == 8 and s % ATTN_GW == 0
    npairs = b * nheads
    cos, sin = _rope_tables(s)
    subln_g = subln_g.reshape(1, HEAD_W)
    whole = lambda a: pl.BlockSpec(a.shape, lambda t: (0,) * a.ndim, pipeline_mode=pl.Buffered(1))
    attended = lambda t: jnp.maximum(t - 1, 0)
    projected = lambda t: jnp.minimum(t, npairs - 1)
    out_block = pl.BlockSpec((1, s, HEAD_W), lambda t: (attended(t) % b, 0, attended(t) // b))
    out_shape = jax.ShapeDtypeStruct((b, s, nheads * HEAD_W), BF16)
    kern = functools.partial(_attn_kernel, gw=ATTN_GW, cb=SB_CUMSUM_ROWS, lambda_init=lambda_init)
    return pl.pallas_call(
        kern,
        grid=(npairs + 1,),
        in_specs=[
            whole(lam_params), whole(subln_g), whole(cos), whole(sin),
            pl.BlockSpec((1, s, d), lambda t: (projected(t) % b, 0, 0)),
        ] + [pl.BlockSpec((d, HEAD_W), lambda t, sec=sec: (0, sec * nheads + projected(t) // b),
                          pipeline_mode=pl.Buffered(1)) for sec in range(nsec)],
        out_specs=[out_block, out_block],
        out_shape=[out_shape, out_shape],
        scratch_shapes=[pltpu.VMEM((1, s, head_cols), BF16),
                        pltpu.VMEM((HEAD_W, s), BF16),
                        pltpu.VMEM((HEAD_W + BF16_SUBLANES, s), BF16),
                        pltpu.VMEM((2, s, HEAD_W), BF16)],
        compiler_params=pltpu.CompilerParams(dimension_semantics=("arbitrary",), vmem_limit_bytes=ATTN_VMEM_LIMIT),
        name="attention",
    )(lam_params, subln_g, cos, sin, hn3d, *([w] * nsec))


def _tail_kernel(x_ref, msb_ref, mdf_ref, p_ref, wo_ref, gp_ref, wg_ref, wp_ref, gf_ref, o_ref, *, final_norm):
    d = x_ref.shape[1]
    nchunks = d // TAIL_CHUNK
    cols = lambda n: slice(n * TAIL_CHUNK, (n + 1) * TAIL_CHUNK)

    def pipelined(matmul, epilogue):
        nxt = matmul(0)
        for n in range(nchunks):
            acc, nxt = nxt, (matmul(n + 1) if n + 1 < nchunks else None)
            epilogue(acc, n)

    def rms_scale(sum_sq):
        return lax.rsqrt(sum_sq * (1.0 / d) + NORM_EPS)

    h, sum_sq = [None] * nchunks, []

    mixed = jnp.concatenate([msb_ref[...], mdf_ref[...]], axis=1)

    def out_proj(n):
        return jnp.dot(mixed, wo_ref[:, cols(n)], preferred_element_type=F32)

    def residual(acc, n):
        h[n] = x_ref[:, cols(n)] + acc
        sum_sq.append(jnp.sum(h[n] * h[n], axis=-1, keepdims=True))

    pipelined(out_proj, residual)
    scale = rms_scale(sum(sum_sq))
    hn = jnp.concatenate([((h[n] * scale) * gp_ref[:, cols(n)]).astype(BF16) for n in range(nchunks)], axis=1)

    p_bf16 = p_ref[...].astype(BF16)
    sum_sq = []

    def gate_proj(n):
        return (jnp.dot(hn, wg_ref[:, cols(n)], preferred_element_type=F32),
                jnp.dot(p_bf16, wp_ref[:, cols(n)], preferred_element_type=F32))

    def gated_add(acc, n):
        gate, pp = acc
        h[n] = h[n] + _sigmoid(gate) * pp
        sum_sq.append(jnp.sum(h[n] * h[n], axis=-1, keepdims=True))

    pipelined(gate_proj, gated_add)
    if final_norm:
        scale = rms_scale(sum(sum_sq))
        for n in range(nchunks):
            o_ref[:, cols(n)] = (h[n] * scale) * gf_ref[:, cols(n)]
    else:
        for n in range(nchunks):
            o_ref[:, cols(n)] = h[n]


def _tail(x2d, msb, mdf, p2d, w_out, g_ple, w_gate, w_ple, g_final, final_norm):
    t, d = x2d.shape
    tm = TAIL_TM
    assert t % tm == 0 and d % TAIL_CHUNK == 0
    dsb, ddf, dp = msb.shape[1], mdf.shape[1], p2d.shape[1]
    rows = lambda width: pl.BlockSpec((tm, width), lambda i: (i, 0))
    whole = lambda a: pl.BlockSpec(a.shape, lambda i: (0, 0), pipeline_mode=pl.Buffered(1))
    g_ple, g_final = g_ple.reshape(1, d), g_final.reshape(1, d)
    return pl.pallas_call(
        functools.partial(_tail_kernel, final_norm=final_norm),
        grid=(t // tm,),
        in_specs=[rows(d), rows(dsb), rows(ddf), rows(dp),
                  whole(w_out), whole(g_ple), whole(w_gate), whole(w_ple), whole(g_final)],
        out_specs=rows(d),
        out_shape=jax.ShapeDtypeStruct((t, d), F32),
        compiler_params=pltpu.CompilerParams(
            dimension_semantics=("arbitrary",), vmem_limit_bytes=VMEM_LIMIT),
        name="tail",
    )(x2d, msb, mdf, p2d, w_out, g_ple, w_gate, w_ple, g_final)


def kernel(x, p, norm_mix_g, w_in, lambda_q1, lambda_k1, lambda_q2, lambda_k2, subln_g, w_out, norm_ple_g,
           w_ple_gate, w_ple_proj, norm_final_g):
    b, s, d = x.shape
    depth = w_in.shape[0]
    h = x.reshape(b * s, d)
    for i in range(depth):
        lambda_init = 0.8 - 0.6 * math.exp(-0.3 * i)
        hn = _norm(h, norm_mix_g[i]).reshape(b, s, d)
        lam_params = jnp.stack([lambda_q1[i], lambda_k1[i], lambda_q2[i], lambda_k2[i]])
        msb, mdf = _attention(hn, w_in[i], lam_params, subln_g[i], lambda_init)
        msb, mdf = msb.reshape(b * s, -1), mdf.reshape(b * s, -1)
        h = _tail(h, msb, mdf, p[i].reshape(b * s, -1), w_out[i].astype(BF16), norm_ple_g[i],
                  w_ple_gate[i].astype(BF16), w_ple_proj[i].astype(BF16), norm_final_g,
                  final_norm=(i == depth - 1))
    return h.reshape(b, s, d)
```
